```python
import jax, jax.numpy as jnp
from jax import lax
import numpy as np

D_MODEL = 2048
BATCH = 16
SEQ = 256
DEPTH = 2
DEC_BATCH = 2
DEC_SEQ = 4096
PAST_LEN = 256

F32 = jnp.float32
GRID_W = 64
ATT_HEAD_DIM = 128
ATT_HEADS = (D_MODEL // 2) // ATT_HEAD_DIM
ATT_KV_HEADS = ATT_HEADS // 4
ATT_GROUP = ATT_HEADS // ATT_KV_HEADS
ATT_WIDTH = ATT_HEADS * ATT_HEAD_DIM
ATT_QBLOCK = 128
ROPE_THETA = 10000.0
RET_DK = 128
RET_DV = 128
RET_HEADS = (D_MODEL // 4) // RET_DV
RET_WIDTH = RET_HEADS * RET_DV
RET_CHUNK = 128
GLA_DK = 64
GLA_DV = 128
GLA_HEADS = (D_MODEL // 4) // GLA_DV
GLA_WIDTH = GLA_HEADS * GLA_DV
GLA_GATE_RANK = 16
GLA_TAU = 16.0
GLA_CHUNK = 64
MIX_WIDTH = ATT_WIDTH + RET_WIDTH + GLA_WIDTH
PROJ_SIZES = (ATT_WIDTH, ATT_KV_HEADS * ATT_HEAD_DIM, ATT_KV_HEADS * ATT_HEAD_DIM,
              RET_HEADS * RET_DK, RET_HEADS * RET_DK, RET_WIDTH, RET_WIDTH,
              GLA_HEADS * GLA_DK, GLA_HEADS * GLA_DK, GLA_WIDTH, GLA_WIDTH, 2 * GLA_GATE_RANK)
PROJ_COLS = sum(PROJ_SIZES)
D_FF = 5632
N_MOD = 9
MACARON_WEIGHT = 0.5
DEEPNORM_ALPHA = (2 * DEPTH) ** 0.25
DEEPNORM_BETA = (8 * DEPTH) ** -0.25
LN_EPS = 1e-5
RMS_EPS = 1e-6

kernel_name = 'hybrid_diffusion_prefix_step'


def layer_norm(x, g, b):
    xf = x.astype(F32)
    mu = jnp.mean(xf, axis=-1, keepdims=True)
    var = jnp.mean(jnp.square(xf - mu), axis=-1, keepdims=True)
    return ((xf - mu) * lax.rsqrt(var + LN_EPS) * g + b).astype(x.dtype)


def rms_norm(x, g):
    xf = x.astype(F32)
    y = xf * lax.rsqrt(jnp.mean(jnp.square(xf), axis=-1, keepdims=True) + RMS_EPS)
    return (y * g).astype(x.dtype)


def head_group_norm(x):
    xf = x.astype(F32)
    mu = jnp.mean(xf, axis=-1, keepdims=True)
    var = jnp.mean(jnp.square(xf - mu), axis=-1, keepdims=True)
    return (xf - mu) * lax.rsqrt(var + LN_EPS)


def modulate(x, shift, scale):
    return x * (1.0 + scale) + shift


def swiglu_ffn(u, w_in, w_out):
    gate, up = jnp.split(u @ w_in, 2, axis=-1)
    return (jax.nn.silu(gate) * up) @ w_out


def axial_rope_tables(rows):
    row = jnp.repeat(jnp.arange(rows, dtype=F32), GRID_W)
    col = jnp.tile(jnp.arange(GRID_W, dtype=F32), rows)
    n_freq = ATT_HEAD_DIM // 4
    inv = ROPE_THETA ** (-jnp.arange(n_freq, dtype=F32) / n_freq)
    ang = jnp.concatenate([row[:, None] * inv, col[:, None] * inv], axis=-1)
    return jnp.cos(ang), jnp.sin(ang)


def apply_rope(x, cos, sin):
    xf = x.astype(F32).reshape(*x.shape[:-1], x.shape[-1] // 2, 2)
    x1, x2 = xf[..., 0], xf[..., 1]
    c = cos[None, :, None, :]
    s = sin[None, :, None, :]
    return jnp.stack([x1 * c - x2 * s, x1 * s + x2 * c], axis=-1).reshape(x.shape).astype(x.dtype)


def blocked_attention(q, k, v):
    B, S, KV, G, HD = q.shape
    nb = S // ATT_QBLOCK
    qb = q.reshape(B, nb, ATT_QBLOCK, KV, G, HD).swapaxes(0, 1)
    scale = HD ** -0.5

    def block(qblk):
        s = jnp.einsum('bqkgd,btkd->bkgqt', qblk, k).astype(F32) * scale
        p = jax.nn.softmax(s, axis=-1).astype(v.dtype)
        return jnp.einsum('bkgqt,btkd->bqkgd', p, v)

    out = lax.map(block, qb)
    return out.swapaxes(0, 1).reshape(B, S, KV * G * HD)


def retention_chunkwise(q, k, v, log_gamma, s0):
    B, S, H, DK = q.shape
    DV = v.shape[-1]
    C = RET_CHUNK
    n = S // C
    q = q.reshape(B, n, C, H, DK)
    k = k.reshape(B, n, C, H, DK)
    v = v.reshape(B, n, C, H, DV)
    idx = jnp.arange(C, dtype=F32)
    diff = idx[:, None] - idx[None, :]
    dmat = jnp.where(diff >= 0, jnp.exp(jnp.maximum(diff, 0.0)[None] * log_gamma[:, None, None]), 0.0)
    scores = jnp.einsum('bnihd,bnjhd->bnhij', q, k) * dmat
    intra = jnp.einsum('bnhij,bnjhe->bnihe', scores, v)
    k_dec = k * jnp.exp((C - 1 - idx)[:, None] * log_gamma[None, :])[:, :, None]
    kv = jnp.einsum('bnjhd,bnjhe->bnhde', k_dec, v)
    chunk_decay = jnp.exp(C * log_gamma)[None, :, None, None]

    def step(s, kv_c):
        return chunk_decay * s + kv_c, s

    s_final, s_prev = lax.scan(step, s0, kv.swapaxes(0, 1))
    q_dec = q * jnp.exp((idx + 1.0)[:, None] * log_gamma[None, :])[:, :, None]
    inter = jnp.einsum('bnihd,nbhde->bnihe', q_dec, s_prev)
    return (intra + inter).reshape(B, S, H, DV), s_final


def gla_chunkwise(q, k, v, log_a, s0):
    B, S, H, DK = q.shape
    DV = v.shape[-1]
    C = GLA_CHUNK
    n = S // C
    q = q.reshape(B, n, C, H, DK)
    k = k.reshape(B, n, C, H, DK)
    v = v.reshape(B, n, C, H, DV)
    b = jnp.cumsum(log_a.reshape(B, n, C, H, DK), axis=2)
    b_last = b[:, :, -1]
    q_t = q * jnp.exp(b)
    k_t = k * jnp.exp(-b)
    causal = jnp.tril(jnp.ones((C, C), F32))
    att = jnp.einsum('bnihd,bnjhd->bnhij', q_t, k_t) * causal
    intra = jnp.einsum('bnhij,bnjhe->bnihe', att, v)
    k_s = k * jnp.exp(b_last[:, :, None] - b)
    kv = jnp.einsum('bnjhd,bnjhe->bnhde', k_s, v)
    dec = jnp.exp(b_last)

    def step(s, inp):
        kv_c, dec_c = inp
        return dec_c[..., None] * s + kv_c, s

    s_final, s_prev = lax.scan(step, s0, (kv.swapaxes(0, 1), dec.swapaxes(0, 1)))
    inter = jnp.einsum('bnihd,nbhde->bnihe', q_t, s_prev)
    return (intra + inter).reshape(B, S, H, DV), s_final


def token_mixer(u, w_in, w_out, q_norm, k_norm, log_decay, w_a2, b_a, gla_norm, rope, ctx):
    B, S, _ = u.shape
    dt = u.dtype
    splits = [int(i) for i in np.cumsum(PROJ_SIZES)[:-1]]
    aq, ak, av, rq, rk, rv, rg, gq, gk, gv, gr, ga = jnp.split(u @ w_in, splits, axis=-1)
    aq = rms_norm(aq.reshape(B, S, ATT_HEADS, ATT_HEAD_DIM), q_norm)
    ak = rms_norm(ak.reshape(B, S, ATT_KV_HEADS, ATT_HEAD_DIM), k_norm)
    av = av.reshape(B, S, ATT_KV_HEADS, ATT_HEAD_DIM)
    rq = rq.reshape(B, S, RET_HEADS, RET_DK)
    rk = rk.reshape(B, S, RET_HEADS, RET_DK) * (RET_DK ** -0.5)
    rv = rv.reshape(B, S, RET_HEADS, RET_DV)
    if rope is not None:
        cos, sin = rope
        aq = apply_rope(aq, cos, sin)
        ak = apply_rope(ak, cos, sin)
        rq = apply_rope(rq, cos, sin)
        rk = apply_rope(rk, cos, sin)
    if ctx is None:
        keys, vals = ak, av
        s_ret0 = jnp.zeros((B, 2, RET_HEADS, RET_DK, RET_DV), F32)
        s_gla0 = jnp.zeros((B, 2, GLA_HEADS, GLA_DK, GLA_DV), F32)
    else:
        ck, cv, s_ret0, s_gla0 = ctx
        keys = jnp.concatenate([ck.astype(dt), ak], axis=1)
        vals = jnp.concatenate([cv.astype(dt), av], axis=1)
    att = blocked_attention(aq.reshape(B, S, ATT_KV_HEADS, ATT_GROUP, ATT_HEAD_DIM), keys, vals)

    rqf, rkf, rvf = rq.astype(F32), rk.astype(F32), rv.astype(F32)
    lg = log_decay.astype(F32)
    s_ret0 = s_ret0.astype(F32)
    ret_f, sr_f = retention_chunkwise(rqf, rkf, rvf, lg[0], s_ret0[:, 0])
    ret_b, sr_b = retention_chunkwise(rqf[:, ::-1], rkf[:, ::-1], rvf[:, ::-1], lg[1], s_ret0[:, 1])
    ret = head_group_norm(ret_f + ret_b[:, ::-1]).reshape(B, S, RET_WIDTH).astype(dt) * jax.nn.silu(rg)

    gqf = gq.reshape(B, S, GLA_HEADS, GLA_DK).astype(F32) * (GLA_DK ** -0.5)
    gkf = gk.reshape(B, S, GLA_HEADS, GLA_DK).astype(F32)
    gvf = gv.reshape(B, S, GLA_HEADS, GLA_DV).astype(F32)

    def gla_log_gate(d):
        pre = ga[..., d * GLA_GATE_RANK:(d + 1) * GLA_GATE_RANK] @ w_a2[d] + b_a[d]
        return (jax.nn.log_sigmoid(pre.astype(F32)) / GLA_TAU).reshape(B, S, GLA_HEADS, GLA_DK)

    s_gla0 = s_gla0.astype(F32)
    gla_f, sg_f = gla_chunkwise(gqf, gkf, gvf, gla_log_gate(0), s_gla0[:, 0])
    gla_b, sg_b = gla_chunkwise(gqf[:, ::-1], gkf[:, ::-1], gvf[:, ::-1], gla_log_gate(1)[:, ::-1], s_gla0[:, 1])
    gla = rms_norm(gla_f + gla_b[:, ::-1], gla_norm).reshape(B, S, GLA_WIDTH).astype(dt) * jax.nn.silu(gr)

    y = jnp.concatenate([att, ret, gla], axis=-1) @ w_out
    if ctx is None:
        new_ctx = (ak, av, jnp.stack([sr_f, sr_b], axis=1).astype(dt), jnp.stack([sg_f, sg_b], axis=1).astype(dt))
    else:
        new_ctx = None
    return y, new_ctx


def trunk_layer(x, mod, ln_g, ln_b, ffn_w_in, ffn_w_out, mixer_params, rope, ctx):
    m = [mod[:, i, None, :] for i in range(N_MOD)]

    def post(x, out, gate, weight, i):
        return layer_norm(DEEPNORM_ALPHA * x + weight * (gate * out), ln_g[i], ln_b[i])

    u = modulate(x, m[0], m[1])
    x = post(x, swiglu_ffn(u, ffn_w_in[0], ffn_w_out[0]), m[2], MACARON_WEIGHT, 0)
    u = modulate(x, m[3], m[4])
    y, new_ctx = token_mixer(u, *mixer_params, rope=rope, ctx=ctx)
    x = post(x, y, m[5], 1.0, 1)
    u = modulate(x, m[6], m[7])
    x = post(x, swiglu_ffn(u, ffn_w_in[1], ffn_w_out[1]), m[8], MACARON_WEIGHT, 2)
    return x, new_ctx


def setup_inputs(seed: int = 0) -> dict:
    key = jax.random.key(seed)
    ks = jax.random.split(key, 24)

    def nrm(k, shape, s=1.0):
        return s * jax.random.normal(k, shape, F32)

    ret_base = jnp.log1p(-(2.0 ** (-5.0 - jnp.arange(RET_HEADS, dtype=F32))))
    return {
        'x_prompt': nrm(ks[0], (BATCH, SEQ, D_MODEL)),
        'x_sample': nrm(ks[1], (DEC_BATCH, DEC_SEQ, D_MODEL)),
        'c': nrm(ks[2], (DEC_BATCH, D_MODEL)),
        'cache_attn_k': nrm(ks[3], (DEC_BATCH, DEPTH, PAST_LEN, ATT_KV_HEADS, ATT_HEAD_DIM)),
        'cache_attn_v': nrm(ks[4], (DEC_BATCH, DEPTH, PAST_LEN, ATT_KV_HEADS, ATT_HEAD_DIM)),
        'state_ret': nrm(ks[5], (DEC_BATCH, DEPTH, 2, RET_HEADS, RET_DK, RET_DV), 0.5),
        'state_gla': nrm(ks[6], (DEC_BATCH, DEPTH, 2, GLA_HEADS, GLA_DK, GLA_DV), 0.5),
        'c_ctx': nrm(ks[7], (D_MODEL,)),
        'w_mod': nrm(ks[8], (DEPTH, D_MODEL, N_MOD * D_MODEL), 0.5 * D_MODEL ** -0.5),
        'b_mod': nrm(ks[9], (DEPTH, N_MOD * D_MODEL), 0.01),
        'ln_g': 1.0 + nrm(ks[10], (DEPTH, 3, D_MODEL), 0.01),
        'ln_b': nrm(ks[11], (DEPTH, 3, D_MODEL), 0.01),
        'ffn_w_in': nrm(ks[12], (DEPTH, 2, D_MODEL, 2 * D_FF), D_MODEL ** -0.5),
        'ffn_w_out': nrm(ks[13], (DEPTH, 2, D_FF, D_MODEL), DEEPNORM_BETA * D_FF ** -0.5),
        'mix_w_in': nrm(ks[14], (DEPTH, D_MODEL, PROJ_COLS), D_MODEL ** -0.5),
        'mix_w_out': nrm(ks[15], (DEPTH, MIX_WIDTH, D_MODEL), DEEPNORM_BETA * MIX_WIDTH ** -0.5),
        'att_q_norm': 1.0 + nrm(ks[16], (DEPTH, ATT_HEAD_DIM), 0.01),
        'att_k_norm': 1.0 + nrm(ks[17], (DEPTH, ATT_HEAD_DIM), 0.01),
        'ret_log_decay': ret_base[None, None, :] * jnp.exp(nrm(ks[18], (DEPTH, 2, RET_HEADS), 0.1)),
        'gla_w_a2': nrm(ks[19], (DEPTH, 2, GLA_GATE_RANK, GLA_HEADS * GLA_DK), GLA_GATE_RANK ** -0.5),
        'gla_b_a': nrm(ks[20], (DEPTH, 2, GLA_HEADS * GLA_DK), 0.01),
        'gla_norm': 1.0 + nrm(ks[21], (DEPTH, GLA_DV), 0.01),
    }


def reference(x_prompt, x_sample, c, cache_attn_k, cache_attn_v, state_ret, state_gla, c_ctx,
              w_mod, b_mod, ln_g, ln_b, ffn_w_in, ffn_w_out, mix_w_in, mix_w_out,
              att_q_norm, att_k_norm, ret_log_decay, gla_w_a2, gla_b_a, gla_norm):
    h = x_prompt
    ks_l, vs_l, sr_l, sg_l = [], [], [], []
    for l in range(DEPTH):
        mod = (jax.nn.silu(c_ctx) @ w_mod[l] + b_mod[l]).reshape(1, N_MOD, D_MODEL)
        mixer_params = (mix_w_in[l], mix_w_out[l], att_q_norm[l], att_k_norm[l], ret_log_decay[l],
                        gla_w_a2[l], gla_b_a[l], gla_norm[l])
        h, (k_l, v_l, s_r, s_g) = trunk_layer(h, mod, ln_g[l], ln_b[l], ffn_w_in[l], ffn_w_out[l],
                                              mixer_params, None, None)
        ks_l.append(k_l)
        vs_l.append(v_l)
        sr_l.append(s_r)
        sg_l.append(s_g)
    y_prompt = h
    new_attn_k = jnp.stack(ks_l, axis=1)
    new_attn_v = jnp.stack(vs_l, axis=1)
    new_state_ret = jnp.stack(sr_l, axis=1)
    new_state_gla = jnp.stack(sg_l, axis=1)

    rows = x_sample.shape[1] // GRID_W
    rope = axial_rope_tables(rows)
    g = x_sample
    for l in range(DEPTH):
        mod = (jax.nn.silu(c) @ w_mod[l] + b_mod[l]).reshape(c.shape[0], N_MOD, D_MODEL)
        mixer_params = (mix_w_in[l], mix_w_out[l], att_q_norm[l], att_k_norm[l], ret_log_decay[l],
                        gla_w_a2[l], gla_b_a[l], gla_norm[l])
        ctx = (cache_attn_k[:, l], cache_attn_v[:, l], state_ret[:, l], state_gla[:, l])
        g, _ = trunk_layer(g, mod, ln_g[l], ln_b[l], ffn_w_in[l], ffn_w_out[l], mixer_params, rope, ctx)
    y_sample = g
    return (y_prompt, y_sample, new_attn_k, new_attn_v, new_state_ret, new_state_gla)
```

```python
import functools

import jax
import jax.numpy as jnp
from jax import lax
from jax.experimental import pallas as pl
from jax.experimental.pallas import tpu as pltpu

F32 = jnp.float32
BF16 = jnp.bfloat16

D_MODEL = 2048
DEPTH = 2
GRID_W = 64
ATT_HEAD_DIM = 128
ATT_HEADS = 8
ATT_KV_HEADS = 2
ATT_GROUP = ATT_HEADS // ATT_KV_HEADS
ATT_WIDTH = ATT_HEADS * ATT_HEAD_DIM
ROPE_THETA = 10000.0
RET_DK = 128
RET_DV = 128
RET_HEADS = 4
RET_WIDTH = RET_HEADS * RET_DV
RET_CHUNK = 128
GLA_DK = 64
GLA_DV = 128
GLA_HEADS = 4
GLA_PAIRS = GLA_HEADS // 2
GLA_WIDTH = GLA_HEADS * GLA_DV
GLA_GATE_RANK = 16
GLA_TAU = 16.0
GLA_CHUNK = 64
D_FF = 5632
N_MOD = 9
MACARON_WEIGHT = 0.5
DEEPNORM_ALPHA = (2 * DEPTH) ** 0.25
LN_EPS = 1e-5
RMS_EPS = 1e-6

LANES = 128
PROJ_COLS = 5152
PROJ_COLS_PADDED = 5376
PROJ_TN = 768
COL_AQ, COL_AK, COL_AV = 0, 8, 10
COL_RQ, COL_RK, COL_RV, COL_RG = 12, 16, 20, 24
COL_GQ, COL_GK, COL_GV, COL_GR, COL_GA = 28, 30, 32, 36, 40

VMEM_LIMIT_BYTES = 56 * 1024 * 1024
ROW_TILE = 512
FF_TILE = 512
MOD_TN = 1024


def _params(*sem):
    return pltpu.CompilerParams(dimension_semantics=sem, vmem_limit_bytes=VMEM_LIMIT_BYTES)


def _dot(a, b):
    return jnp.dot(a, b, preferred_element_type=F32)


def _dot_nt(a, b):
    return lax.dot_general(a, b, (((1,), (1,)), ((), ())), preferred_element_type=F32)


def _dot_tn(a, b):
    return lax.dot_general(a, b, (((0,), (0,)), ((), ())), preferred_element_type=F32)


def _silu(x):
    return x * jax.nn.sigmoid(x)


def _layer_norm(y, g, b):
    mu = jnp.mean(y, axis=-1, keepdims=True)
    d = y - mu
    var = jnp.mean(d * d, axis=-1, keepdims=True)
    return d * lax.rsqrt(var + LN_EPS) * g + b


def _rms_norm(x, g):
    return x * lax.rsqrt(jnp.mean(x * x, axis=-1, keepdims=True) + RMS_EPS) * g


def _rope(x, cos, sin_signed):
    lane = lax.broadcasted_iota(jnp.int32, x.shape, 1)
    partner = jnp.where((lane & 1) == 0, pltpu.roll(x, LANES - 1, 1), pltpu.roll(x, 1, 1))
    return x * cos + partner * sin_signed


def _mod_kernel(c_ref, w_ref, b_ref, o_ref):
    a = _silu(c_ref[...]).astype(BF16)
    o_ref[...] = _dot(a, w_ref[...].astype(BF16)) + b_ref[...]


def _modulation(cvec, w_mod, b_mod):
    L, D, N = w_mod.shape
    return pl.pallas_call(
        _mod_kernel,
        grid=(L, N // MOD_TN),
        in_specs=[
            pl.BlockSpec((8, D), lambda l, n: (0, 0)),
            pl.BlockSpec((None, D, MOD_TN), lambda l, n: (l, 0, n)),
            pl.BlockSpec((None, 1, MOD_TN), lambda l, n: (l, 0, n)),
        ],
        out_specs=pl.BlockSpec((None, 8, MOD_TN), lambda l, n: (l, 0, n)),
        out_shape=jax.ShapeDtypeStruct((L, 8, N), F32),
        compiler_params=_params("parallel", "parallel"),
        name="modulation",
    )(cvec, w_mod, b_mod)


def _ffn_kernel(x_ref, mod_ref, wg_ref, wu_ref, wo_ref, g_ref, b_ref, o_ref, u_ref, *, mod_base):
    f = pl.program_id(1)

    @pl.when(f == 0)
    def _():
        shift = mod_ref[mod_base:mod_base + 1, :]
        scale = mod_ref[mod_base + 1:mod_base + 2, :]
        u_ref[...] = (x_ref[...] * (1.0 + scale) + shift).astype(BF16)
        o_ref[...] = jnp.zeros_like(o_ref)

    u = u_ref[...]
    gate = _dot(u, wg_ref[...])
    up = _dot(u, wu_ref[...])
    act = (_silu(gate) * up).astype(BF16)
    o_ref[...] += _dot(act, wo_ref[...])

    @pl.when(f == pl.num_programs(1) - 1)
    def _():
        g3 = mod_ref[mod_base + 2:mod_base + 3, :]
        y = DEEPNORM_ALPHA * x_ref[...] + MACARON_WEIGHT * (g3 * o_ref[...])
        o_ref[...] = _layer_norm(y, g_ref[...], b_ref[...])


def _ffn(x, mod, w_in, w_out, ln_g, ln_b, mod_base):
    M, D = x.shape
    rows_per_mod = M // mod.shape[0]
    nf = D_FF // FF_TILE
    return pl.pallas_call(
        functools.partial(_ffn_kernel, mod_base=mod_base),
        grid=(M // ROW_TILE, nf),
        in_specs=[
            pl.BlockSpec((ROW_TILE, D), lambda m, f: (m, 0)),
            pl.BlockSpec((None, N_MOD, D), lambda m, f: ((m * ROW_TILE) // rows_per_mod, 0, 0)),
            pl.BlockSpec((D, FF_TILE), lambda m, f: (0, f)),
            pl.BlockSpec((D, FF_TILE), lambda m, f: (0, f + nf)),
            pl.BlockSpec((FF_TILE, D), lambda m, f: (f, 0)),
            pl.BlockSpec((1, D), lambda m, f: (0, 0)),
            pl.BlockSpec((1, D), lambda m, f: (0, 0)),
        ],
        out_specs=pl.BlockSpec((ROW_TILE, D), lambda m, f: (m, 0)),
        out_shape=jax.ShapeDtypeStruct((M, D), F32),
        scratch_shapes=[pltpu.VMEM((ROW_TILE, D), BF16)],
        compiler_params=_params("parallel", "arbitrary"),
        name="ffn",
    )(x, mod, w_in, w_in, w_out, ln_g, ln_b)


def _inproj_kernel(x_ref, mod_ref, w_ref, o_ref, u_ref):
    @pl.when(pl.program_id(1) == 0)
    def _():
        shift = mod_ref[3:4, :]
        scale = mod_ref[4:5, :]
        u_ref[...] = (x_ref[...] * (1.0 + scale) + shift).astype(BF16)

    o_ref[...] = _dot(u_ref[...], w_ref[...])


def _inproj(x, mod, w):
    M, D = x.shape
    rows_per_mod = M // mod.shape[0]
    N = w.shape[1]
    return pl.pallas_call(
        _inproj_kernel,
        grid=(M // ROW_TILE, N // PROJ_TN),
        in_specs=[
            pl.BlockSpec((ROW_TILE, D), lambda m, n: (m, 0)),
            pl.BlockSpec((None, N_MOD, D), lambda m, n: ((m * ROW_TILE) // rows_per_mod, 0, 0)),
            pl.BlockSpec((D, PROJ_TN), lambda m, n: (0, n)),
        ],
        out_specs=pl.BlockSpec((ROW_TILE, PROJ_TN), lambda m, n: (m, n)),
        out_shape=jax.ShapeDtypeStruct((M, N), F32),
        scratch_shapes=[pltpu.VMEM((ROW_TILE, D), BF16)],
        compiler_params=_params("parallel", "arbitrary"),
        name="inproj",
    )(x, mod, w)


def _attn_kernel(*refs, rope, cache_len, emit_kv, tq):
    it = iter(refs)
    q_ref, k_ref, v_ref, qn_ref, kn_ref = (next(it) for _ in range(5))
    if rope:
        cosq_ref, sinq_ref, cosk_ref, sink_ref = (next(it) for _ in range(4))
    if cache_len:
        ck_ref, cv_ref = next(it), next(it)
    o_ref = next(it)
    if emit_kv:
        newk_ref, newv_ref = next(it), next(it)
    kb_ref, vb_ref = next(it), next(it)

    @pl.when(pl.program_id(2) == 0)
    def _():
        k = _rms_norm(k_ref[...], kn_ref[...])
        v = v_ref[...]
        if emit_kv:
            newk_ref[...] = k
            newv_ref[...] = v
        if rope:
            k = _rope(k, cosk_ref[...], sink_ref[...])
        if cache_len:
            kb_ref[0:cache_len, :] = ck_ref[...].astype(BF16)
            vb_ref[0:cache_len, :] = cv_ref[...].astype(BF16)
        kb_ref[cache_len:, :] = k.astype(BF16)
        vb_ref[cache_len:, :] = v.astype(BF16)

    q = q_ref[...]
    heads = []
    for g in range(ATT_GROUP):
        qg = _rms_norm(q[:, g * LANES:(g + 1) * LANES], qn_ref[...])
        if rope:
            qg = _rope(qg, cosq_ref[...], sinq_ref[...])
        heads.append(qg.astype(BF16))
    q4 = jnp.concatenate(heads, axis=0)
    s = _dot_nt(q4, kb_ref[...]) * (ATT_HEAD_DIM ** -0.5)
    p = jnp.exp(s - jnp.max(s, axis=-1, keepdims=True))
    denom = jnp.sum(p, axis=-1, keepdims=True)
    o = _dot(p.astype(BF16), vb_ref[...]) / denom
    for g in range(ATT_GROUP):
        o_ref[:, g * LANES:(g + 1) * LANES] = o[g * tq:(g + 1) * tq, :].astype(o_ref.dtype)


def _attention(P, B, S, q_norm, k_norm, rope_tabs, cache, emit_kv, tq):
    nq = S // tq
    cache_len = cache[0].shape[2] if cache is not None else 0
    T = cache_len + S
    gw = ATT_GROUP * ATT_HEAD_DIM
    in_specs = [
        pl.BlockSpec((tq, gw), lambda b, j, i: (b * nq + i, j)),
        pl.BlockSpec((S, LANES), lambda b, j, i: (b, COL_AK + j)),
        pl.BlockSpec((S, LANES), lambda b, j, i: (b, COL_AV + j)),
        pl.BlockSpec((1, LANES), lambda b, j, i: (0, 0)),
        pl.BlockSpec((1, LANES), lambda b, j, i: (0, 0)),
    ]
    args = [P, P, P, q_norm, k_norm]
    if rope_tabs is not None:
        cos, sin = rope_tabs
        in_specs += [
            pl.BlockSpec((tq, LANES), lambda b, j, i: (i, 0)),
            pl.BlockSpec((tq, LANES), lambda b, j, i: (i, 0)),
            pl.BlockSpec((S, LANES), lambda b, j, i: (0, 0)),
            pl.BlockSpec((S, LANES), lambda b, j, i: (0, 0)),
        ]
        args += [cos, sin, cos, sin]
    if cache is not None:
        spec = pl.BlockSpec((None, None, cache_len, LANES), lambda b, j, i: (b, j, 0, 0))
        in_specs += [spec, spec]
        args += list(cache)
    out_specs = [pl.BlockSpec((tq, gw), lambda b, j, i: (b * nq + i, j))]
    out_shape = [jax.ShapeDtypeStruct((B * S, ATT_WIDTH), BF16)]
    if emit_kv:
        kv_spec = pl.BlockSpec((S, LANES), lambda b, j, i: (b, j))
        out_specs += [kv_spec, kv_spec]
        out_shape += [jax.ShapeDtypeStruct((B * S, ATT_KV_HEADS * ATT_HEAD_DIM), F32)] * 2
    return pl.pallas_call(
        functools.partial(_attn_kernel, rope=rope_tabs is not None, cache_len=cache_len,
                          emit_kv=emit_kv, tq=tq),
        grid=(B, ATT_KV_HEADS, nq),
        in_specs=in_specs,
        out_specs=out_specs,
        out_shape=out_shape,
        scratch_shapes=[pltpu.VMEM((T, LANES), BF16), pltpu.VMEM((T, LANES), BF16)],
        compiler_params=_params("parallel", "parallel", "arbitrary"),
        name="attention",
    )(*args)


def _ret_kernel(*refs, rope, has_s0, emit_state, S):
    it = iter(refs)
    lg_ref, q_ref, k_ref, v_ref, g_ref = (next(it) for _ in range(5))
    if rope:
        cos_ref, sin_ref = next(it), next(it)
    if has_s0:
        s0_ref = next(it)
    o_ref = next(it)
    if emit_state:
        st_ref = next(it)
    qs_ref, ks_ref, acc_ref, state_ref = (next(it) for _ in range(4))

    h = pl.program_id(1)
    C = RET_CHUNK
    n = S // C
    q = q_ref[...]
    k = k_ref[...] * (RET_DK ** -0.5)
    if rope:
        q = _rope(q, cos_ref[...], sin_ref[...])
        k = _rope(k, cos_ref[...], sin_ref[...])
    qs_ref[...] = q
    ks_ref[...] = k

    ii = lax.broadcasted_iota(jnp.int32, (C, C), 0).astype(F32)
    jj = lax.broadcasted_iota(jnp.int32, (C, C), 1).astype(F32)
    t = lax.broadcasted_iota(jnp.int32, (C, 1), 0).astype(F32)

    for d in range(2):
        lg = lg_ref[d, h]
        diff = ii - jj if d == 0 else jj - ii
        dmat = jnp.where(diff >= 0, jnp.exp(jnp.maximum(diff, 0.0) * lg), 0.0)
        q_dec = jnp.exp(((t + 1.0) if d == 0 else (C - t)) * lg)
        k_dec = jnp.exp(((C - 1.0 - t) if d == 0 else t) * lg)
        chunk_decay = jnp.exp(jnp.full((RET_DK, RET_DV), C * lg, F32))
        if has_s0:
            state_ref[...] = s0_ref[d]
        else:
            state_ref[...] = jnp.zeros_like(state_ref)

        def chunk(c, carry, d=d, dmat=dmat, q_dec=q_dec, k_dec=k_dec, chunk_decay=chunk_decay):
            cc = c if d == 0 else n - 1 - c
            sl = pl.ds(pl.multiple_of(cc * C, C), C)
            qc, kc, vc = qs_ref[sl, :], ks_ref[sl, :], v_ref[sl, :].astype(BF16)
            scores = _dot_nt(qc.astype(BF16), kc.astype(BF16)) * dmat
            state = state_ref[...]
            out = _dot(scores.astype(BF16), vc) + _dot((qc * q_dec).astype(BF16), state.astype(BF16))
            state_ref[...] = chunk_decay * state + _dot_tn((kc * k_dec).astype(BF16), vc)
            if d == 0:
                acc_ref[sl, :] = out
            else:
                y = acc_ref[sl, :] + out
                mu = jnp.mean(y, axis=-1, keepdims=True)
                yc = y - mu
                var = jnp.mean(yc * yc, axis=-1, keepdims=True)
                o_ref[sl, :] = (yc * lax.rsqrt(var + LN_EPS) * _silu(g_ref[sl, :])).astype(o_ref.dtype)
            return carry

        lax.fori_loop(0, n, chunk, 0)
        if emit_state:
            st_ref[d] = state_ref[...]


def _retention(P, B, S, log_decay, rope_tabs, s0, emit_state):
    def col(c0):
        return pl.BlockSpec((S, LANES), lambda b, h: (b, c0 + h))

    in_specs = [pl.BlockSpec(memory_space=pltpu.SMEM), col(COL_RQ), col(COL_RK), col(COL_RV), col(COL_RG)]
    args = [log_decay, P, P, P, P]
    if rope_tabs is not None:
        tab = pl.BlockSpec((S, LANES), lambda b, h: (0, 0))
        in_specs += [tab, tab]
        args += list(rope_tabs)
    state_spec = pl.BlockSpec((None, 2, None, RET_DK, RET_DV), lambda b, h: (b, 0, h, 0, 0))
    if s0 is not None:
        in_specs.append(state_spec)
        args.append(s0)
    out_specs = [pl.BlockSpec((S, LANES), lambda b, h: (b, h))]
    out_shape = [jax.ShapeDtypeStruct((B * S, RET_WIDTH), BF16)]
    if emit_state:
        out_specs.append(state_spec)
        out_shape.append(jax.ShapeDtypeStruct((B, 2, RET_HEADS, RET_DK, RET_DV), F32))
    return pl.pallas_call(
        functools.partial(_ret_kernel, rope=rope_tabs is not None, has_s0=s0 is not None,
                          emit_state=emit_state, S=S),
        grid=(B, RET_HEADS),
        in_specs=in_specs,
        out_specs=out_specs,
        out_shape=out_shape,
        scratch_shapes=[pltpu.VMEM((S, RET_DK), F32), pltpu.VMEM((S, RET_DK), F32),
                        pltpu.VMEM((S, RET_DV), F32), pltpu.VMEM((RET_DK, RET_DV), F32)],
        compiler_params=_params("parallel", "parallel"),
        name="retention",
    )(*args)


def _gla_kernel(*refs, has_s0, emit_state, S):
    it = iter(refs)
    q_ref, k_ref, v_ref, gr_ref, ga_ref, wa_ref, ba_ref, gn_ref = (next(it) for _ in range(8))
    if has_s0:
        s0_ref = next(it)
    o_ref = next(it)
    if emit_state:
        st_ref = next(it)
    la_ref, acc_ref, state_ref = (next(it) for _ in range(3))

    C = GLA_CHUNK
    n = S // C
    ga = ga_ref[...].astype(BF16)
    for d in range(2):
        pre = _dot(ga, wa_ref[d]) + ba_ref[d]
        la_ref[d] = (jnp.minimum(pre, 0.0) - jnp.log1p(jnp.exp(-jnp.abs(pre)))) * (1.0 / GLA_TAU)

    ii = lax.broadcasted_iota(jnp.int32, (C, C), 0)
    jj = lax.broadcasted_iota(jnp.int32, (C, C), 1)
    lane = lax.broadcasted_iota(jnp.int32, (1, LANES), 1)
    head_lanes = [lane < GLA_DK, lane >= GLA_DK]

    for d in range(2):
        tri = (jj <= ii) if d == 0 else (jj >= ii)
        tri_b = jnp.where(tri, 1.0, 0.0).astype(BF16)
        tri_f = jnp.where(tri, 1.0, 0.0)
        if has_s0:
            state_ref[...] = s0_ref[d].T
        else:
            state_ref[...] = jnp.zeros_like(state_ref)

        def chunk(c, carry, d=d, tri_b=tri_b, tri_f=tri_f):
            cc = c if d == 0 else n - 1 - c
            sl = pl.ds(pl.multiple_of(cc * C, C), C)
            la = la_ref[d, sl, :]
            hi = la.astype(BF16)
            r1 = la - hi.astype(F32)
            mid = r1.astype(BF16)
            lo = (r1 - mid.astype(F32)).astype(BF16)
            b = _dot(tri_b, hi) + _dot(tri_b, mid) + _dot(tri_b, lo)
            b_end = b[C - 1:C, :] if d == 0 else b[0:1, :]
            qc = q_ref[sl, :] * (GLA_DK ** -0.5)
            kc = k_ref[sl, :]
            vc = v_ref[sl, :]
            q_t = qc * jnp.exp(b)
            k_t = (kc * jnp.exp(-b)).astype(BF16)
            k_s = kc * jnp.exp(b_end - b)
            state_t = state_ref[...]
            state_b = state_t.astype(BF16)
            new_state = state_t * jnp.exp(b_end)
            for i in range(2):
                q_i = jnp.where(head_lanes[i], q_t, 0.0).astype(BF16)
                v_i = vc[:, i * GLA_DV:(i + 1) * GLA_DV].astype(BF16)
                att = _dot_nt(q_i, k_t) * tri_f
                out = _dot(att.astype(BF16), v_i) + _dot_nt(q_i, state_b)
                new_state = new_state + _dot_tn(v_i, jnp.where(head_lanes[i], k_s, 0.0).astype(BF16))
                if d == 0:
                    acc_ref[i, sl, :] = out
                else:
                    y = _rms_norm(acc_ref[i, sl, :] + out, gn_ref[...])
                    gate = _silu(gr_ref[sl, i * GLA_DV:(i + 1) * GLA_DV])
                    o_ref[sl, i * GLA_DV:(i + 1) * GLA_DV] = (y * gate).astype(o_ref.dtype)
            state_ref[...] = new_state
            return carry

        lax.fori_loop(0, n, chunk, 0)
        if emit_state:
            st_ref[d] = state_ref[...].T


def _gla(P, B, S, wa, ba, gla_norm, s0, emit_state):
    pw = 2 * GLA_DV
    in_specs = [
        pl.BlockSpec((S, LANES), lambda b, p: (b, COL_GQ + p)),
        pl.BlockSpec((S, LANES), lambda b, p: (b, COL_GK + p)),
        pl.BlockSpec((S, pw), lambda b, p: (b, COL_GV // 2 + p)),
        pl.BlockSpec((S, pw), lambda b, p: (b, COL_GR // 2 + p)),
        pl.BlockSpec((S, LANES), lambda b, p: (b, COL_GA)),
        pl.BlockSpec((2, LANES, LANES), lambda b, p: (0, 0, p)),
        pl.BlockSpec((2, 1, LANES), lambda b, p: (0, 0, p)),
        pl.BlockSpec((1, GLA_DV), lambda b, p: (0, 0)),
    ]
    args = [P, P, P, P, P, wa, ba, gla_norm]
    state_spec = pl.BlockSpec((None, 2, None, LANES, GLA_DV), lambda b, p: (b, 0, p, 0, 0))
    if s0 is not None:
        in_specs.append(state_spec)
        args.append(s0)
    out_specs = [pl.BlockSpec((S, pw), lambda b, p: (b, p))]
    out_shape = [jax.ShapeDtypeStruct((B * S, GLA_WIDTH), BF16)]
    if emit_state:
        out_specs.append(state_spec)
        out_shape.append(jax.ShapeDtypeStruct((B, 2, GLA_PAIRS, LANES, GLA_DV), F32))
    return pl.pallas_call(
        functools.partial(_gla_kernel, has_s0=s0 is not None, emit_state=emit_state, S=S),
        grid=(B, GLA_PAIRS),
        in_specs=in_specs,
        out_specs=out_specs,
        out_shape=out_shape,
        scratch_shapes=[pltpu.VMEM((2, S, LANES), F32), pltpu.VMEM((2, S, GLA_DV), F32),
                        pltpu.VMEM((GLA_DV, LANES), F32)],
        compiler_params=_params("parallel", "parallel"),
        name="gla",
    )(*args)


def _outproj_kernel(x_ref, mod_ref, a_ref, r_ref, gl_ref, wa_ref, wr_ref, wg_ref, g_ref, b_ref, o_ref):
    y = _dot(a_ref[...], wa_ref[...]) + _dot(r_ref[...], wr_ref[...]) + _dot(gl_ref[...], wg_ref[...])
    y = DEEPNORM_ALPHA * x_ref[...] + mod_ref[5:6, :] * y
    o_ref[...] = _layer_norm(y, g_ref[...], b_ref[...])


def _outproj(x, mod, att, ret, gla, w, ln_g, ln_b):
    M, D = x.shape
    rows_per_mod = M // mod.shape[0]
    return pl.pallas_call(
        _outproj_kernel,
        grid=(M // ROW_TILE,),
        in_specs=[
            pl.BlockSpec((ROW_TILE, D), lambda m: (m, 0)),
            pl.BlockSpec((None, N_MOD, D), lambda m: ((m * ROW_TILE) // rows_per_mod, 0, 0)),
            pl.BlockSpec((ROW_TILE, ATT_WIDTH), lambda m: (m, 0)),
            pl.BlockSpec((ROW_TILE, RET_WIDTH), lambda m: (m, 0)),
            pl.BlockSpec((ROW_TILE, GLA_WIDTH), lambda m: (m, 0)),
            pl.BlockSpec((ATT_WIDTH, D), lambda m: (0, 0)),
            pl.BlockSpec((RET_WIDTH, D), lambda m: (ATT_WIDTH // RET_WIDTH, 0)),
            pl.BlockSpec((GLA_WIDTH, D), lambda m: ((ATT_WIDTH + RET_WIDTH) // GLA_WIDTH, 0)),
            pl.BlockSpec((1, D), lambda m: (0, 0)),
            pl.BlockSpec((1, D), lambda m: (0, 0)),
        ],
        out_specs=pl.BlockSpec((ROW_TILE, D), lambda m: (m, 0)),
        out_shape=jax.ShapeDtypeStruct((M, D), F32),
        compiler_params=_params("parallel"),
        name="outproj",
    )(x, mod, att, ret, gla, w, w, w, ln_g, ln_b)


def _trunk_layer(x, mod, B, S, lw, rope_tabs, ctx, attn_tq):
    x = _ffn(x, mod, lw["ffn_w_in"][0], lw["ffn_w_out"][0], lw["ln_g"][0], lw["ln_b"][0], 0)
    P = _inproj(x, mod, lw["mix_w_in"])
    is_context = ctx is None
    cache = None if is_context else (ctx[0], ctx[1])
    s_ret0 = None if is_context else ctx[2]
    s_gla0 = None if is_context else ctx[3]
    att_out = _attention(P, B, S, lw["q_norm"], lw["k_norm"], rope_tabs, cache, is_context, attn_tq)
    ret_out = _retention(P, B, S, lw["log_decay"], rope_tabs, s_ret0, is_context)
    gla_out = _gla(P, B, S, lw["gla_wa"], lw["gla_ba"], lw["gla_norm"], s_gla0, is_context)
    x = _outproj(x, mod, att_out[0], ret_out[0], gla_out[0], lw["mix_w_out"], lw["ln_g"][1], lw["ln_b"][1])
    x = _ffn(x, mod, lw["ffn_w_in"][1], lw["ffn_w_out"][1], lw["ln_g"][2], lw["ln_b"][2], 6)
    new_ctx = (att_out[1], att_out[2], ret_out[1], gla_out[1]) if is_context else None
    return x, new_ctx


def _rope_tables(rows):
    row = jnp.repeat(jnp.arange(rows, dtype=F32), GRID_W)
    col = jnp.tile(jnp.arange(GRID_W, dtype=F32), rows)
    n_freq = ATT_HEAD_DIM // 4
    inv = ROPE_THETA ** (-jnp.arange(n_freq, dtype=F32) / n_freq)
    ang = jnp.concatenate([row[:, None] * inv, col[:, None] * inv], axis=-1)
    cos, sin = jnp.cos(ang), jnp.sin(ang)
    cos_full = jnp.repeat(cos, 2, axis=-1)
    sin_signed = jnp.stack([-sin, sin], axis=-1).reshape(ang.shape[0], ATT_HEAD_DIM)
    return cos_full, sin_signed


def kernel(x_prompt, x_sample, c, cache_attn_k, cache_attn_v, state_ret, state_gla, c_ctx,
           w_mod, b_mod, ln_g, ln_b, ffn_w_in, ffn_w_out, mix_w_in, mix_w_out,
           att_q_norm, att_k_norm, ret_log_decay, gla_w_a2, gla_b_a, gla_norm):
    B_ctx, S_ctx, D = x_prompt.shape
    B_lat, S_lat, _ = x_sample.shape

    ffn_w_in_b = ffn_w_in.astype(BF16)
    ffn_w_out_b = ffn_w_out.astype(BF16)
    mix_w_in_b = jnp.pad(mix_w_in.astype(BF16), ((0, 0), (0, 0), (0, PROJ_COLS_PADDED - PROJ_COLS)))
    mix_w_out_b = mix_w_out.astype(BF16)
    gla_wa = jnp.zeros((DEPTH, 2, LANES, GLA_HEADS * GLA_DK), BF16)
    for d in range(2):
        gla_wa = gla_wa.at[:, d, d * GLA_GATE_RANK:(d + 1) * GLA_GATE_RANK, :].set(gla_w_a2[:, d].astype(BF16))

    cvec = jnp.concatenate([c_ctx[None, :], c, jnp.zeros((8 - 1 - B_lat, D), F32)], axis=0)
    mod = _modulation(cvec, w_mod, b_mod[:, None, :]).reshape(DEPTH, 8, N_MOD, D)

    def layer_weights(l):
        return dict(
            ffn_w_in=ffn_w_in_b[l], ffn_w_out=ffn_w_out_b[l], mix_w_in=mix_w_in_b[l], mix_w_out=mix_w_out_b[l],
            ln_g=ln_g[l][:, None, :], ln_b=ln_b[l][:, None, :],
            q_norm=att_q_norm[l][None, :], k_norm=att_k_norm[l][None, :],
            log_decay=ret_log_decay[l], gla_wa=gla_wa[l], gla_ba=gla_b_a[l][:, None, :],
            gla_norm=gla_norm[l][None, :])

    h = x_prompt.reshape(B_ctx * S_ctx, D)
    ks_l, vs_l, sr_l, sg_l = [], [], [], []
    for l in range(DEPTH):
        h, (k_l, v_l, s_r, s_g) = _trunk_layer(h, mod[l, 0:1], B_ctx, S_ctx, layer_weights(l), None, None, S_ctx)
        ks_l.append(k_l.reshape(B_ctx, S_ctx, ATT_KV_HEADS, ATT_HEAD_DIM))
        vs_l.append(v_l.reshape(B_ctx, S_ctx, ATT_KV_HEADS, ATT_HEAD_DIM))
        sr_l.append(s_r)
        sg_l.append(s_g.reshape(B_ctx, 2, GLA_HEADS, GLA_DK, GLA_DV))
    y_prompt = h.reshape(B_ctx, S_ctx, D)

    rope_tabs = _rope_tables(S_lat // GRID_W)
    g = x_sample.reshape(B_lat * S_lat, D)
    for l in range(DEPTH):
        ctx = (cache_attn_k[:, l].transpose(0, 2, 1, 3), cache_attn_v[:, l].transpose(0, 2, 1, 3),
               state_ret[:, l], state_gla[:, l].reshape(B_lat, 2, GLA_PAIRS, LANES, GLA_DV))
        g, _ = _trunk_layer(g, mod[l, 1:1 + B_lat], B_lat, S_lat, layer_weights(l), rope_tabs, ctx, 128)
    y_sample = g.reshape(B_lat, S_lat, D)

    return (y_prompt, y_sample, jnp.stack(ks_l, axis=1), jnp.stack(vs_l, axis=1),
            jnp.stack(sr_l, axis=1), jnp.stack(sg_l, axis=1))
```

```python
import functools
import math

import jax
import jax.numpy as jnp
from jax import lax
from jax.experimental import pallas as pl
from jax.experimental.pallas import tpu as pltpu

F32 = jnp.float32
BF16 = jnp.bfloat16

D_MODEL = 2048
DEPTH = 2
GRID_W = 64
ATT_HEAD_DIM = 128
ATT_HEADS = 8
ATT_KV_HEADS = 2
ATT_GROUP = ATT_HEADS // ATT_KV_HEADS
ATT_WIDTH = ATT_HEADS * ATT_HEAD_DIM
ATT_KV_CHUNK = 256
ROPE_THETA = 10000.0
RET_DK = 128
RET_DV = 128
RET_HEADS = 4
RET_WIDTH = RET_HEADS * RET_DV
RET_CHUNK = 128
GLA_DK = 64
GLA_DV = 128
GLA_HEADS = 4
GLA_PAIRS = GLA_HEADS // 2
GLA_WIDTH = GLA_HEADS * GLA_DV
GLA_GATE_RANK = 16
GLA_TAU = 16.0
GLA_CHUNK = 64
D_FF = 5632
N_MOD = 9
MACARON_WEIGHT = 0.5
DEEPNORM_ALPHA = (2 * DEPTH) ** 0.25
LN_EPS = 1e-5
RMS_EPS = 1e-6

LANES = 128
PROJ_COLS = 5152
PROJ_COLS_PADDED = 5376
PROJ_TN = 768
COL_AQ, COL_AK, COL_AV = 0, 8, 10
COL_RQ, COL_RK, COL_RV, COL_RG = 12, 16, 20, 24
COL_GQ, COL_GK, COL_GV, COL_GR, COL_GA = 28, 30, 32, 36, 40

VMEM_LIMIT_BYTES = 56 * 1024 * 1024
ROW_TILE = 512
FF_TILE = 512
MOD_TN = 1024
MOD_ROWS = 8


def _params(*sem):
    return pltpu.CompilerParams(dimension_semantics=sem, vmem_limit_bytes=VMEM_LIMIT_BYTES)


def _dot(a, b):
    return jnp.dot(a, b, preferred_element_type=F32)


def _dot_nt(a, b):
    return lax.dot_general(a, b, (((1,), (1,)), ((), ())), preferred_element_type=F32)


def _dot_tn(a, b):
    return lax.dot_general(a, b, (((0,), (0,)), ((), ())), preferred_element_type=F32)


def _silu(x):
    return x * jax.nn.sigmoid(x)


def _layer_norm(y, g, b):
    mu = jnp.mean(y, axis=-1, keepdims=True)
    d = y - mu
    var = jnp.mean(d * d, axis=-1, keepdims=True)
    return d * lax.rsqrt(var + LN_EPS) * g + b


def _rms_norm(x, g):
    return x * lax.rsqrt(jnp.mean(x * x, axis=-1, keepdims=True) + RMS_EPS) * g


def _rope(x, cos, sin_signed):
    lane = lax.broadcasted_iota(jnp.int32, x.shape, 1)
    partner = jnp.where((lane & 1) == 0, pltpu.roll(x, LANES - 1, 1), pltpu.roll(x, 1, 1))
    return x * cos + partner * sin_signed


def _mod_spec(l, who0, rows_per_mod, ndim_grid):
    if ndim_grid == 1:
        return pl.BlockSpec((None, None, N_MOD, D_MODEL),
                            lambda m: (l, who0 + (m * ROW_TILE) // rows_per_mod, 0, 0))
    return pl.BlockSpec((None, None, N_MOD, D_MODEL),
                        lambda m, n: (l, who0 + (m * ROW_TILE) // rows_per_mod, 0, 0))


def _mod_kernel(c_ref, w_ref, b_ref, o_ref):
    a = _silu(c_ref[...]).astype(BF16)
    o_ref[...] = _dot(a, w_ref[...].astype(BF16)) + b_ref[...]


def _modulation(cvec, w_mod, b_mod):
    L, D, N = w_mod.shape
    return pl.pallas_call(
        _mod_kernel,
        grid=(L, N // MOD_TN),
        in_specs=[
            pl.BlockSpec((MOD_ROWS, D), lambda l, n: (0, 0)),
            pl.BlockSpec((None, D, MOD_TN), lambda l, n: (l, 0, n)),
            pl.BlockSpec((None, 1, MOD_TN), lambda l, n: (l, 0, n)),
        ],
        out_specs=pl.BlockSpec((None, MOD_ROWS, MOD_TN), lambda l, n: (l, 0, n)),
        out_shape=jax.ShapeDtypeStruct((L, MOD_ROWS, N), F32),
        compiler_params=_params("parallel", "parallel"),
        name="modulation",
    )(cvec, w_mod, b_mod)


def _ffn_kernel(x_ref, mod_ref, wg_ref, wu_ref, wo_ref, g_ref, b_ref, o_ref, u_ref, *, mod_base):
    f = pl.program_id(1)

    @pl.when(f == 0)
    def _():
        shift = mod_ref[mod_base:mod_base + 1, :]
        scale = mod_ref[mod_base + 1:mod_base + 2, :]
        u_ref[...] = (x_ref[...] * (1.0 + scale) + shift).astype(BF16)
        o_ref[...] = jnp.zeros_like(o_ref)

    u = u_ref[...]
    gate = _dot(u, wg_ref[...])
    up = _dot(u, wu_ref[...])
    act = (_silu(gate) * up).astype(BF16)
    o_ref[...] += _dot(act, wo_ref[...])

    @pl.when(f == pl.num_programs(1) - 1)
    def _():
        g3 = mod_ref[mod_base + 2:mod_base + 3, :]
        y = DEEPNORM_ALPHA * x_ref[...] + MACARON_WEIGHT * (g3 * o_ref[...])
        o_ref[...] = _layer_norm(y, g_ref[...], b_ref[...])


def _ffn(x, mod, who0, rows_per_mod, w_in, w_out, ln_g, ln_b, l, half):
    M, D = x.shape
    nf = D_FF // FF_TILE
    sub = 2 * half
    return pl.pallas_call(
        functools.partial(_ffn_kernel, mod_base=3 * sub),
        grid=(M // ROW_TILE, nf),
        in_specs=[
            pl.BlockSpec((ROW_TILE, D), lambda m, f: (m, 0)),
            _mod_spec(l, who0, rows_per_mod, 2),
            pl.BlockSpec((None, None, D, FF_TILE), lambda m, f: (l, half, 0, f)),
            pl.BlockSpec((None, None, D, FF_TILE), lambda m, f: (l, half, 0, f + nf)),
            pl.BlockSpec((None, None, FF_TILE, D), lambda m, f: (l, half, f, 0)),
            pl.BlockSpec((None, None, 1, D), lambda m, f: (l, sub, 0, 0)),
            pl.BlockSpec((None, None, 1, D), lambda m, f: (l, sub, 0, 0)),
        ],
        out_specs=pl.BlockSpec((ROW_TILE, D), lambda m, f: (m, 0)),
        out_shape=jax.ShapeDtypeStruct((M, D), F32),
        scratch_shapes=[pltpu.VMEM((ROW_TILE, D), BF16)],
        compiler_params=_params("parallel", "arbitrary"),
        name="ffn",
    )(x, mod, w_in, w_in, w_out, ln_g, ln_b)


def _inproj_kernel(x_ref, mod_ref, w_ref, o_ref, u_ref):
    @pl.when(pl.program_id(1) == 0)
    def _():
        shift = mod_ref[3:4, :]
        scale = mod_ref[4:5, :]
        u_ref[...] = (x_ref[...] * (1.0 + scale) + shift).astype(BF16)

    o_ref[...] = _dot(u_ref[...], w_ref[...])


def _inproj(x, mod, who0, rows_per_mod, w, l):
    M, D = x.shape
    N = w.shape[2]
    return pl.pallas_call(
        _inproj_kernel,
        grid=(M // ROW_TILE, N // PROJ_TN),
        in_specs=[
            pl.BlockSpec((ROW_TILE, D), lambda m, n: (m, 0)),
            _mod_spec(l, who0, rows_per_mod, 2),
            pl.BlockSpec((None, D, PROJ_TN), lambda m, n: (l, 0, n)),
        ],
        out_specs=pl.BlockSpec((ROW_TILE, PROJ_TN), lambda m, n: (m, n)),
        out_shape=jax.ShapeDtypeStruct((M, N), F32),
        scratch_shapes=[pltpu.VMEM((ROW_TILE, D), BF16)],
        compiler_params=_params("parallel", "arbitrary"),
        name="inproj",
    )(x, mod, w)


def _attn_kernel(*refs, rope, cache_len, emit_kv, tq, n_chunks):
    it = iter(refs)
    q_ref, k_ref, v_ref, qn_ref, kn_ref = (next(it) for _ in range(5))
    if rope:
        cosq_ref, sinq_ref, cosk_ref, sink_ref = (next(it) for _ in range(4))
    if cache_len:
        ck_ref, cv_ref = next(it), next(it)
    o_ref = next(it)
    if emit_kv:
        newk_ref, newv_ref = next(it), next(it)
    kb_ref, vt_ref = next(it), next(it)
    CK = ATT_KV_CHUNK

    @pl.when(pl.program_id(2) == 0)
    def _():
        k = _rms_norm(k_ref[...], kn_ref[...])
        v = v_ref[...]
        if emit_kv:
            newk_ref[...] = k
            newv_ref[...] = v
        if rope:
            k = _rope(k, cosk_ref[...], sink_ref[...])
        if cache_len:
            k = jnp.concatenate([ck_ref[...], k], axis=0)
            v = jnp.concatenate([cv_ref[...], v], axis=0)
        for c in range(n_chunks):
            kb_ref[c] = k[c * CK:(c + 1) * CK, :].astype(BF16)
            vt_ref[c] = v[c * CK:(c + 1) * CK, :].T.astype(BF16)

    q = q_ref[...]
    heads = []
    for g in range(ATT_GROUP):
        qg = _rms_norm(q[:, g * LANES:(g + 1) * LANES], qn_ref[...])
        if rope:
            qg = _rope(qg, cosq_ref[...], sinq_ref[...])
        heads.append(qg.T)
    q_t = jnp.concatenate(heads, axis=1).astype(BF16)

    c2 = (ATT_HEAD_DIM ** -0.5) * math.log2(math.e)
    m = denom = acc = None
    for c in range(n_chunks):
        s = _dot(kb_ref[c], q_t)
        m_c = jnp.max(s, axis=0, keepdims=True)
        if c == 0:
            m = m_c
            p = jnp.exp2((s - m) * c2)
            denom = jnp.sum(p, axis=0, keepdims=True)
            acc = _dot(vt_ref[c], p.astype(BF16))
        else:
            m_new = jnp.maximum(m, m_c)
            alpha = jnp.exp2((m - m_new) * c2)
            p = jnp.exp2((s - m_new) * c2)
            denom = denom * alpha + jnp.sum(p, axis=0, keepdims=True)
            acc = acc * alpha + _dot(vt_ref[c], p.astype(BF16))
            m = m_new
    o_t = acc * (1.0 / denom)
    for g in range(ATT_GROUP):
        o_ref[:, g * LANES:(g + 1) * LANES] = o_t[:, g * tq:(g + 1) * tq].T.astype(o_ref.dtype)


def _attention(P, B, S, q_norm, k_norm, l, rope_tabs, cache, emit_kv, tq):
    nq = S // tq
    cache_len = cache[0].shape[3] if cache is not None else 0
    T = cache_len + S
    n_chunks = T // ATT_KV_CHUNK
    gw = ATT_GROUP * ATT_HEAD_DIM
    norm_spec = pl.BlockSpec((None, 1, LANES), lambda b, j, i: (l, 0, 0))
    in_specs = [
        pl.BlockSpec((tq, gw), lambda b, j, i: (b * nq + i, j)),
        pl.BlockSpec((S, LANES), lambda b, j, i: (b, COL_AK + j)),
        pl.BlockSpec((S, LANES), lambda b, j, i: (b, COL_AV + j)),
        norm_spec, norm_spec,
    ]
    args = [P, P, P, q_norm, k_norm]
    if rope_tabs is not None:
        cos, sin = rope_tabs
        in_specs += [
            pl.BlockSpec((tq, LANES), lambda b, j, i: (i, 0)),
            pl.BlockSpec((tq, LANES), lambda b, j, i: (i, 0)),
            pl.BlockSpec((S, LANES), lambda b, j, i: (0, 0)),
            pl.BlockSpec((S, LANES), lambda b, j, i: (0, 0)),
        ]
        args += [cos, sin, cos, sin]
    if cache is not None:
        spec = pl.BlockSpec((None, None, None, cache_len, LANES), lambda b, j, i: (b, l, j, 0, 0))
        in_specs += [spec, spec]
        args += list(cache)
    out_specs = [pl.BlockSpec((tq, gw), lambda b, j, i: (b * nq + i, j))]
    out_shape = [jax.ShapeDtypeStruct((B * S, ATT_WIDTH), BF16)]
    if emit_kv:
        kv_spec = pl.BlockSpec((S, LANES), lambda b, j, i: (b, j))
        out_specs += [kv_spec, kv_spec]
        out_shape += [jax.ShapeDtypeStruct((B * S, ATT_KV_HEADS * ATT_HEAD_DIM), F32)] * 2
    return pl.pallas_call(
        functools.partial(_attn_kernel, rope=rope_tabs is not None, cache_len=cache_len,
                          emit_kv=emit_kv, tq=tq, n_chunks=n_chunks),
        grid=(B, ATT_KV_HEADS, nq),
        in_specs=in_specs,
        out_specs=out_specs,
        out_shape=out_shape,
        scratch_shapes=[pltpu.VMEM((n_chunks, ATT_KV_CHUNK, LANES), BF16),
                        pltpu.VMEM((n_chunks, LANES, ATT_KV_CHUNK), BF16)],
        compiler_params=_params("parallel", "parallel", "arbitrary"),
        name="attention",
    )(*args)


def _ret_kernel(*refs, l, rope, has_s0, emit_state, S):
    it = iter(refs)
    lg_ref, q_ref, k_ref, v_ref, g_ref = (next(it) for _ in range(5))
    if rope:
        cos_ref, sin_ref = next(it), next(it)
    if has_s0:
        s0_ref = next(it)
    o_ref = next(it)
    if emit_state:
        st_ref = next(it)
    qs_ref, ks_ref, acc_ref, state_ref = (next(it) for _ in range(4))

    h = pl.program_id(1)
    C = RET_CHUNK
    n = S // C
    q = q_ref[...]
    k = k_ref[...] * (RET_DK ** -0.5)
    if rope:
        q = _rope(q, cos_ref[...], sin_ref[...])
        k = _rope(k, cos_ref[...], sin_ref[...])
    qs_ref[...] = q
    ks_ref[...] = k

    ii = lax.broadcasted_iota(jnp.int32, (C, C), 0).astype(F32)
    jj = lax.broadcasted_iota(jnp.int32, (C, C), 1).astype(F32)
    t = lax.broadcasted_iota(jnp.int32, (C, 1), 0).astype(F32)

    for d in range(2):
        lg = lg_ref[l, d, h]
        diff = ii - jj if d == 0 else jj - ii
        dmat = jnp.where(diff >= 0, jnp.exp(jnp.maximum(diff, 0.0) * lg), 0.0)
        q_dec = jnp.exp(((t + 1.0) if d == 0 else (C - t)) * lg)
        k_dec = jnp.exp(((C - 1.0 - t) if d == 0 else t) * lg)
        chunk_decay = jnp.exp(jnp.full((RET_DK, RET_DV), C * lg, F32))
        if has_s0:
            state_ref[...] = s0_ref[d]
        else:
            state_ref[...] = jnp.zeros_like(state_ref)

        def chunk(c, carry, d=d, dmat=dmat, q_dec=q_dec, k_dec=k_dec, chunk_decay=chunk_decay):
            cc = c if d == 0 else n - 1 - c
            sl = pl.ds(pl.multiple_of(cc * C, C), C)
            qc, kc, vc = qs_ref[sl, :], ks_ref[sl, :], v_ref[sl, :].astype(BF16)
            scores = _dot_nt(qc.astype(BF16), kc.astype(BF16)) * dmat
            state = state_ref[...]
            out = _dot(scores.astype(BF16), vc) + _dot((qc * q_dec).astype(BF16), state.astype(BF16))
            state_ref[...] = chunk_decay * state + _dot_tn((kc * k_dec).astype(BF16), vc)
            if d == 0:
                acc_ref[sl, :] = out
            else:
                y = acc_ref[sl, :] + out
                mu = jnp.mean(y, axis=-1, keepdims=True)
                yc = y - mu
                var = jnp.mean(yc * yc, axis=-1, keepdims=True)
                o_ref[sl, :] = (yc * lax.rsqrt(var + LN_EPS) * _silu(g_ref[sl, :])).astype(o_ref.dtype)
            return carry

        lax.fori_loop(0, n, chunk, 0)
        if emit_state:
            st_ref[d] = state_ref[...]


def _retention(P, B, S, log_decay, l, rope_tabs, s0, emit_state):
    def col(c0):
        return pl.BlockSpec((S, LANES), lambda b, h: (b, c0 + h))

    in_specs = [pl.BlockSpec(memory_space=pltpu.SMEM), col(COL_RQ), col(COL_RK), col(COL_RV), col(COL_RG)]
    args = [log_decay, P, P, P, P]
    if rope_tabs is not None:
        tab = pl.BlockSpec((S, LANES), lambda b, h: (0, 0))
        in_specs += [tab, tab]
        args += list(rope_tabs)
    if s0 is not None:
        in_specs.append(pl.BlockSpec((None, None, 2, None, RET_DK, RET_DV), lambda b, h: (b, l, 0, h, 0, 0)))
        args.append(s0)
    out_specs = [pl.BlockSpec((S, LANES), lambda b, h: (b, h))]
    out_shape = [jax.ShapeDtypeStruct((B * S, RET_WIDTH), BF16)]
    if emit_state:
        out_specs.append(pl.BlockSpec((None, 2, None, RET_DK, RET_DV), lambda b, h: (b, 0, h, 0, 0)))
        out_shape.append(jax.ShapeDtypeStruct((B, 2, RET_HEADS, RET_DK, RET_DV), F32))
    return pl.pallas_call(
        functools.partial(_ret_kernel, l=l, rope=rope_tabs is not None, has_s0=s0 is not None,
                          emit_state=emit_state, S=S),
        grid=(B, RET_HEADS),
        in_specs=in_specs,
        out_specs=out_specs,
        out_shape=out_shape,
        scratch_shapes=[pltpu.VMEM((S, RET_DK), F32), pltpu.VMEM((S, RET_DK), F32),
                        pltpu.VMEM((S, RET_DV), F32), pltpu.VMEM((RET_DK, RET_DV), F32)],
        compiler_params=_params("parallel", "parallel"),
        name="retention",
    )(*args)


def _gla_kernel(*refs, has_s0, emit_state, S):
    it = iter(refs)
    q_ref, k_ref, v_ref, gr_ref, ga_ref, wa_ref, ba_ref, gn_ref = (next(it) for _ in range(8))
    if has_s0:
        s0_ref = next(it)
    o_ref = next(it)
    if emit_state:
        st_ref = next(it)
    la_ref, acc_ref, state_ref = (next(it) for _ in range(3))

    C = GLA_CHUNK
    n = S // C
    ga = ga_ref[...].astype(BF16)
    for d in range(2):
        pre = _dot(ga, wa_ref[d]) + ba_ref[d]
        la_ref[d] = (jnp.minimum(pre, 0.0) - jnp.log1p(jnp.exp(-jnp.abs(pre)))) * (1.0 / GLA_TAU)

    ii = lax.broadcasted_iota(jnp.int32, (C, C), 0)
    jj = lax.broadcasted_iota(jnp.int32, (C, C), 1)
    lane = lax.broadcasted_iota(jnp.int32, (1, LANES), 1)
    head_lanes = [lane < GLA_DK, lane >= GLA_DK]

    for d in range(2):
        tri = (jj <= ii) if d == 0 else (jj >= ii)
        tri_b = jnp.where(tri, 1.0, 0.0).astype(BF16)
        tri_f = jnp.where(tri, 1.0, 0.0)
        if has_s0:
            state_ref[...] = s0_ref[d].T
        else:
            state_ref[...] = jnp.zeros_like(state_ref)

        def chunk(c, carry, d=d, tri_b=tri_b, tri_f=tri_f):
            cc = c if d == 0 else n - 1 - c
            sl = pl.ds(pl.multiple_of(cc * C, C), C)
            la = la_ref[d, sl, :]
            hi = la.astype(BF16)
            r1 = la - hi.astype(F32)
            mid = r1.astype(BF16)
            lo = (r1 - mid.astype(F32)).astype(BF16)
            b = _dot(tri_b, hi) + _dot(tri_b, mid) + _dot(tri_b, lo)
            b_end = b[C - 1:C, :] if d == 0 else b[0:1, :]
            qc = q_ref[sl, :] * (GLA_DK ** -0.5)
            kc = k_ref[sl, :]
            vc = v_ref[sl, :]
            q_t = qc * jnp.exp(b)
            k_t = (kc * jnp.exp(-b)).astype(BF16)
            k_s = kc * jnp.exp(b_end - b)
            state_t = state_ref[...]
            state_b = state_t.astype(BF16)
            new_state = state_t * jnp.exp(b_end)
            for i in range(2):
                q_i = jnp.where(head_lanes[i], q_t, 0.0).astype(BF16)
                v_i = vc[:, i * GLA_DV:(i + 1) * GLA_DV].astype(BF16)
                att = _dot_nt(q_i, k_t) * tri_f
                out = _dot(att.astype(BF16), v_i) + _dot_nt(q_i, state_b)
                new_state = new_state + _dot_tn(v_i, jnp.where(head_lanes[i], k_s, 0.0).astype(BF16))
                if d == 0:
                    acc_ref[i, sl, :] = out
                else:
                    y = _rms_norm(acc_ref[i, sl, :] + out, gn_ref[...])
                    gate = _silu(gr_ref[sl, i * GLA_DV:(i + 1) * GLA_DV])
                    o_ref[sl, i * GLA_DV:(i + 1) * GLA_DV] = (y * gate).astype(o_ref.dtype)
            state_ref[...] = new_state
            return carry

        lax.fori_loop(0, n, chunk, 0)
        if emit_state:
            st_ref[d] = state_ref[...].T


def _gla(P, B, S, wa, ba, gla_norm, l, s0, emit_state):
    pw = 2 * GLA_DV
    in_specs = [
        pl.BlockSpec((S, LANES), lambda b, p: (b, COL_GQ + p)),
        pl.BlockSpec((S, LANES), lambda b, p: (b, COL_GK + p)),
        pl.BlockSpec((S, pw), lambda b, p: (b, COL_GV // 2 + p)),
        pl.BlockSpec((S, pw), lambda b, p: (b, COL_GR // 2 + p)),
        pl.BlockSpec((S, LANES), lambda b, p: (b, COL_GA)),
        pl.BlockSpec((None, 2, LANES, LANES), lambda b, p: (l, 0, 0, p)),
        pl.BlockSpec((None, 2, 1, LANES), lambda b, p: (l, 0, 0, p)),
        pl.BlockSpec((None, 1, GLA_DV), lambda b, p: (l, 0, 0)),
    ]
    args = [P, P, P, P, P, wa, ba, gla_norm]
    if s0 is not None:
        in_specs.append(pl.BlockSpec((None, None, 2, None, LANES, GLA_DV), lambda b, p: (b, l, 0, p, 0, 0)))
        args.append(s0)
    out_specs = [pl.BlockSpec((S, pw), lambda b, p: (b, p))]
    out_shape = [jax.ShapeDtypeStruct((B * S, GLA_WIDTH), BF16)]
    if emit_state:
        out_specs.append(pl.BlockSpec((None, 2, None, LANES, GLA_DV), lambda b, p: (b, 0, p, 0, 0)))
        out_shape.append(jax.ShapeDtypeStruct((B, 2, GLA_PAIRS, LANES, GLA_DV), F32))
    return pl.pallas_call(
        functools.partial(_gla_kernel, has_s0=s0 is not None, emit_state=emit_state, S=S),
        grid=(B, GLA_PAIRS),
        in_specs=in_specs,
        out_specs=out_specs,
        out_shape=out_shape,
        scratch_shapes=[pltpu.VMEM((2, S, LANES), F32), pltpu.VMEM((2, S, GLA_DV), F32),
                        pltpu.VMEM((GLA_DV, LANES), F32)],
        compiler_params=_params("parallel", "parallel"),
        name="gla",
    )(*args)


def _outproj_kernel(x_ref, mod_ref, a_ref, r_ref, gl_ref, wa_ref, wr_ref, wg_ref, g_ref, b_ref, o_ref):
    y = _dot(a_ref[...], wa_ref[...]) + _dot(r_ref[...], wr_ref[...]) + _dot(gl_ref[...], wg_ref[...])
    y = DEEPNORM_ALPHA * x_ref[...] + mod_ref[5:6, :] * y
    o_ref[...] = _layer_norm(y, g_ref[...], b_ref[...])


def _outproj(x, mod, who0, rows_per_mod, att, ret, gla, w, ln_g, ln_b, l):
    M, D = x.shape
    return pl.pallas_call(
        _outproj_kernel,
        grid=(M // ROW_TILE,),
        in_specs=[
            pl.BlockSpec((ROW_TILE, D), lambda m: (m, 0)),
            _mod_spec(l, who0, rows_per_mod, 1),
            pl.BlockSpec((ROW_TILE, ATT_WIDTH), lambda m: (m, 0)),
            pl.BlockSpec((ROW_TILE, RET_WIDTH), lambda m: (m, 0)),
            pl.BlockSpec((ROW_TILE, GLA_WIDTH), lambda m: (m, 0)),
            pl.BlockSpec((None, ATT_WIDTH, D), lambda m: (l, 0, 0)),
            pl.BlockSpec((None, RET_WIDTH, D), lambda m: (l, ATT_WIDTH // RET_WIDTH, 0)),
            pl.BlockSpec((None, GLA_WIDTH, D), lambda m: (l, (ATT_WIDTH + RET_WIDTH) // GLA_WIDTH, 0)),
            pl.BlockSpec((None, None, 1, D), lambda m: (l, 1, 0, 0)),
            pl.BlockSpec((None, None, 1, D), lambda m: (l, 1, 0, 0)),
        ],
        out_specs=pl.BlockSpec((ROW_TILE, D), lambda m: (m, 0)),
        out_shape=jax.ShapeDtypeStruct((M, D), F32),
        compiler_params=_params("parallel"),
        name="outproj",
    )(x, mod, att, ret, gla, w, w, w, ln_g, ln_b)


def _trunk_layer(x, B, S, l, who0, w, rope_tabs, ctx, attn_tq):
    is_context = ctx is None
    rows_per_mod = x.shape[0] if is_context else S
    mod = w["mod"]
    x = _ffn(x, mod, who0, rows_per_mod, w["ffn_w_in"], w["ffn_w_out"], w["ln_g"], w["ln_b"], l, 0)
    P = _inproj(x, mod, who0, rows_per_mod, w["mix_w_in"], l)
    cache = None if is_context else (ctx[0], ctx[1])
    s_ret0 = None if is_context else ctx[2]
    s_gla0 = None if is_context else ctx[3]
    att_out = _attention(P, B, S, w["q_norm"], w["k_norm"], l, rope_tabs, cache, is_context, attn_tq)
    ret_out = _retention(P, B, S, w["log_decay"], l, rope_tabs, s_ret0, is_context)
    gla_out = _gla(P, B, S, w["gla_wa"], w["gla_ba"], w["gla_norm"], l, s_gla0, is_context)
    x = _outproj(x, mod, who0, rows_per_mod, att_out[0], ret_out[0], gla_out[0],
                 w["mix_w_out"], w["ln_g"], w["ln_b"], l)
    x = _ffn(x, mod, who0, rows_per_mod, w["ffn_w_in"], w["ffn_w_out"], w["ln_g"], w["ln_b"], l, 1)
    new_ctx = (att_out[1], att_out[2], ret_out[1], gla_out[1]) if is_context else None
    return x, new_ctx


def _rope_tables(rows):
    row = jnp.repeat(jnp.arange(rows, dtype=F32), GRID_W)
    col = jnp.tile(jnp.arange(GRID_W, dtype=F32), rows)
    n_freq = ATT_HEAD_DIM // 4
    inv = ROPE_THETA ** (-jnp.arange(n_freq, dtype=F32) / n_freq)
    ang = jnp.concatenate([row[:, None] * inv, col[:, None] * inv], axis=-1)
    cos, sin = jnp.cos(ang), jnp.sin(ang)
    cos_full = jnp.repeat(cos, 2, axis=-1)
    sin_signed = jnp.stack([-sin, sin], axis=-1).reshape(ang.shape[0], ATT_HEAD_DIM)
    return cos_full, sin_signed


def kernel(x_prompt, x_sample, c, cache_attn_k, cache_attn_v, state_ret, state_gla, c_ctx,
           w_mod, b_mod, ln_g, ln_b, ffn_w_in, ffn_w_out, mix_w_in, mix_w_out,
           att_q_norm, att_k_norm, ret_log_decay, gla_w_a2, gla_b_a, gla_norm):
    B_ctx, S_ctx, D = x_prompt.shape
    B_lat, S_lat, _ = x_sample.shape

    gla_wa = jnp.zeros((DEPTH, 2, LANES, GLA_HEADS * GLA_DK), BF16)
    for d in range(2):
        gla_wa = gla_wa.at[:, d, d * GLA_GATE_RANK:(d + 1) * GLA_GATE_RANK, :].set(gla_w_a2[:, d].astype(BF16))

    cvec = jnp.concatenate([c_ctx[None, :], c, jnp.zeros((MOD_ROWS - 1 - B_lat, D), F32)], axis=0)
    mod = _modulation(cvec, w_mod, b_mod[:, None, :]).reshape(DEPTH, MOD_ROWS, N_MOD, D)

    w = dict(
        mod=mod,
        ffn_w_in=ffn_w_in.astype(BF16),
        ffn_w_out=ffn_w_out.astype(BF16),
        mix_w_in=jnp.pad(mix_w_in.astype(BF16), ((0, 0), (0, 0), (0, PROJ_COLS_PADDED - PROJ_COLS))),
        mix_w_out=mix_w_out.astype(BF16),
        ln_g=ln_g[:, :, None, :], ln_b=ln_b[:, :, None, :],
        q_norm=att_q_norm[:, None, :], k_norm=att_k_norm[:, None, :],
        log_decay=ret_log_decay, gla_wa=gla_wa, gla_ba=gla_b_a[:, :, None, :],
        gla_norm=gla_norm[:, None, :])

    h = x_prompt.reshape(B_ctx * S_ctx, D)
    ks_l, vs_l, sr_l, sg_l = [], [], [], []
    for l in range(DEPTH):
        h, (k_l, v_l, s_r, s_g) = _trunk_layer(h, B_ctx, S_ctx, l, 0, w, None, None, S_ctx)
        ks_l.append(k_l.reshape(B_ctx, S_ctx, ATT_KV_HEADS, ATT_HEAD_DIM))
        vs_l.append(v_l.reshape(B_ctx, S_ctx, ATT_KV_HEADS, ATT_HEAD_DIM))
        sr_l.append(s_r)
        sg_l.append(s_g.reshape(B_ctx, 2, GLA_HEADS, GLA_DK, GLA_DV))
    y_prompt = h.reshape(B_ctx, S_ctx, D)

    rope_tabs = _rope_tables(S_lat // GRID_W)
    ctx = (cache_attn_k.transpose(0, 1, 3, 2, 4), cache_attn_v.transpose(0, 1, 3, 2, 4), state_ret,
           state_gla.reshape(B_lat, DEPTH, 2, GLA_PAIRS, LANES, GLA_DV))
    g = x_sample.reshape(B_lat * S_lat, D)
    for l in range(DEPTH):
        g, _ = _trunk_layer(g, B_lat, S_lat, l, 1, w, rope_tabs, ctx, 256)
    y_sample = g.reshape(B_lat, S_lat, D)

    return (y_prompt, y_sample, jnp.stack(ks_l, axis=1), jnp.stack(vs_l, axis=1),
            jnp.stack(sr_l, axis=1), jnp.stack(sg_l, axis=1))
```

```python
import functools
import math

import jax
import jax.numpy as jnp
from jax import lax
from jax.experimental import pallas as pl
from jax.experimental.pallas import tpu as pltpu

F32 = jnp.float32
BF16 = jnp.bfloat16

D_MODEL = 2048
DEPTH = 2
GRID_W = 64
ATT_HEAD_DIM = 128
ATT_HEADS = 8
ATT_KV_HEADS = 2
ATT_GROUP = ATT_HEADS // ATT_KV_HEADS
ATT_WIDTH = ATT_HEADS * ATT_HEAD_DIM
ATT_KV_CHUNK = 256
ROPE_THETA = 10000.0
RET_DK = 128
RET_DV = 128
RET_HEADS = 4
RET_WIDTH = RET_HEADS * RET_DV
RET_CHUNK = 128
GLA_DK = 64
GLA_DV = 128
GLA_HEADS = 4
GLA_PAIRS = GLA_HEADS // 2
GLA_WIDTH = GLA_HEADS * GLA_DV
GLA_GATE_RANK = 16
GLA_TAU = 16.0
GLA_CHUNK = 64
D_FF = 5632
N_MOD = 9
MACARON_WEIGHT = 0.5
DEEPNORM_ALPHA = (2 * DEPTH) ** 0.25
LN_EPS = 1e-5
RMS_EPS = 1e-6

LANES = 128
PROJ_COLS = 5152
PROJ_COLS_PADDED = 5376
PROJ_TN = 768
COL_AQ, COL_AK, COL_AV = 0, 8, 10
COL_RQ, COL_RK, COL_RV, COL_RG = 12, 16, 20, 24
COL_GQ, COL_GK, COL_GV, COL_GR, COL_GA = 28, 30, 32, 36, 40

VMEM_LIMIT_BYTES = 56 * 1024 * 1024
ROW_TILE = 512
PROJ_ROW_TILE = 1024
FF_TILE = 512
MOD_TN = 1024
MOD_ROWS = 8


def _params(*sem):
    return pltpu.CompilerParams(dimension_semantics=sem, vmem_limit_bytes=VMEM_LIMIT_BYTES)


def _dot(a, b):
    return jnp.dot(a, b, preferred_element_type=F32)


def _dot_nt(a, b):
    return lax.dot_general(a, b, (((1,), (1,)), ((), ())), preferred_element_type=F32)


def _dot_tn(a, b):
    return lax.dot_general(a, b, (((0,), (0,)), ((), ())), preferred_element_type=F32)


def _silu(x):
    return x * jax.nn.sigmoid(x)


def _layer_norm(y, g, b):
    mu = jnp.mean(y, axis=-1, keepdims=True)
    d = y - mu
    var = jnp.mean(d * d, axis=-1, keepdims=True)
    return d * lax.rsqrt(var + LN_EPS) * g + b


def _rms_norm(x, g):
    return x * lax.rsqrt(jnp.mean(x * x, axis=-1, keepdims=True) + RMS_EPS) * g


def _rope(x, cos, sin_signed):
    lane = lax.broadcasted_iota(jnp.int32, x.shape, 1)
    partner = jnp.where((lane & 1) == 0, pltpu.roll(x, LANES - 1, 1), pltpu.roll(x, 1, 1))
    return x * cos + partner * sin_signed


def _mod_spec(l, who0, rows_per_mod, ndim_grid, row_tile=ROW_TILE):
    if ndim_grid == 1:
        return pl.BlockSpec((None, None, N_MOD, D_MODEL),
                            lambda m: (l, who0 + (m * row_tile) // rows_per_mod, 0, 0))
    return pl.BlockSpec((None, None, N_MOD, D_MODEL),
                        lambda m, n: (l, who0 + (m * row_tile) // rows_per_mod, 0, 0))


def _mod_kernel(c_ref, w_ref, b_ref, o_ref):
    a = _silu(c_ref[...]).astype(BF16)
    o_ref[...] = _dot(a, w_ref[...].astype(BF16)) + b_ref[...]


def _modulation(cvec, w_mod, b_mod):
    L, D, N = w_mod.shape
    return pl.pallas_call(
        _mod_kernel,
        grid=(L, N // MOD_TN),
        in_specs=[
            pl.BlockSpec((MOD_ROWS, D), lambda l, n: (0, 0)),
            pl.BlockSpec((None, D, MOD_TN), lambda l, n: (l, 0, n)),
            pl.BlockSpec((None, 1, MOD_TN), lambda l, n: (l, 0, n)),
        ],
        out_specs=pl.BlockSpec((None, MOD_ROWS, MOD_TN), lambda l, n: (l, 0, n)),
        out_shape=jax.ShapeDtypeStruct((L, MOD_ROWS, N), F32),
        compiler_params=_params("parallel", "parallel"),
        name="modulation",
    )(cvec, w_mod, b_mod)


def _ffn_kernel(x_ref, mod_ref, wg_ref, wu_ref, wo_ref, g_ref, b_ref, o_ref, u_ref, *, mod_base):
    f = pl.program_id(1)

    @pl.when(f == 0)
    def _():
        shift = mod_ref[mod_base:mod_base + 1, :]
        scale = mod_ref[mod_base + 1:mod_base + 2, :]
        u_ref[...] = (x_ref[...] * (1.0 + scale) + shift).astype(BF16)
        o_ref[...] = jnp.zeros_like(o_ref)

    u = u_ref[...]
    gate = _dot(u, wg_ref[...])
    up = _dot(u, wu_ref[...])
    act = (_silu(gate) * up).astype(BF16)
    o_ref[...] += _dot(act, wo_ref[...])

    @pl.when(f == pl.num_programs(1) - 1)
    def _():
        g3 = mod_ref[mod_base + 2:mod_base + 3, :]
        y = DEEPNORM_ALPHA * x_ref[...] + MACARON_WEIGHT * (g3 * o_ref[...])
        o_ref[...] = _layer_norm(y, g_ref[...], b_ref[...])


def _ffn(x, mod, who0, rows_per_mod, w_in, w_out, ln_g, ln_b, l, half):
    M, D = x.shape
    nf = D_FF // FF_TILE
    sub = 2 * half
    return pl.pallas_call(
        functools.partial(_ffn_kernel, mod_base=3 * sub),
        grid=(M // ROW_TILE, nf),
        in_specs=[
            pl.BlockSpec((ROW_TILE, D), lambda m, f: (m, 0)),
            _mod_spec(l, who0, rows_per_mod, 2),
            pl.BlockSpec((None, None, D, FF_TILE), lambda m, f: (l, half, 0, f)),
            pl.BlockSpec((None, None, D, FF_TILE), lambda m, f: (l, half, 0, f + nf)),
            pl.BlockSpec((None, None, FF_TILE, D), lambda m, f: (l, half, f, 0)),
            pl.BlockSpec((None, None, 1, D), lambda m, f: (l, sub, 0, 0)),
            pl.BlockSpec((None, None, 1, D), lambda m, f: (l, sub, 0, 0)),
        ],
        out_specs=pl.BlockSpec((ROW_TILE, D), lambda m, f: (m, 0)),
        out_shape=jax.ShapeDtypeStruct((M, D), F32),
        scratch_shapes=[pltpu.VMEM((ROW_TILE, D), BF16)],
        compiler_params=_params("parallel", "arbitrary"),
        name="ffn",
    )(x, mod, w_in, w_in, w_out, ln_g, ln_b)


def _inproj_kernel(x_ref, mod_ref, w_ref, o_ref, u_ref):
    @pl.when(pl.program_id(1) == 0)
    def _():
        shift = mod_ref[3:4, :]
        scale = mod_ref[4:5, :]
        u_ref[...] = (x_ref[...] * (1.0 + scale) + shift).astype(BF16)

    o_ref[...] = _dot(u_ref[...], w_ref[...]).astype(o_ref.dtype)


def _inproj(x, mod, who0, rows_per_mod, w, l):
    M, D = x.shape
    N = w.shape[2]
    return pl.pallas_call(
        _inproj_kernel,
        grid=(M // PROJ_ROW_TILE, N // PROJ_TN),
        in_specs=[
            pl.BlockSpec((PROJ_ROW_TILE, D), lambda m, n: (m, 0)),
            _mod_spec(l, who0, rows_per_mod, 2, PROJ_ROW_TILE),
            pl.BlockSpec((None, D, PROJ_TN), lambda m, n: (l, 0, n)),
        ],
        out_specs=pl.BlockSpec((PROJ_ROW_TILE, PROJ_TN), lambda m, n: (m, n)),
        out_shape=jax.ShapeDtypeStruct((M, N), BF16),
        scratch_shapes=[pltpu.VMEM((PROJ_ROW_TILE, D), BF16)],
        compiler_params=_params("parallel", "arbitrary"),
        name="inproj",
    )(x, mod, w)


def _attn_kernel(*refs, rope, cache_len, emit_kv, tq, n_chunks):
    it = iter(refs)
    q_ref, k_ref, v_ref, qn_ref, kn_ref = (next(it) for _ in range(5))
    if rope:
        cosq_ref, sinq_ref, cosk_ref, sink_ref = (next(it) for _ in range(4))
    if cache_len:
        ck_ref, cv_ref = next(it), next(it)
    o_ref = next(it)
    if emit_kv:
        newk_ref, newv_ref = next(it), next(it)
    kb_ref, vt_ref = next(it), next(it)
    CK = ATT_KV_CHUNK

    @pl.when(pl.program_id(2) == 0)
    def _():
        k = _rms_norm(k_ref[...].astype(F32), kn_ref[...])
        v = v_ref[...].astype(F32)
        if emit_kv:
            newk_ref[...] = k
            newv_ref[...] = v
        if rope:
            k = _rope(k, cosk_ref[...], sink_ref[...])
        if cache_len:
            k = jnp.concatenate([ck_ref[...], k], axis=0)
            v = jnp.concatenate([cv_ref[...], v], axis=0)
        for c in range(n_chunks):
            kb_ref[c] = k[c * CK:(c + 1) * CK, :].astype(BF16)
            vt_ref[c] = v[c * CK:(c + 1) * CK, :].T.astype(BF16)

    q = q_ref[...].astype(F32)
    heads = []
    for g in range(ATT_GROUP):
        qg = _rms_norm(q[:, g * LANES:(g + 1) * LANES], qn_ref[...])
        if rope:
            qg = _rope(qg, cosq_ref[...], sinq_ref[...])
        heads.append(qg.T)
    q_t = jnp.concatenate(heads, axis=1).astype(BF16)

    c2 = (ATT_HEAD_DIM ** -0.5) * math.log2(math.e)
    m = denom = acc = None
    for c in range(n_chunks):
        s = _dot(kb_ref[c], q_t)
        m_c = jnp.max(s, axis=0, keepdims=True)
        if c == 0:
            m = m_c
            p = jnp.exp2((s - m) * c2)
            denom = jnp.sum(p, axis=0, keepdims=True)
            acc = _dot(vt_ref[c], p.astype(BF16))
        else:
            m_new = jnp.maximum(m, m_c)
            alpha = jnp.exp2((m - m_new) * c2)
            p = jnp.exp2((s - m_new) * c2)
            denom = denom * alpha + jnp.sum(p, axis=0, keepdims=True)
            acc = acc * alpha + _dot(vt_ref[c], p.astype(BF16))
            m = m_new
    o_t = acc * (1.0 / denom)
    for g in range(ATT_GROUP):
        o_ref[:, g * LANES:(g + 1) * LANES] = o_t[:, g * tq:(g + 1) * tq].T.astype(o_ref.dtype)


def _attention(P, B, S, q_norm, k_norm, l, rope_tabs, cache, emit_kv, tq):
    nq = S // tq
    cache_len = cache[0].shape[3] if cache is not None else 0
    T = cache_len + S
    n_chunks = T // ATT_KV_CHUNK
    gw = ATT_GROUP * ATT_HEAD_DIM
    norm_spec = pl.BlockSpec((None, 1, LANES), lambda b, j, i: (l, 0, 0))
    in_specs = [
        pl.BlockSpec((tq, gw), lambda b, j, i: (b * nq + i, j)),
        pl.BlockSpec((S, LANES), lambda b, j, i: (b, COL_AK + j)),
        pl.BlockSpec((S, LANES), lambda b, j, i: (b, COL_AV + j)),
        norm_spec, norm_spec,
    ]
    args = [P, P, P, q_norm, k_norm]
    if rope_tabs is not None:
        cos, sin = rope_tabs
        in_specs += [
            pl.BlockSpec((tq, LANES), lambda b, j, i: (i, 0)),
            pl.BlockSpec((tq, LANES), lambda b, j, i: (i, 0)),
            pl.BlockSpec((S, LANES), lambda b, j, i: (0, 0)),
            pl.BlockSpec((S, LANES), lambda b, j, i: (0, 0)),
        ]
        args += [cos, sin, cos, sin]
    if cache is not None:
        spec = pl.BlockSpec((None, None, None, cache_len, LANES), lambda b, j, i: (b, l, j, 0, 0))
        in_specs += [spec, spec]
        args += list(cache)
    out_specs = [pl.BlockSpec((tq, gw), lambda b, j, i: (b * nq + i, j))]
    out_shape = [jax.ShapeDtypeStruct((B * S, ATT_WIDTH), BF16)]
    if emit_kv:
        kv_spec = pl.BlockSpec((S, LANES), lambda b, j, i: (b, j))
        out_specs += [kv_spec, kv_spec]
        out_shape += [jax.ShapeDtypeStruct((B * S, ATT_KV_HEADS * ATT_HEAD_DIM), F32)] * 2
    return pl.pallas_call(
        functools.partial(_attn_kernel, rope=rope_tabs is not None, cache_len=cache_len,
                          emit_kv=emit_kv, tq=tq, n_chunks=n_chunks),
        grid=(B, ATT_KV_HEADS, nq),
        in_specs=in_specs,
        out_specs=out_specs,
        out_shape=out_shape,
        scratch_shapes=[pltpu.VMEM((n_chunks, ATT_KV_CHUNK, LANES), BF16),
                        pltpu.VMEM((n_chunks, LANES, ATT_KV_CHUNK), BF16)],
        compiler_params=_params("parallel", "parallel", "arbitrary"),
        name="attention",
    )(*args)


def _ret_kernel(*refs, l, rope, has_s0, emit_state, S):
    it = iter(refs)
    lg_ref, q_ref, k_ref, v_ref, g_ref = (next(it) for _ in range(5))
    if rope:
        cos_ref, sin_ref = next(it), next(it)
    if has_s0:
        s0_ref = next(it)
    o_ref = next(it)
    if emit_state:
        st_ref = next(it)
    qb_ref, kb_ref, qdf_ref, qdb_ref, kdf_ref, kdb_ref, accf_ref, accb_ref = (next(it) for _ in range(8))

    h = pl.program_id(1)
    C = RET_CHUNK
    n = S // C
    lg_f = lg_ref[l, 0, h]
    lg_b = lg_ref[l, 1, h]
    q = q_ref[...].astype(F32)
    k = k_ref[...].astype(F32) * (RET_DK ** -0.5)
    if rope:
        q = _rope(q, cos_ref[...], sin_ref[...])
        k = _rope(k, cos_ref[...], sin_ref[...])

    t = lax.broadcasted_iota(jnp.int32, (C, LANES), 0).astype(F32)

    def scaled(x, expo):
        return (x.reshape(n, C, LANES) * jnp.exp(expo)[None]).reshape(S, LANES).astype(BF16)

    qb_ref[...] = q.astype(BF16)
    kb_ref[...] = k.astype(BF16)
    qdf_ref[...] = scaled(q, (t + 1.0) * lg_f)
    qdb_ref[...] = scaled(q, (C - t) * lg_b)
    kdf_ref[...] = scaled(k, (C - 1.0 - t) * lg_f)
    kdb_ref[...] = scaled(k, t * lg_b)

    ii = lax.broadcasted_iota(jnp.int32, (C, C), 0).astype(F32)
    jj = lax.broadcasted_iota(jnp.int32, (C, C), 1).astype(F32)
    dmat = (jnp.where(ii >= jj, jnp.exp(jnp.maximum(ii - jj, 0.0) * lg_f), 0.0)
            + jnp.where(jj >= ii, jnp.exp(jnp.maximum(jj - ii, 0.0) * lg_b), 0.0))
    decay_f = jnp.exp(jnp.full((RET_DK, RET_DV), C * lg_f, F32))
    decay_b = jnp.exp(jnp.full((RET_DK, RET_DV), C * lg_b, F32))
    if has_s0:
        state0 = (s0_ref[0], s0_ref[1])
    else:
        state0 = (jnp.zeros((RET_DK, RET_DV), F32), jnp.zeros((RET_DK, RET_DV), F32))

    def chunk(c, carry):
        state_f, state_b = carry
        sf = pl.ds(pl.multiple_of(c * C, C), C)
        sb = pl.ds(pl.multiple_of((n - 1 - c) * C, C), C)
        v_f = v_ref[sf, :]
        v_b = v_ref[sb, :]
        scores = _dot_nt(qb_ref[sf, :], kb_ref[sf, :]) * dmat
        accf_ref[sf, :] = _dot(scores.astype(BF16), v_f) + _dot(qdf_ref[sf, :], state_f.astype(BF16))
        accb_ref[sb, :] = _dot(qdb_ref[sb, :], state_b.astype(BF16))
        state_f = decay_f * state_f + _dot_tn(kdf_ref[sf, :], v_f)
        state_b = decay_b * state_b + _dot_tn(kdb_ref[sb, :], v_b)
        return state_f, state_b

    state_f, state_b = lax.fori_loop(0, n, chunk, state0, unroll=2)
    if emit_state:
        st_ref[0] = state_f
        st_ref[1] = state_b

    y = accf_ref[...] + accb_ref[...]
    mu = jnp.mean(y, axis=-1, keepdims=True)
    yc = y - mu
    var = jnp.mean(yc * yc, axis=-1, keepdims=True)
    gate = _silu(g_ref[...].astype(F32))
    o_ref[...] = (yc * lax.rsqrt(var + LN_EPS) * gate).astype(o_ref.dtype)


def _retention(P, B, S, log_decay, l, rope_tabs, s0, emit_state):
    def col(c0):
        return pl.BlockSpec((S, LANES), lambda b, h: (b, c0 + h))

    in_specs = [pl.BlockSpec(memory_space=pltpu.SMEM), col(COL_RQ), col(COL_RK), col(COL_RV), col(COL_RG)]
    args = [log_decay, P, P, P, P]
    if rope_tabs is not None:
        tab = pl.BlockSpec((S, LANES), lambda b, h: (0, 0))
        in_specs += [tab, tab]
        args += list(rope_tabs)
    if s0 is not None:
        in_specs.append(pl.BlockSpec((None, None, 2, None, RET_DK, RET_DV), lambda b, h: (b, l, 0, h, 0, 0)))
        args.append(s0)
    out_specs = [pl.BlockSpec((S, LANES), lambda b, h: (b, h))]
    out_shape = [jax.ShapeDtypeStruct((B * S, RET_WIDTH), BF16)]
    if emit_state:
        out_specs.append(pl.BlockSpec((None, 2, None, RET_DK, RET_DV), lambda b, h: (b, 0, h, 0, 0)))
        out_shape.append(jax.ShapeDtypeStruct((B, 2, RET_HEADS, RET_DK, RET_DV), F32))
    return pl.pallas_call(
        functools.partial(_ret_kernel, l=l, rope=rope_tabs is not None, has_s0=s0 is not None,
                          emit_state=emit_state, S=S),
        grid=(B, RET_HEADS),
        in_specs=in_specs,
        out_specs=out_specs,
        out_shape=out_shape,
        scratch_shapes=[pltpu.VMEM((S, RET_DK), BF16)] * 6 + [pltpu.VMEM((S, RET_DV), F32)] * 2,
        compiler_params=_params("parallel", "parallel"),
        name="retention",
    )(*args)


def _gla_kernel(*refs, has_s0, emit_state, S):
    it = iter(refs)
    q_ref, k_ref, v_ref, gr_ref, ga_ref, wa_ref, ba_ref, gn_ref = (next(it) for _ in range(8))
    if has_s0:
        s0_ref = next(it)
    o_ref = next(it)
    if emit_state:
        st_ref = next(it)
    la_ref, qt_ref, kt_ref, ks_ref, dec_ref, acc_ref = (next(it) for _ in range(6))

    C = GLA_CHUNK
    n = S // C
    ga = ga_ref[...]
    for d in range(2):
        pre = _dot(ga, wa_ref[d]) + ba_ref[d]
        la_ref[d] = (jnp.minimum(pre, 0.0) - jnp.log1p(jnp.exp(-jnp.abs(pre)))) * (1.0 / GLA_TAU)

    ii = lax.broadcasted_iota(jnp.int32, (C, C), 0)
    jj = lax.broadcasted_iota(jnp.int32, (C, C), 1)
    tri = [jj <= ii, jj >= ii]
    tri_b = [jnp.where(m, 1.0, 0.0).astype(BF16) for m in tri]
    tri_f = [jnp.where(m, 1.0, 0.0) for m in tri]
    lane = lax.broadcasted_iota(jnp.int32, (C, LANES), 1)
    head_lanes = [lane < GLA_DK, lane >= GLA_DK]

    def prepare(c, carry):
        sl = pl.ds(pl.multiple_of(c * C, C), C)
        qc = q_ref[sl, :].astype(F32) * (GLA_DK ** -0.5)
        kc = k_ref[sl, :].astype(F32)
        for d in range(2):
            la = la_ref[d, sl, :]
            hi = la.astype(BF16)
            r1 = la - hi.astype(F32)
            mid = r1.astype(BF16)
            lo = (r1 - mid.astype(F32)).astype(BF16)
            b = _dot(tri_b[d], hi) + _dot(tri_b[d], mid) + _dot(tri_b[d], lo)
            b_end = b[C - 1:C, :] if d == 0 else b[0:1, :]
            qt_ref[d, sl, :] = (qc * jnp.exp(b)).astype(BF16)
            kt_ref[d, sl, :] = (kc * jnp.exp(-b)).astype(BF16)
            ks_ref[d, sl, :] = (kc * jnp.exp(b_end - b)).astype(BF16)
            dec_ref[d, c] = jnp.exp(b_end)
        return carry

    lax.fori_loop(0, n, prepare, 0, unroll=2)

    if has_s0:
        state0 = (s0_ref[0].T, s0_ref[1].T)
    else:
        state0 = (jnp.zeros((GLA_DV, LANES), F32), jnp.zeros((GLA_DV, LANES), F32))

    def chunk(c, carry):
        new_states = []
        for d in range(2):
            cc = c if d == 0 else n - 1 - c
            sl = pl.ds(pl.multiple_of(cc * C, C), C)
            q_t, k_t, k_s = qt_ref[d, sl, :], kt_ref[d, sl, :], ks_ref[d, sl, :]
            state_b = carry[d].astype(BF16)
            new_state = carry[d] * dec_ref[d, cc]
            for i in range(2):
                q_i = jnp.where(head_lanes[i], q_t, jnp.zeros_like(q_t))
                v_i = v_ref[sl, i * GLA_DV:(i + 1) * GLA_DV]
                att = _dot_nt(q_i, k_t) * tri_f[d]
                acc_ref[d, i, sl, :] = _dot(att.astype(BF16), v_i) + _dot_nt(q_i, state_b)
                new_state = new_state + _dot_tn(v_i, jnp.where(head_lanes[i], k_s, jnp.zeros_like(k_s)))
            new_states.append(new_state)
        return tuple(new_states)

    states = lax.fori_loop(0, n, chunk, state0, unroll=2)
    if emit_state:
        st_ref[0] = states[0].T
        st_ref[1] = states[1].T

    for i in range(2):
        y = _rms_norm(acc_ref[0, i] + acc_ref[1, i], gn_ref[...])
        gate = _silu(gr_ref[:, i * GLA_DV:(i + 1) * GLA_DV].astype(F32))
        o_ref[:, i * GLA_DV:(i + 1) * GLA_DV] = (y * gate).astype(o_ref.dtype)


def _gla(P, B, S, wa, ba, gla_norm, l, s0, emit_state):
    pw = 2 * GLA_DV
    in_specs = [
        pl.BlockSpec((S, LANES), lambda b, p: (b, COL_GQ + p)),
        pl.BlockSpec((S, LANES), lambda b, p: (b, COL_GK + p)),
        pl.BlockSpec((S, pw), lambda b, p: (b, COL_GV // 2 + p)),
        pl.BlockSpec((S, pw), lambda b, p: (b, COL_GR // 2 + p)),
        pl.BlockSpec((S, LANES), lambda b, p: (b, COL_GA)),
        pl.BlockSpec((None, 2, LANES, LANES), lambda b, p: (l, 0, 0, p)),
        pl.BlockSpec((None, 2, 1, LANES), lambda b, p: (l, 0, 0, p)),
        pl.BlockSpec((None, 1, GLA_DV), lambda b, p: (l, 0, 0)),
    ]
    args = [P, P, P, P, P, wa, ba, gla_norm]
    if s0 is not None:
        in_specs.append(pl.BlockSpec((None, None, 2, None, LANES, GLA_DV), lambda b, p: (b, l, 0, p, 0, 0)))
        args.append(s0)
    out_specs = [pl.BlockSpec((S, pw), lambda b, p: (b, p))]
    out_shape = [jax.ShapeDtypeStruct((B * S, GLA_WIDTH), BF16)]
    if emit_state:
        out_specs.append(pl.BlockSpec((None, 2, None, LANES, GLA_DV), lambda b, p: (b, 0, p, 0, 0)))
        out_shape.append(jax.ShapeDtypeStruct((B, 2, GLA_PAIRS, LANES, GLA_DV), F32))
    return pl.pallas_call(
        functools.partial(_gla_kernel, has_s0=s0 is not None, emit_state=emit_state, S=S),
        grid=(B, GLA_PAIRS),
        in_specs=in_specs,
        out_specs=out_specs,
        out_shape=out_shape,
        scratch_shapes=[pltpu.VMEM((2, S, LANES), F32),
                        pltpu.VMEM((2, S, LANES), BF16), pltpu.VMEM((2, S, LANES), BF16),
                        pltpu.VMEM((2, S, LANES), BF16),
                        pltpu.VMEM((2, S // GLA_CHUNK, 1, LANES), F32),
                        pltpu.VMEM((2, 2, S, GLA_DV), F32)],
        compiler_params=_params("parallel", "parallel"),
        name="gla",
    )(*args)


def _outproj_kernel(x_ref, mod_ref, a_ref, r_ref, gl_ref, wa_ref, wr_ref, wg_ref, g_ref, b_ref, o_ref):
    y = _dot(a_ref[...], wa_ref[...]) + _dot(r_ref[...], wr_ref[...]) + _dot(gl_ref[...], wg_ref[...])
    y = DEEPNORM_ALPHA * x_ref[...] + mod_ref[5:6, :] * y
    o_ref[...] = _layer_norm(y, g_ref[...], b_ref[...])


def _outproj(x, mod, who0, rows_per_mod, att, ret, gla, w, ln_g, ln_b, l):
    M, D = x.shape
    return pl.pallas_call(
        _outproj_kernel,
        grid=(M // ROW_TILE,),
        in_specs=[
            pl.BlockSpec((ROW_TILE, D), lambda m: (m, 0)),
            _mod_spec(l, who0, rows_per_mod, 1),
            pl.BlockSpec((ROW_TILE, ATT_WIDTH), lambda m: (m, 0)),
            pl.BlockSpec((ROW_TILE, RET_WIDTH), lambda m: (m, 0)),
            pl.BlockSpec((ROW_TILE, GLA_WIDTH), lambda m: (m, 0)),
            pl.BlockSpec((None, ATT_WIDTH, D), lambda m: (l, 0, 0)),
            pl.BlockSpec((None, RET_WIDTH, D), lambda m: (l, ATT_WIDTH // RET_WIDTH, 0)),
            pl.BlockSpec((None, GLA_WIDTH, D), lambda m: (l, (ATT_WIDTH + RET_WIDTH) // GLA_WIDTH, 0)),
            pl.BlockSpec((None, None, 1, D), lambda m: (l, 1, 0, 0)),
            pl.BlockSpec((None, None, 1, D), lambda m: (l, 1, 0, 0)),
        ],
        out_specs=pl.BlockSpec((ROW_TILE, D), lambda m: (m, 0)),
        out_shape=jax.ShapeDtypeStruct((M, D), F32),
        compiler_params=_params("parallel"),
        name="outproj",
    )(x, mod, att, ret, gla, w, w, w, ln_g, ln_b)


def _trunk_layer(x, B, S, l, who0, w, rope_tabs, ctx, attn_tq):
    is_context = ctx is None
    rows_per_mod = x.shape[0] if is_context else S
    mod = w["mod"]
    x = _ffn(x, mod, who0, rows_per_mod, w["ffn_w_in"], w["ffn_w_out"], w["ln_g"], w["ln_b"], l, 0)
    P = _inproj(x, mod, who0, rows_per_mod, w["mix_w_in"], l)
    cache = None if is_context else (ctx[0], ctx[1])
    s_ret0 = None if is_context else ctx[2]
    s_gla0 = None if is_context else ctx[3]
    att_out = _attention(P, B, S, w["q_norm"], w["k_norm"], l, rope_tabs, cache, is_context, attn_tq)
    ret_out = _retention(P, B, S, w["log_decay"], l, rope_tabs, s_ret0, is_context)
    gla_out = _gla(P, B, S, w["gla_wa"], w["gla_ba"], w["gla_norm"], l, s_gla0, is_context)
    x = _outproj(x, mod, who0, rows_per_mod, att_out[0], ret_out[0], gla_out[0],
                 w["mix_w_out"], w["ln_g"], w["ln_b"], l)
    x = _ffn(x, mod, who0, rows_per_mod, w["ffn_w_in"], w["ffn_w_out"], w["ln_g"], w["ln_b"], l, 1)
    new_ctx = (att_out[1], att_out[2], ret_out[1], gla_out[1]) if is_context else None
    return x, new_ctx


def _rope_tables(rows):
    row = jnp.repeat(jnp.arange(rows, dtype=F32), GRID_W)
    col = jnp.tile(jnp.arange(GRID_W, dtype=F32), rows)
    n_freq = ATT_HEAD_DIM // 4
    inv = ROPE_THETA ** (-jnp.arange(n_freq, dtype=F32) / n_freq)
    ang = jnp.concatenate([row[:, None] * inv, col[:, None] * inv], axis=-1)
    cos, sin = jnp.cos(ang), jnp.sin(ang)
    cos_full = jnp.repeat(cos, 2, axis=-1)
    sin_signed = jnp.stack([-sin, sin], axis=-1).reshape(ang.shape[0], ATT_HEAD_DIM)
    return cos_full, sin_signed


def kernel(x_prompt, x_sample, c, cache_attn_k, cache_attn_v, state_ret, state_gla, c_ctx,
           w_mod, b_mod, ln_g, ln_b, ffn_w_in, ffn_w_out, mix_w_in, mix_w_out,
           att_q_norm, att_k_norm, ret_log_decay, gla_w_a2, gla_b_a, gla_norm):
    B_ctx, S_ctx, D = x_prompt.shape
    B_lat, S_lat, _ = x_sample.shape

    gla_wa = jnp.zeros((DEPTH, 2, LANES, GLA_HEADS * GLA_DK), BF16)
    for d in range(2):
        gla_wa = gla_wa.at[:, d, d * GLA_GATE_RANK:(d + 1) * GLA_GATE_RANK, :].set(gla_w_a2[:, d].astype(BF16))

    cvec = jnp.concatenate([c_ctx[None, :], c, jnp.zeros((MOD_ROWS - 1 - B_lat, D), F32)], axis=0)
    mod = _modulation(cvec, w_mod, b_mod[:, None, :]).reshape(DEPTH, MOD_ROWS, N_MOD, D)

    w = dict(
        mod=mod,
        ffn_w_in=ffn_w_in.astype(BF16),
        ffn_w_out=ffn_w_out.astype(BF16),
        mix_w_in=jnp.pad(mix_w_in.astype(BF16), ((0, 0), (0, 0), (0, PROJ_COLS_PADDED - PROJ_COLS))),
        mix_w_out=mix_w_out.astype(BF16),
        ln_g=ln_g[:, :, None, :], ln_b=ln_b[:, :, None, :],
        q_norm=att_q_norm[:, None, :], k_norm=att_k_norm[:, None, :],
        log_decay=ret_log_decay, gla_wa=gla_wa, gla_ba=gla_b_a[:, :, None, :],
        gla_norm=gla_norm[:, None, :])

    h = x_prompt.reshape(B_ctx * S_ctx, D)
    ks_l, vs_l, sr_l, sg_l = [], [], [], []
    for l in range(DEPTH):
        h, (k_l, v_l, s_r, s_g) = _trunk_layer(h, B_ctx, S_ctx, l, 0, w, None, None, S_ctx)
        ks_l.append(k_l.reshape(B_ctx, S_ctx, ATT_KV_HEADS, ATT_HEAD_DIM))
        vs_l.append(v_l.reshape(B_ctx, S_ctx, ATT_KV_HEADS, ATT_HEAD_DIM))
        sr_l.append(s_r)
        sg_l.append(s_g.reshape(B_ctx, 2, GLA_HEADS, GLA_DK, GLA_DV))
    y_prompt = h.reshape(B_ctx, S_ctx, D)

    rope_tabs = _rope_tables(S_lat // GRID_W)
    ctx = (cache_attn_k.transpose(0, 1, 3, 2, 4), cache_attn_v.transpose(0, 1, 3, 2, 4), state_ret,
           state_gla.reshape(B_lat, DEPTH, 2, GLA_PAIRS, LANES, GLA_DV))
    g = x_sample.reshape(B_lat * S_lat, D)
    for l in range(DEPTH):
        g, _ = _trunk_layer(g, B_lat, S_lat, l, 1, w, rope_tabs, ctx, 256)
    y_sample = g.reshape(B_lat, S_lat, D)

    return (y_prompt, y_sample, jnp.stack(ks_l, axis=1), jnp.stack(vs_l, axis=1),
            jnp.stack(sr_l, axis=1), jnp.stack(sg_l, axis=1))
```

```python
import functools
import math

import jax
import jax.numpy as jnp
from jax import lax
from jax.experimental import pallas as pl
from jax.experimental.pallas import tpu as pltpu

F32 = jnp.float32
BF16 = jnp.bfloat16

D_MODEL = 2048
DEPTH = 2
GRID_W = 64
ATT_HEAD_DIM = 128
ATT_HEADS = 8
ATT_KV_HEADS = 2
ATT_GROUP = ATT_HEADS // ATT_KV_HEADS
ATT_WIDTH = ATT_HEADS * ATT_HEAD_DIM
ATT_KV_CHUNK = 512
ROPE_THETA = 10000.0
RET_DK = 128
RET_DV = 128
RET_HEADS = 4
RET_WIDTH = RET_HEADS * RET_DV
RET_CHUNK = 128
GLA_DK = 64
GLA_DV = 128
GLA_HEADS = 4
GLA_PAIRS = GLA_HEADS // 2
GLA_WIDTH = GLA_HEADS * GLA_DV
GLA_GATE_RANK = 16
GLA_TAU = 16.0
GLA_CHUNK = 64
D_FF = 5632
N_MOD = 9
MACARON_WEIGHT = 0.5
DEEPNORM_ALPHA = (2 * DEPTH) ** 0.25
LN_EPS = 1e-5
RMS_EPS = 1e-6

LANES = 128
PROJ_COLS = 5152
PROJ_COLS_PADDED = 5376
PROJ_TN = 768
COL_AQ, COL_AK, COL_AV = 0, 8, 10
COL_RQ, COL_RK, COL_RV, COL_RG = 12, 16, 20, 24
COL_GQ, COL_GK, COL_GV, COL_GR, COL_GA = 28, 30, 32, 36, 40

VMEM_LIMIT_BYTES = 56 * 1024 * 1024
ROW_TILE = 512
PROJ_ROW_TILE = 1024
FF_TILE = 256
FFN_ROW_TILE = 1024
MOD_TN = 1024
LOOP_UNROLL = 4
MOD_ROWS = 8


def _params(*sem):
    return pltpu.CompilerParams(dimension_semantics=sem, vmem_limit_bytes=VMEM_LIMIT_BYTES)


def _dot(a, b):
    return jnp.dot(a, b, preferred_element_type=F32)


def _dot_nt(a, b):
    return lax.dot_general(a, b, (((1,), (1,)), ((), ())), preferred_element_type=F32)


def _dot_tn(a, b):
    return lax.dot_general(a, b, (((0,), (0,)), ((), ())), preferred_element_type=F32)


def _silu(x):
    return x * jax.nn.sigmoid(x)


def _layer_norm(y, g, b):
    mu = jnp.mean(y, axis=-1, keepdims=True)
    d = y - mu
    var = jnp.mean(d * d, axis=-1, keepdims=True)
    return d * lax.rsqrt(var + LN_EPS) * g + b


def _rms_norm(x, g):
    return x * lax.rsqrt(jnp.mean(x * x, axis=-1, keepdims=True) + RMS_EPS) * g


def _rope(x, cos, sin_signed):
    lane = lax.broadcasted_iota(jnp.int32, x.shape, 1)
    partner = jnp.where((lane & 1) == 0, pltpu.roll(x, LANES - 1, 1), pltpu.roll(x, 1, 1))
    return x * cos + partner * sin_signed


def _mod_spec(l, who0, rows_per_mod, ndim_grid, row_tile=ROW_TILE):
    if ndim_grid == 1:
        return pl.BlockSpec((None, None, N_MOD, D_MODEL),
                            lambda m: (l, who0 + (m * row_tile) // rows_per_mod, 0, 0))
    return pl.BlockSpec((None, None, N_MOD, D_MODEL),
                        lambda m, n: (l, who0 + (m * row_tile) // rows_per_mod, 0, 0))


def _mod_kernel(c_ref, w_ref, b_ref, o_ref):
    a = _silu(c_ref[...]).astype(BF16)
    o_ref[...] = _dot(a, w_ref[...].astype(BF16)) + b_ref[...]


def _modulation(cvec, w_mod, b_mod):
    L, D, N = w_mod.shape
    return pl.pallas_call(
        _mod_kernel,
        grid=(L, N // MOD_TN),
        in_specs=[
            pl.BlockSpec((MOD_ROWS, D), lambda l, n: (0, 0)),
            pl.BlockSpec((None, D, MOD_TN), lambda l, n: (l, 0, n)),
            pl.BlockSpec((None, 1, MOD_TN), lambda l, n: (l, 0, n)),
        ],
        out_specs=pl.BlockSpec((None, MOD_ROWS, MOD_TN), lambda l, n: (l, 0, n)),
        out_shape=jax.ShapeDtypeStruct((L, MOD_ROWS, N), F32),
        compiler_params=_params("parallel", "parallel"),
        name="modulation",
    )(cvec, w_mod, b_mod)


def _ffn_kernel(x_ref, mod_ref, wg_ref, wu_ref, wo_ref, g_ref, b_ref, o_ref, u_ref, *, mod_base):
    f = pl.program_id(1)

    @pl.when(f == 0)
    def _():
        shift = mod_ref[mod_base:mod_base + 1, :]
        scale = mod_ref[mod_base + 1:mod_base + 2, :]
        u_ref[...] = (x_ref[...] * (1.0 + scale) + shift).astype(BF16)
        o_ref[...] = jnp.zeros_like(o_ref)

    u = u_ref[...]
    gate = _dot(u, wg_ref[...].astype(BF16))
    up = _dot(u, wu_ref[...].astype(BF16))
    act = (_silu(gate) * up).astype(BF16)
    o_ref[...] += _dot(act, wo_ref[...].astype(BF16))

    @pl.when(f == pl.num_programs(1) - 1)
    def _():
        g3 = mod_ref[mod_base + 2:mod_base + 3, :]
        y = DEEPNORM_ALPHA * x_ref[...] + MACARON_WEIGHT * (g3 * o_ref[...])
        o_ref[...] = _layer_norm(y, g_ref[...], b_ref[...])


def _ffn(x, mod, who0, rows_per_mod, w_in, w_out, ln_g, ln_b, l, half):
    M, D = x.shape
    nf = D_FF // FF_TILE
    sub = 2 * half
    return pl.pallas_call(
        functools.partial(_ffn_kernel, mod_base=3 * sub),
        grid=(M // FFN_ROW_TILE, nf),
        in_specs=[
            pl.BlockSpec((FFN_ROW_TILE, D), lambda m, f: (m, 0), pipeline_mode=pl.Buffered(1)),
            _mod_spec(l, who0, rows_per_mod, 2, FFN_ROW_TILE),
            pl.BlockSpec((None, None, D, FF_TILE), lambda m, f: (l, half, 0, f)),
            pl.BlockSpec((None, None, D, FF_TILE), lambda m, f: (l, half, 0, f + nf)),
            pl.BlockSpec((None, None, FF_TILE, D), lambda m, f: (l, half, f, 0)),
            pl.BlockSpec((None, None, 1, D), lambda m, f: (l, sub, 0, 0)),
            pl.BlockSpec((None, None, 1, D), lambda m, f: (l, sub, 0, 0)),
        ],
        out_specs=pl.BlockSpec((FFN_ROW_TILE, D), lambda m, f: (m, 0)),
        out_shape=jax.ShapeDtypeStruct((M, D), F32),
        scratch_shapes=[pltpu.VMEM((FFN_ROW_TILE, D), BF16)],
        compiler_params=_params("parallel", "arbitrary"),
        name="ffn",
    )(x, mod, w_in, w_in, w_out, ln_g, ln_b)


def _inproj_kernel(x_ref, mod_ref, w_ref, o_ref, u_ref):
    @pl.when(pl.program_id(1) == 0)
    def _():
        shift = mod_ref[3:4, :]
        scale = mod_ref[4:5, :]
        u_ref[...] = (x_ref[...] * (1.0 + scale) + shift).astype(BF16)

    o_ref[...] = _dot(u_ref[...], w_ref[...]).astype(o_ref.dtype)


def _inproj(x, mod, who0, rows_per_mod, w, l):
    M, D = x.shape
    N = w.shape[2]
    return pl.pallas_call(
        _inproj_kernel,
        grid=(M // PROJ_ROW_TILE, N // PROJ_TN),
        in_specs=[
            pl.BlockSpec((PROJ_ROW_TILE, D), lambda m, n: (m, 0)),
            _mod_spec(l, who0, rows_per_mod, 2, PROJ_ROW_TILE),
            pl.BlockSpec((None, D, PROJ_TN), lambda m, n: (l, 0, n)),
        ],
        out_specs=pl.BlockSpec((PROJ_ROW_TILE, PROJ_TN), lambda m, n: (m, n)),
        out_shape=jax.ShapeDtypeStruct((M, N), BF16),
        scratch_shapes=[pltpu.VMEM((PROJ_ROW_TILE, D), BF16)],
        compiler_params=_params("parallel", "arbitrary"),
        name="inproj",
    )(x, mod, w)


def _attn_kernel(*refs, rope, cache_len, emit_kv, tq):
    it = iter(refs)
    q_ref, k_ref, v_ref, qn_ref, kn_ref = (next(it) for _ in range(5))
    if rope:
        cosq_ref, sinq_ref, cosk_ref, sink_ref = (next(it) for _ in range(4))
    if cache_len:
        ck_ref, cv_ref = next(it), next(it)
    o_ref = next(it)
    if emit_kv:
        newk_ref, newv_ref = next(it), next(it)
    kb_ref, vt_ref = next(it), next(it)
    T = kb_ref.shape[0]
    chunks = ([(0, cache_len)] if cache_len else []) + [
        (o, min(ATT_KV_CHUNK, T - o)) for o in range(cache_len, T, ATT_KV_CHUNK)]

    @pl.when(pl.program_id(2) == 0)
    def _():
        k = _rms_norm(k_ref[...].astype(F32), kn_ref[...])
        v = v_ref[...].astype(F32)
        if emit_kv:
            newk_ref[...] = k
            newv_ref[...] = v
        if rope:
            k = _rope(k, cosk_ref[...], sink_ref[...])
        if cache_len:
            k = jnp.concatenate([ck_ref[...], k], axis=0)
            v = jnp.concatenate([cv_ref[...], v], axis=0)
        kb_ref[...] = k.astype(BF16)
        for o, n in chunks:
            vt_ref[:, o:o + n] = v[o:o + n, :].T.astype(BF16)

    q = q_ref[...].astype(F32)
    heads = []
    for g in range(ATT_GROUP):
        qg = _rms_norm(q[:, g * LANES:(g + 1) * LANES], qn_ref[...])
        if rope:
            qg = _rope(qg, cosq_ref[...], sinq_ref[...])
        heads.append(qg.T)
    q_t = jnp.concatenate(heads, axis=1).astype(BF16)

    c2 = (ATT_HEAD_DIM ** -0.5) * math.log2(math.e)
    m = denom = acc = None
    for o, n in chunks:
        s = _dot(kb_ref[o:o + n, :], q_t)
        m_c = jnp.max(s, axis=0, keepdims=True)
        m = m_c if m is None else jnp.maximum(m, m_c)
    for o, n in chunks:
        s = _dot(kb_ref[o:o + n, :], q_t)
        p = jnp.exp2((s - m) * c2)
        d_c = jnp.sum(p, axis=0, keepdims=True)
        a_c = _dot(vt_ref[:, o:o + n], p.astype(BF16))
        denom = d_c if denom is None else denom + d_c
        acc = a_c if acc is None else acc + a_c
    o_t = acc * (1.0 / denom)
    for g in range(ATT_GROUP):
        o_ref[:, g * LANES:(g + 1) * LANES] = o_t[:, g * tq:(g + 1) * tq].T.astype(o_ref.dtype)


def _attention(P, B, S, q_norm, k_norm, l, rope_tabs, cache, emit_kv, tq):
    nq = S // tq
    cache_len = cache[0].shape[3] if cache is not None else 0
    T = cache_len + S
    gw = ATT_GROUP * ATT_HEAD_DIM
    norm_spec = pl.BlockSpec((None, 1, LANES), lambda b, j, i: (l, 0, 0))
    in_specs = [
        pl.BlockSpec((tq, gw), lambda b, j, i: (b * nq + i, j)),
        pl.BlockSpec((S, LANES), lambda b, j, i: (b, COL_AK + j)),
        pl.BlockSpec((S, LANES), lambda b, j, i: (b, COL_AV + j)),
        norm_spec, norm_spec,
    ]
    args = [P, P, P, q_norm, k_norm]
    if rope_tabs is not None:
        cos, sin = rope_tabs
        in_specs += [
            pl.BlockSpec((tq, LANES), lambda b, j, i: (i, 0)),
            pl.BlockSpec((tq, LANES), lambda b, j, i: (i, 0)),
            pl.BlockSpec((S, LANES), lambda b, j, i: (0, 0)),
            pl.BlockSpec((S, LANES), lambda b, j, i: (0, 0)),
        ]
        args += [cos, sin, cos, sin]
    if cache is not None:
        spec = pl.BlockSpec((None, None, None, cache_len, LANES), lambda b, j, i: (b, l, j, 0, 0))
        in_specs += [spec, spec]
        args += list(cache)
    out_specs = [pl.BlockSpec((tq, gw), lambda b, j, i: (b * nq + i, j))]
    out_shape = [jax.ShapeDtypeStruct((B * S, ATT_WIDTH), BF16)]
    if emit_kv:
        kv_spec = pl.BlockSpec((S, LANES), lambda b, j, i: (b, j))
        out_specs += [kv_spec, kv_spec]
        out_shape += [jax.ShapeDtypeStruct((B * S, ATT_KV_HEADS * ATT_HEAD_DIM), F32)] * 2
    return pl.pallas_call(
        functools.partial(_attn_kernel, rope=rope_tabs is not None, cache_len=cache_len,
                          emit_kv=emit_kv, tq=tq),
        grid=(B, ATT_KV_HEADS, nq),
        in_specs=in_specs,
        out_specs=out_specs,
        out_shape=out_shape,
        scratch_shapes=[pltpu.VMEM((T, LANES), BF16), pltpu.VMEM((LANES, T), BF16)],
        compiler_params=_params("parallel", "parallel", "arbitrary"),
        name="attention",
    )(*args)


def _ret_kernel(*refs, l, rope, has_s0, emit_state, S):
    it = iter(refs)
    lg_ref, q_ref, k_ref, v_ref, g_ref = (next(it) for _ in range(5))
    if rope:
        cos_ref, sin_ref = next(it), next(it)
    if has_s0:
        s0_ref = next(it)
    o_ref = next(it)
    if emit_state:
        st_ref = next(it)
    qb_ref, kb_ref, qdf_ref, qdb_ref, kdf_ref, kdb_ref, accf_ref, accb_ref = (next(it) for _ in range(8))

    h = pl.program_id(1)
    C = RET_CHUNK
    n = S // C
    lg_f = lg_ref[l, 0, h]
    lg_b = lg_ref[l, 1, h]
    q = q_ref[...].astype(F32)
    k = k_ref[...].astype(F32) * (RET_DK ** -0.5)
    if rope:
        q = _rope(q, cos_ref[...], sin_ref[...])
        k = _rope(k, cos_ref[...], sin_ref[...])

    t = lax.broadcasted_iota(jnp.int32, (C, LANES), 0).astype(F32)

    def scaled(x, expo):
        return (x.reshape(n, C, LANES) * jnp.exp(expo)[None]).reshape(S, LANES).astype(BF16)

    qb_ref[...] = q.astype(BF16)
    kb_ref[...] = k.astype(BF16)
    qdf_ref[...] = scaled(q, (t + 1.0) * lg_f)
    qdb_ref[...] = scaled(q, (C - t) * lg_b)
    kdf_ref[...] = scaled(k, (C - 1.0 - t) * lg_f)
    kdb_ref[...] = scaled(k, t * lg_b)

    ii = lax.broadcasted_iota(jnp.int32, (C, C), 0).astype(F32)
    jj = lax.broadcasted_iota(jnp.int32, (C, C), 1).astype(F32)
    dmat = (jnp.where(ii >= jj, jnp.exp(jnp.maximum(ii - jj, 0.0) * lg_f), 0.0)
            + jnp.where(jj >= ii, jnp.exp(jnp.maximum(jj - ii, 0.0) * lg_b), 0.0))
    decay_f = jnp.exp(jnp.full((RET_DK, RET_DV), C * lg_f, F32))
    decay_b = jnp.exp(jnp.full((RET_DK, RET_DV), C * lg_b, F32))
    if has_s0:
        state0 = (s0_ref[0], s0_ref[1])
    else:
        state0 = (jnp.zeros((RET_DK, RET_DV), F32), jnp.zeros((RET_DK, RET_DV), F32))

    def chunk(c, carry):
        state_f, state_b = carry
        sf = pl.ds(pl.multiple_of(c * C, C), C)
        sb = pl.ds(pl.multiple_of((n - 1 - c) * C, C), C)
        v_f = v_ref[sf, :]
        v_b = v_ref[sb, :]
        scores = _dot_nt(qb_ref[sf, :], kb_ref[sf, :]) * dmat
        accf_ref[sf, :] = _dot(scores.astype(BF16), v_f) + _dot(qdf_ref[sf, :], state_f.astype(BF16))
        accb_ref[sb, :] = _dot(qdb_ref[sb, :], state_b.astype(BF16))
        state_f = decay_f * state_f + _dot_tn(kdf_ref[sf, :], v_f)
        state_b = decay_b * state_b + _dot_tn(kdb_ref[sb, :], v_b)
        return state_f, state_b

    state_f, state_b = lax.fori_loop(0, n, chunk, state0, unroll=LOOP_UNROLL)
    if emit_state:
        st_ref[0] = state_f
        st_ref[1] = state_b

    y = accf_ref[...] + accb_ref[...]
    mu = jnp.mean(y, axis=-1, keepdims=True)
    yc = y - mu
    var = jnp.mean(yc * yc, axis=-1, keepdims=True)
    gate = _silu(g_ref[...].astype(F32))
    o_ref[...] = (yc * lax.rsqrt(var + LN_EPS) * gate).astype(o_ref.dtype)


def _retention(P, B, S, log_decay, l, rope_tabs, s0, emit_state):
    def col(c0):
        return pl.BlockSpec((S, LANES), lambda b, h: (b, c0 + h))

    in_specs = [pl.BlockSpec(memory_space=pltpu.SMEM), col(COL_RQ), col(COL_RK), col(COL_RV), col(COL_RG)]
    args = [log_decay, P, P, P, P]
    if rope_tabs is not None:
        tab = pl.BlockSpec((S, LANES), lambda b, h: (0, 0))
        in_specs += [tab, tab]
        args += list(rope_tabs)
    if s0 is not None:
        in_specs.append(pl.BlockSpec((None, None, 2, None, RET_DK, RET_DV), lambda b, h: (b, l, 0, h, 0, 0)))
        args.append(s0)
    out_specs = [pl.BlockSpec((S, LANES), lambda b, h: (b, h))]
    out_shape = [jax.ShapeDtypeStruct((B * S, RET_WIDTH), BF16)]
    if emit_state:
        out_specs.append(pl.BlockSpec((None, 2, None, RET_DK, RET_DV), lambda b, h: (b, 0, h, 0, 0)))
        out_shape.append(jax.ShapeDtypeStruct((B, 2, RET_HEADS, RET_DK, RET_DV), F32))
    return pl.pallas_call(
        functools.partial(_ret_kernel, l=l, rope=rope_tabs is not None, has_s0=s0 is not None,
                          emit_state=emit_state, S=S),
        grid=(B, RET_HEADS),
        in_specs=in_specs,
        out_specs=out_specs,
        out_shape=out_shape,
        scratch_shapes=[pltpu.VMEM((S, RET_DK), BF16)] * 6 + [pltpu.VMEM((S, RET_DV), F32)] * 2,
        compiler_params=_params("parallel", "parallel"),
        name="retention",
    )(*args)


def _gla_kernel(*refs, has_s0, emit_state, S):
    it = iter(refs)
    q_ref, k_ref, v_ref, gr_ref, ga_ref, wa_ref, ba_ref, gn_ref = (next(it) for _ in range(8))
    if has_s0:
        s0_ref = next(it)
    o_ref = next(it)
    if emit_state:
        st_ref = next(it)
    la_ref, qt_ref, kt_ref, ks_ref, dec_ref, acc_ref = (next(it) for _ in range(6))

    C = GLA_CHUNK
    n = S // C
    ga = ga_ref[...]
    for d in range(2):
        pre = _dot(ga, wa_ref[d]) + ba_ref[d]
        la_ref[d] = (jnp.minimum(pre, 0.0) - jnp.log1p(jnp.exp(-jnp.abs(pre)))) * (1.0 / GLA_TAU)

    ii = lax.broadcasted_iota(jnp.int32, (C, C), 0)
    jj = lax.broadcasted_iota(jnp.int32, (C, C), 1)
    tri = [jj <= ii, jj >= ii]
    tri_b = [jnp.where(m, 1.0, 0.0).astype(BF16) for m in tri]
    tri_f = [jnp.where(m, 1.0, 0.0) for m in tri]
    lane = lax.broadcasted_iota(jnp.int32, (C, LANES), 1)
    head_lanes = [lane < GLA_DK, lane >= GLA_DK]

    def prepare(c, carry):
        sl = pl.ds(pl.multiple_of(c * C, C), C)
        qc = q_ref[sl, :].astype(F32) * (GLA_DK ** -0.5)
        kc = k_ref[sl, :].astype(F32)
        for d in range(2):
            la = la_ref[d, sl, :]
            hi = la.astype(BF16)
            r1 = la - hi.astype(F32)
            mid = r1.astype(BF16)
            lo = (r1 - mid.astype(F32)).astype(BF16)
            b = _dot(tri_b[d], hi) + _dot(tri_b[d], mid) + _dot(tri_b[d], lo)
            b_end = b[C - 1:C, :] if d == 0 else b[0:1, :]
            qt_ref[d, sl, :] = (qc * jnp.exp(b)).astype(BF16)
            kt_ref[d, sl, :] = (kc * jnp.exp(-b)).astype(BF16)
            ks_ref[d, sl, :] = (kc * jnp.exp(b_end - b)).astype(BF16)
            dec_ref[d, c] = jnp.exp(b_end)
        return carry

    lax.fori_loop(0, n, prepare, 0, unroll=LOOP_UNROLL)

    if has_s0:
        state0 = (s0_ref[0].T, s0_ref[1].T)
    else:
        state0 = (jnp.zeros((GLA_DV, LANES), F32), jnp.zeros((GLA_DV, LANES), F32))

    def chunk(c, carry):
        new_states = []
        for d in range(2):
            cc = c if d == 0 else n - 1 - c
            sl = pl.ds(pl.multiple_of(cc * C, C), C)
            q_t, k_t, k_s = qt_ref[d, sl, :], kt_ref[d, sl, :], ks_ref[d, sl, :]
            state_b = carry[d].astype(BF16)
            new_state = carry[d] * dec_ref[d, cc]
            for i in range(2):
                q_i = jnp.where(head_lanes[i], q_t, jnp.zeros_like(q_t))
                v_i = v_ref[sl, i * GLA_DV:(i + 1) * GLA_DV]
                att = _dot_nt(q_i, k_t) * tri_f[d]
                acc_ref[d, i, sl, :] = _dot(att.astype(BF16), v_i) + _dot_nt(q_i, state_b)
                new_state = new_state + _dot_tn(v_i, jnp.where(head_lanes[i], k_s, jnp.zeros_like(k_s)))
            new_states.append(new_state)
        return tuple(new_states)

    states = lax.fori_loop(0, n, chunk, state0, unroll=LOOP_UNROLL)
    if emit_state:
        st_ref[0] = states[0].T
        st_ref[1] = states[1].T

    for i in range(2):
        y = _rms_norm(acc_ref[0, i] + acc_ref[1, i], gn_ref[...])
        gate = _silu(gr_ref[:, i * GLA_DV:(i + 1) * GLA_DV].astype(F32))
        o_ref[:, i * GLA_DV:(i + 1) * GLA_DV] = (y * gate).astype(o_ref.dtype)


def _gla(P, B, S, wa, ba, gla_norm, l, s0, emit_state):
    pw = 2 * GLA_DV
    in_specs = [
        pl.BlockSpec((S, LANES), lambda b, p: (b, COL_GQ + p)),
        pl.BlockSpec((S, LANES), lambda b, p: (b, COL_GK + p)),
        pl.BlockSpec((S, pw), lambda b, p: (b, COL_GV // 2 + p)),
        pl.BlockSpec((S, pw), lambda b, p: (b, COL_GR // 2 + p)),
        pl.BlockSpec((S, LANES), lambda b, p: (b, COL_GA)),
        pl.BlockSpec((None, 2, LANES, LANES), lambda b, p: (l, 0, 0, p)),
        pl.BlockSpec((None, 2, 1, LANES), lambda b, p: (l, 0, 0, p)),
        pl.BlockSpec((None, 1, GLA_DV), lambda b, p: (l, 0, 0)),
    ]
    args = [P, P, P, P, P, wa, ba, gla_norm]
    if s0 is not None:
        in_specs.append(pl.BlockSpec((None, None, 2, None, LANES, GLA_DV), lambda b, p: (b, l, 0, p, 0, 0)))
        args.append(s0)
    out_specs = [pl.BlockSpec((S, pw), lambda b, p: (b, p))]
    out_shape = [jax.ShapeDtypeStruct((B * S, GLA_WIDTH), BF16)]
    if emit_state:
        out_specs.append(pl.BlockSpec((None, 2, None, LANES, GLA_DV), lambda b, p: (b, 0, p, 0, 0)))
        out_shape.append(jax.ShapeDtypeStruct((B, 2, GLA_PAIRS, LANES, GLA_DV), F32))
    return pl.pallas_call(
        functools.partial(_gla_kernel, has_s0=s0 is not None, emit_state=emit_state, S=S),
        grid=(B, GLA_PAIRS),
        in_specs=in_specs,
        out_specs=out_specs,
        out_shape=out_shape,
        scratch_shapes=[pltpu.VMEM((2, S, LANES), F32),
                        pltpu.VMEM((2, S, LANES), BF16), pltpu.VMEM((2, S, LANES), BF16),
                        pltpu.VMEM((2, S, LANES), BF16),
                        pltpu.VMEM((2, S // GLA_CHUNK, 1, LANES), F32),
                        pltpu.VMEM((2, 2, S, GLA_DV), F32)],
        compiler_params=_params("parallel", "parallel"),
        name="gla",
    )(*args)


def _outproj_kernel(x_ref, mod_ref, a_ref, r_ref, gl_ref, wa_ref, wr_ref, wg_ref, g_ref, b_ref, o_ref):
    y = _dot(a_ref[...], wa_ref[...]) + _dot(r_ref[...], wr_ref[...]) + _dot(gl_ref[...], wg_ref[...])
    y = DEEPNORM_ALPHA * x_ref[...] + mod_ref[5:6, :] * y
    o_ref[...] = _layer_norm(y, g_ref[...], b_ref[...])


def _outproj(x, mod, who0, rows_per_mod, att, ret, gla, w, ln_g, ln_b, l):
    M, D = x.shape
    return pl.pallas_call(
        _outproj_kernel,
        grid=(M // ROW_TILE,),
        in_specs=[
            pl.BlockSpec((ROW_TILE, D), lambda m: (m, 0)),
            _mod_spec(l, who0, rows_per_mod, 1),
            pl.BlockSpec((ROW_TILE, ATT_WIDTH), lambda m: (m, 0)),
            pl.BlockSpec((ROW_TILE, RET_WIDTH), lambda m: (m, 0)),
            pl.BlockSpec((ROW_TILE, GLA_WIDTH), lambda m: (m, 0)),
            pl.BlockSpec((None, ATT_WIDTH, D), lambda m: (l, 0, 0)),
            pl.BlockSpec((None, RET_WIDTH, D), lambda m: (l, ATT_WIDTH // RET_WIDTH, 0)),
            pl.BlockSpec((None, GLA_WIDTH, D), lambda m: (l, (ATT_WIDTH + RET_WIDTH) // GLA_WIDTH, 0)),
            pl.BlockSpec((None, None, 1, D), lambda m: (l, 1, 0, 0)),
            pl.BlockSpec((None, None, 1, D), lambda m: (l, 1, 0, 0)),
        ],
        out_specs=pl.BlockSpec((ROW_TILE, D), lambda m: (m, 0)),
        out_shape=jax.ShapeDtypeStruct((M, D), F32),
        compiler_params=_params("parallel"),
        name="outproj",
    )(x, mod, att, ret, gla, w, w, w, ln_g, ln_b)


def _trunk_layer(x, B, S, l, who0, w, rope_tabs, ctx, attn_tq):
    is_context = ctx is None
    rows_per_mod = x.shape[0] if is_context else S
    mod = w["mod"]
    x = _ffn(x, mod, who0, rows_per_mod, w["ffn_w_in"], w["ffn_w_out"], w["ln_g"], w["ln_b"], l, 0)
    P = _inproj(x, mod, who0, rows_per_mod, w["mix_w_in"], l)
    cache = None if is_context else (ctx[0], ctx[1])
    s_ret0 = None if is_context else ctx[2]
    s_gla0 = None if is_context else ctx[3]
    att_out = _attention(P, B, S, w["q_norm"], w["k_norm"], l, rope_tabs, cache, is_context, attn_tq)
    ret_out = _retention(P, B, S, w["log_decay"], l, rope_tabs, s_ret0, is_context)
    gla_out = _gla(P, B, S, w["gla_wa"], w["gla_ba"], w["gla_norm"], l, s_gla0, is_context)
    x = _outproj(x, mod, who0, rows_per_mod, att_out[0], ret_out[0], gla_out[0],
                 w["mix_w_out"], w["ln_g"], w["ln_b"], l)
    x = _ffn(x, mod, who0, rows_per_mod, w["ffn_w_in"], w["ffn_w_out"], w["ln_g"], w["ln_b"], l, 1)
    new_ctx = (att_out[1], att_out[2], ret_out[1], gla_out[1]) if is_context else None
    return x, new_ctx


def _rope_tables(rows):
    row = jnp.repeat(jnp.arange(rows, dtype=F32), GRID_W)
    col = jnp.tile(jnp.arange(GRID_W, dtype=F32), rows)
    n_freq = ATT_HEAD_DIM // 4
    inv = ROPE_THETA ** (-jnp.arange(n_freq, dtype=F32) / n_freq)
    ang = jnp.concatenate([row[:, None] * inv, col[:, None] * inv], axis=-1)
    cos, sin = jnp.cos(ang), jnp.sin(ang)
    cos_full = jnp.repeat(cos, 2, axis=-1)
    sin_signed = jnp.stack([-sin, sin], axis=-1).reshape(ang.shape[0], ATT_HEAD_DIM)
    return cos_full, sin_signed


def kernel(x_prompt, x_sample, c, cache_attn_k, cache_attn_v, state_ret, state_gla, c_ctx,
           w_mod, b_mod, ln_g, ln_b, ffn_w_in, ffn_w_out, mix_w_in, mix_w_out,
           att_q_norm, att_k_norm, ret_log_decay, gla_w_a2, gla_b_a, gla_norm):
    B_ctx, S_ctx, D = x_prompt.shape
    B_lat, S_lat, _ = x_sample.shape

    gla_wa = jnp.zeros((DEPTH, 2, LANES, GLA_HEADS * GLA_DK), BF16)
    for d in range(2):
        gla_wa = gla_wa.at[:, d, d * GLA_GATE_RANK:(d + 1) * GLA_GATE_RANK, :].set(gla_w_a2[:, d].astype(BF16))

    cvec = jnp.concatenate([c_ctx[None, :], c, jnp.zeros((MOD_ROWS - 1 - B_lat, D), F32)], axis=0)
    mod = _modulation(cvec, w_mod, b_mod[:, None, :]).reshape(DEPTH, MOD_ROWS, N_MOD, D)

    w = dict(
        mod=mod,
        ffn_w_in=ffn_w_in,
        ffn_w_out=ffn_w_out,
        mix_w_in=jnp.pad(mix_w_in.astype(BF16), ((0, 0), (0, 0), (0, PROJ_COLS_PADDED - PROJ_COLS))),
        mix_w_out=mix_w_out.astype(BF16),
        ln_g=ln_g[:, :, None, :], ln_b=ln_b[:, :, None, :],
        q_norm=att_q_norm[:, None, :], k_norm=att_k_norm[:, None, :],
        log_decay=ret_log_decay, gla_wa=gla_wa, gla_ba=gla_b_a[:, :, None, :],
        gla_norm=gla_norm[:, None, :])

    h = x_prompt.reshape(B_ctx * S_ctx, D)
    ks_l, vs_l, sr_l, sg_l = [], [], [], []
    for l in range(DEPTH):
        h, (k_l, v_l, s_r, s_g) = _trunk_layer(h, B_ctx, S_ctx, l, 0, w, None, None, S_ctx)
        ks_l.append(k_l.reshape(B_ctx, S_ctx, ATT_KV_HEADS, ATT_HEAD_DIM))
        vs_l.append(v_l.reshape(B_ctx, S_ctx, ATT_KV_HEADS, ATT_HEAD_DIM))
        sr_l.append(s_r)
        sg_l.append(s_g.reshape(B_ctx, 2, GLA_HEADS, GLA_DK, GLA_DV))
    y_prompt = h.reshape(B_ctx, S_ctx, D)

    rope_tabs = _rope_tables(S_lat // GRID_W)
    ctx = (cache_attn_k.transpose(0, 1, 3, 2, 4), cache_attn_v.transpose(0, 1, 3, 2, 4), state_ret,
           state_gla.reshape(B_lat, DEPTH, 2, GLA_PAIRS, LANES, GLA_DV))
    g = x_sample.reshape(B_lat * S_lat, D)
    for l in range(DEPTH):
        g, _ = _trunk_layer(g, B_lat, S_lat, l, 1, w, rope_tabs, ctx, 256)
    y_sample = g.reshape(B_lat, S_lat, D)

    return (y_prompt, y_sample, jnp.stack(ks_l, axis=1), jnp.stack(vs_l, axis=1),
            jnp.stack(sr_l, axis=1), jnp.stack(sg_l, axis=1))
```

```python
import functools
import math

import jax
import jax.numpy as jnp
from jax import lax
from jax.experimental import pallas as pl
from jax.experimental.pallas import tpu as pltpu

F32 = jnp.float32
BF16 = jnp.bfloat16

D_MODEL = 2048
DEPTH = 2
GRID_W = 64
ATT_HEAD_DIM = 128
ATT_HEADS = 8
ATT_KV_HEADS = 2
ATT_GROUP = ATT_HEADS // ATT_KV_HEADS
ATT_WIDTH = ATT_HEADS * ATT_HEAD_DIM
ATT_KV_CHUNK = 512
ROPE_THETA = 10000.0
RET_DK = 128
RET_DV = 128
RET_HEADS = 4
RET_WIDTH = RET_HEADS * RET_DV
RET_CHUNK = 128
GLA_DK = 64
GLA_DV = 128
GLA_HEADS = 4
GLA_PAIRS = GLA_HEADS // 2
GLA_WIDTH = GLA_HEADS * GLA_DV
GLA_GATE_RANK = 16
GLA_TAU = 16.0
GLA_CHUNK = 64
D_FF = 5632
N_MOD = 9
MACARON_WEIGHT = 0.5
DEEPNORM_ALPHA = (2 * DEPTH) ** 0.25
LN_EPS = 1e-5
RMS_EPS = 1e-6

LANES = 128
PROJ_COLS = 5152
PROJ_COLS_PADDED = 5376
PROJ_TN = 768
COL_AQ, COL_AK, COL_AV = 0, 8, 10
COL_RQ, COL_RK, COL_RV, COL_RG = 12, 16, 20, 24
COL_GQ, COL_GK, COL_GV, COL_GR, COL_GA = 28, 30, 32, 36, 40

VMEM_LIMIT_BYTES = 56 * 1024 * 1024
ROW_TILE = 512
PROJ_ROW_TILE = 1024
FF_TILE = 512
FFN_ROW_TILE = 1024
MOD_TN = 1024
LOOP_UNROLL = 8
SCAN_GROUP = 4
MOD_ROWS = 8


def _params(*sem):
    return pltpu.CompilerParams(dimension_semantics=sem, vmem_limit_bytes=VMEM_LIMIT_BYTES)


def _dot(a, b):
    return jnp.dot(a, b, preferred_element_type=F32)


def _dot_nt(a, b):
    return lax.dot_general(a, b, (((1,), (1,)), ((), ())), preferred_element_type=F32)


def _dot_tn(a, b):
    return lax.dot_general(a, b, (((0,), (0,)), ((), ())), preferred_element_type=F32)


def _silu(x):
    return x * jax.nn.sigmoid(x)


def _layer_norm(y, g, b):
    mu = jnp.mean(y, axis=-1, keepdims=True)
    d = y - mu
    var = jnp.mean(d * d, axis=-1, keepdims=True)
    return d * lax.rsqrt(var + LN_EPS) * g + b


def _rms_norm(x, g):
    return x * lax.rsqrt(jnp.mean(x * x, axis=-1, keepdims=True) + RMS_EPS) * g


def _rope(x, cos, sin_signed):
    lane = lax.broadcasted_iota(jnp.int32, x.shape, 1)
    partner = jnp.where((lane & 1) == 0, pltpu.roll(x, LANES - 1, 1), pltpu.roll(x, 1, 1))
    return x * cos + partner * sin_signed


def _mod_spec(l, who0, rows_per_mod, ndim_grid, row_tile=ROW_TILE):
    if ndim_grid == 1:
        return pl.BlockSpec((None, None, N_MOD, D_MODEL),
                            lambda m: (l, who0 + (m * row_tile) // rows_per_mod, 0, 0))
    return pl.BlockSpec((None, None, N_MOD, D_MODEL),
                        lambda m, n: (l, who0 + (m * row_tile) // rows_per_mod, 0, 0))


def _mod_kernel(c_ref, w_ref, b_ref, o_ref):
    a = _silu(c_ref[...]).astype(BF16)
    o_ref[...] = _dot(a, w_ref[...].astype(BF16)) + b_ref[...]


def _modulation(cvec, w_mod, b_mod):
    L, D, N = w_mod.shape
    return pl.pallas_call(
        _mod_kernel,
        grid=(L, N // MOD_TN),
        in_specs=[
            pl.BlockSpec((MOD_ROWS, D), lambda l, n: (0, 0)),
            pl.BlockSpec((None, D, MOD_TN), lambda l, n: (l, 0, n)),
            pl.BlockSpec((None, 1, MOD_TN), lambda l, n: (l, 0, n)),
        ],
        out_specs=pl.BlockSpec((None, MOD_ROWS, MOD_TN), lambda l, n: (l, 0, n)),
        out_shape=jax.ShapeDtypeStruct((L, MOD_ROWS, N), F32),
        compiler_params=_params("parallel", "parallel"),
        name="modulation",
    )(cvec, w_mod, b_mod)


def _ffn_kernel(x_ref, mod_ref, wg_ref, wu_ref, wo_ref, g_ref, b_ref, o_ref, u_ref, *, mod_base):
    f = pl.program_id(1)

    @pl.when(f == 0)
    def _():
        shift = mod_ref[mod_base:mod_base + 1, :]
        scale = mod_ref[mod_base + 1:mod_base + 2, :]
        u_ref[...] = (x_ref[...] * (1.0 + scale) + shift).astype(BF16)
        o_ref[...] = jnp.zeros_like(o_ref)

    u = u_ref[...]
    gate = _dot(u, wg_ref[...])
    up = _dot(u, wu_ref[...])
    act = (_silu(gate) * up).astype(BF16)
    o_ref[...] += _dot(act, wo_ref[...])

    @pl.when(f == pl.num_programs(1) - 1)
    def _():
        g3 = mod_ref[mod_base + 2:mod_base + 3, :]
        y = DEEPNORM_ALPHA * x_ref[...] + MACARON_WEIGHT * (g3 * o_ref[...])
        o_ref[...] = _layer_norm(y, g_ref[...], b_ref[...])


def _ffn(x, mod, who0, rows_per_mod, w_in, w_out, ln_g, ln_b, l, half):
    M, D = x.shape
    nf = D_FF // FF_TILE
    sub = 2 * half
    return pl.pallas_call(
        functools.partial(_ffn_kernel, mod_base=3 * sub),
        grid=(M // FFN_ROW_TILE, nf),
        in_specs=[
            pl.BlockSpec((FFN_ROW_TILE, D), lambda m, f: (m, 0), pipeline_mode=pl.Buffered(1)),
            _mod_spec(l, who0, rows_per_mod, 2, FFN_ROW_TILE),
            pl.BlockSpec((None, None, D, FF_TILE), lambda m, f: (l, half, 0, f)),
            pl.BlockSpec((None, None, D, FF_TILE), lambda m, f: (l, half, 0, f + nf)),
            pl.BlockSpec((None, None, FF_TILE, D), lambda m, f: (l, half, f, 0)),
            pl.BlockSpec((None, None, 1, D), lambda m, f: (l, sub, 0, 0)),
            pl.BlockSpec((None, None, 1, D), lambda m, f: (l, sub, 0, 0)),
        ],
        out_specs=pl.BlockSpec((FFN_ROW_TILE, D), lambda m, f: (m, 0)),
        out_shape=jax.ShapeDtypeStruct((M, D), F32),
        scratch_shapes=[pltpu.VMEM((FFN_ROW_TILE, D), BF16)],
        compiler_params=_params("parallel", "arbitrary"),
        name="ffn",
    )(x, mod, w_in, w_in, w_out, ln_g, ln_b)


def _inproj_kernel(x_ref, mod_ref, w_ref, o_ref, u_ref):
    @pl.when(pl.program_id(1) == 0)
    def _():
        shift = mod_ref[3:4, :]
        scale = mod_ref[4:5, :]
        u_ref[...] = (x_ref[...] * (1.0 + scale) + shift).astype(BF16)

    o_ref[...] = _dot(u_ref[...], w_ref[...]).astype(o_ref.dtype)


def _inproj(x, mod, who0, rows_per_mod, w, l):
    M, D = x.shape
    N = w.shape[2]
    return pl.pallas_call(
        _inproj_kernel,
        grid=(M // PROJ_ROW_TILE, N // PROJ_TN),
        in_specs=[
            pl.BlockSpec((PROJ_ROW_TILE, D), lambda m, n: (m, 0)),
            _mod_spec(l, who0, rows_per_mod, 2, PROJ_ROW_TILE),
            pl.BlockSpec((None, D, PROJ_TN), lambda m, n: (l, 0, n)),
        ],
        out_specs=pl.BlockSpec((PROJ_ROW_TILE, PROJ_TN), lambda m, n: (m, n)),
        out_shape=jax.ShapeDtypeStruct((M, N), BF16),
        scratch_shapes=[pltpu.VMEM((PROJ_ROW_TILE, D), BF16)],
        compiler_params=_params("parallel", "arbitrary"),
        name="inproj",
    )(x, mod, w)


def _attn_kernel(*refs, rope, cache_len, emit_kv, tq):
    it = iter(refs)
    q_ref, k_ref, v_ref, qn_ref, kn_ref = (next(it) for _ in range(5))
    if rope:
        cosq_ref, sinq_ref, cosk_ref, sink_ref = (next(it) for _ in range(4))
    if cache_len:
        ck_ref, cv_ref = next(it), next(it)
    o_ref = next(it)
    if emit_kv:
        newk_ref, newv_ref = next(it), next(it)
    kb_ref, vt_ref = next(it), next(it)
    T = kb_ref.shape[0]
    chunks = ([(0, cache_len)] if cache_len else []) + [
        (o, min(ATT_KV_CHUNK, T - o)) for o in range(cache_len, T, ATT_KV_CHUNK)]

    @pl.when(pl.program_id(2) == 0)
    def _():
        k = _rms_norm(k_ref[...].astype(F32), kn_ref[...])
        v = v_ref[...].astype(F32)
        if emit_kv:
            newk_ref[...] = k
            newv_ref[...] = v
        if rope:
            k = _rope(k, cosk_ref[...], sink_ref[...])
        if cache_len:
            k = jnp.concatenate([ck_ref[...], k], axis=0)
            v = jnp.concatenate([cv_ref[...], v], axis=0)
        kb_ref[...] = k.astype(BF16)
        for o, n in chunks:
            vt_ref[:, o:o + n] = v[o:o + n, :].T.astype(BF16)

    q = q_ref[...].astype(F32)
    heads = []
    for g in range(ATT_GROUP):
        qg = _rms_norm(q[:, g * LANES:(g + 1) * LANES], qn_ref[...])
        if rope:
            qg = _rope(qg, cosq_ref[...], sinq_ref[...])
        heads.append(qg.T)
    q_t = jnp.concatenate(heads, axis=1).astype(BF16)

    c2 = (ATT_HEAD_DIM ** -0.5) * math.log2(math.e)
    m = denom = acc = None
    for o, n in chunks:
        s = _dot(kb_ref[o:o + n, :], q_t)
        m_c = jnp.max(s, axis=0, keepdims=True)
        m = m_c if m is None else jnp.maximum(m, m_c)
    for o, n in chunks:
        s = _dot(kb_ref[o:o + n, :], q_t)
        p = jnp.exp2((s - m) * c2)
        d_c = jnp.sum(p, axis=0, keepdims=True)
        a_c = _dot(vt_ref[:, o:o + n], p.astype(BF16))
        denom = d_c if denom is None else denom + d_c
        acc = a_c if acc is None else acc + a_c
    o_t = acc * (1.0 / denom)
    for g in range(ATT_GROUP):
        o_ref[:, g * LANES:(g + 1) * LANES] = o_t[:, g * tq:(g + 1) * tq].T.astype(o_ref.dtype)


def _attention(P, B, S, q_norm, k_norm, l, rope_tabs, cache, emit_kv, tq):
    nq = S // tq
    cache_len = cache[0].shape[3] if cache is not None else 0
    T = cache_len + S
    gw = ATT_GROUP * ATT_HEAD_DIM
    norm_spec = pl.BlockSpec((None, 1, LANES), lambda b, j, i: (l, 0, 0))
    in_specs = [
        pl.BlockSpec((tq, gw), lambda b, j, i: (b * nq + i, j)),
        pl.BlockSpec((S, LANES), lambda b, j, i: (b, COL_AK + j)),
        pl.BlockSpec((S, LANES), lambda b, j, i: (b, COL_AV + j)),
        norm_spec, norm_spec,
    ]
    args = [P, P, P, q_norm, k_norm]
    if rope_tabs is not None:
        cos, sin = rope_tabs
        in_specs += [
            pl.BlockSpec((tq, LANES), lambda b, j, i: (i, 0)),
            pl.BlockSpec((tq, LANES), lambda b, j, i: (i, 0)),
            pl.BlockSpec((S, LANES), lambda b, j, i: (0, 0)),
            pl.BlockSpec((S, LANES), lambda b, j, i: (0, 0)),
        ]
        args += [cos, sin, cos, sin]
    if cache is not None:
        spec = pl.BlockSpec((None, None, None, cache_len, LANES), lambda b, j, i: (b, l, j, 0, 0))
        in_specs += [spec, spec]
        args += list(cache)
    out_specs = [pl.BlockSpec((tq, gw), lambda b, j, i: (b * nq + i, j))]
    out_shape = [jax.ShapeDtypeStruct((B * S, ATT_WIDTH), BF16)]
    if emit_kv:
        kv_spec = pl.BlockSpec((S, LANES), lambda b, j, i: (b, j))
        out_specs += [kv_spec, kv_spec]
        out_shape += [jax.ShapeDtypeStruct((B * S, ATT_KV_HEADS * ATT_HEAD_DIM), F32)] * 2
    return pl.pallas_call(
        functools.partial(_attn_kernel, rope=rope_tabs is not None, cache_len=cache_len,
                          emit_kv=emit_kv, tq=tq),
        grid=(B, ATT_KV_HEADS, nq),
        in_specs=in_specs,
        out_specs=out_specs,
        out_shape=out_shape,
        scratch_shapes=[pltpu.VMEM((T, LANES), BF16), pltpu.VMEM((LANES, T), BF16)],
        compiler_params=_params("parallel", "parallel", "arbitrary"),
        name="attention",
    )(*args)


def _ret_kernel(*refs, l, rope, has_s0, emit_state, S):
    it = iter(refs)
    lg_ref, q_ref, k_ref, v_ref, g_ref = (next(it) for _ in range(5))
    if rope:
        cos_ref, sin_ref = next(it), next(it)
    if has_s0:
        s0_ref = next(it)
    o_ref = next(it)
    if emit_state:
        st_ref = next(it)
    qb_ref, kb_ref, qd_ref, kd_ref, work_ref, sb_ref = (next(it) for _ in range(6))

    h = pl.program_id(1)
    C = RET_CHUNK
    n = S // C
    lg_f = lg_ref[l, 0, h]
    lg_b = lg_ref[l, 1, h]
    q = q_ref[...].astype(F32)
    k = k_ref[...].astype(F32) * (RET_DK ** -0.5)
    if rope:
        q = _rope(q, cos_ref[...], sin_ref[...])
        k = _rope(k, cos_ref[...], sin_ref[...])

    t = lax.broadcasted_iota(jnp.int32, (C, LANES), 0).astype(F32)

    def scaled(x, expo):
        return (x.reshape(n, C, LANES) * jnp.exp(expo)[None]).reshape(S, LANES).astype(BF16)

    qb_ref[...] = q.astype(BF16)
    kb_ref[...] = k.astype(BF16)
    qd_ref[:, 0:RET_DK] = scaled(q, (t + 1.0) * lg_f)
    qd_ref[:, RET_DK:] = scaled(q, (C - t) * lg_b)
    kd_ref[:, 0:RET_DK] = scaled(k, (C - 1.0 - t) * lg_f)
    kd_ref[:, RET_DK:] = scaled(k, t * lg_b)

    ii = lax.broadcasted_iota(jnp.int32, (C, C), 0).astype(F32)
    jj = lax.broadcasted_iota(jnp.int32, (C, C), 1).astype(F32)
    dmat = (jnp.where(ii >= jj, jnp.exp(jnp.maximum(ii - jj, 0.0) * lg_f), 0.0)
            + jnp.where(jj >= ii, jnp.exp(jnp.maximum(jj - ii, 0.0) * lg_b), 0.0))
    decay_f = jnp.exp(jnp.full((RET_DK, RET_DV), C * lg_f, F32))
    decay_b = jnp.exp(jnp.full((RET_DK, RET_DV), C * lg_b, F32))
    if has_s0:
        state0 = (s0_ref[0], s0_ref[1])
    else:
        state0 = (jnp.zeros((RET_DK, RET_DV), F32), jnp.zeros((RET_DK, RET_DV), F32))

    def rows(c):
        return pl.ds(pl.multiple_of(c * C, C), C)

    def state_updates(c, carry):
        sl = rows(c)
        update = _dot_tn(kd_ref[sl, :], v_ref[sl, :])
        work_ref[0, sl, :] = update[0:RET_DK, :]
        work_ref[1, sl, :] = update[RET_DK:, :]
        return carry

    lax.fori_loop(0, n, state_updates, 0, unroll=LOOP_UNROLL)

    def scan(c, carry):
        state_f, state_b = carry
        cb = n - 1 - c
        sb_ref[c, 0:RET_DK, :] = state_f.astype(BF16)
        sb_ref[cb, RET_DK:, :] = state_b.astype(BF16)
        return (decay_f * state_f + work_ref[0, rows(c), :], decay_b * state_b + work_ref[1, rows(cb), :])

    state_f, state_b = lax.fori_loop(0, n, scan, state0, unroll=LOOP_UNROLL)
    if emit_state:
        st_ref[0] = state_f
        st_ref[1] = state_b

    U = min(SCAN_GROUP, n)

    def outputs(j, carry):
        chunks = [j * U + u for u in range(U)]
        scores = [(_dot_nt(qb_ref[rows(c), :], kb_ref[rows(c), :]) * dmat).astype(BF16) for c in chunks]
        cross = [_dot(qd_ref[rows(c), :], sb_ref[c]) for c in chunks]
        for c, sc, x in zip(chunks, scores, cross):
            work_ref[0, rows(c), :] = _dot(sc, v_ref[rows(c), :]) + x
        return carry

    lax.fori_loop(0, n // U, outputs, 0)

    y = work_ref[0]
    mu = jnp.mean(y, axis=-1, keepdims=True)
    yc = y - mu
    var = jnp.mean(yc * yc, axis=-1, keepdims=True)
    gate = _silu(g_ref[...].astype(F32))
    o_ref[...] = (yc * lax.rsqrt(var + LN_EPS) * gate).astype(o_ref.dtype)


def _retention(P, B, S, log_decay, l, rope_tabs, s0, emit_state):
    def col(c0):
        return pl.BlockSpec((S, LANES), lambda b, h: (b, c0 + h))

    in_specs = [pl.BlockSpec(memory_space=pltpu.SMEM), col(COL_RQ), col(COL_RK), col(COL_RV), col(COL_RG)]
    args = [log_decay, P, P, P, P]
    if rope_tabs is not None:
        tab = pl.BlockSpec((S, LANES), lambda b, h: (0, 0))
        in_specs += [tab, tab]
        args += list(rope_tabs)
    if s0 is not None:
        in_specs.append(pl.BlockSpec((None, None, 2, None, RET_DK, RET_DV), lambda b, h: (b, l, 0, h, 0, 0)))
        args.append(s0)
    out_specs = [pl.BlockSpec((S, LANES), lambda b, h: (b, h))]
    out_shape = [jax.ShapeDtypeStruct((B * S, RET_WIDTH), BF16)]
    if emit_state:
        out_specs.append(pl.BlockSpec((None, 2, None, RET_DK, RET_DV), lambda b, h: (b, 0, h, 0, 0)))
        out_shape.append(jax.ShapeDtypeStruct((B, 2, RET_HEADS, RET_DK, RET_DV), F32))
    return pl.pallas_call(
        functools.partial(_ret_kernel, l=l, rope=rope_tabs is not None, has_s0=s0 is not None,
                          emit_state=emit_state, S=S),
        grid=(B, RET_HEADS),
        in_specs=in_specs,
        out_specs=out_specs,
        out_shape=out_shape,
        scratch_shapes=[pltpu.VMEM((S, RET_DK), BF16), pltpu.VMEM((S, RET_DK), BF16),
                        pltpu.VMEM((S, 2 * RET_DK), BF16), pltpu.VMEM((S, 2 * RET_DK), BF16),
                        pltpu.VMEM((2, S, RET_DV), F32),
                        pltpu.VMEM((S // RET_CHUNK, 2 * RET_DK, RET_DV), BF16)],
        compiler_params=_params("parallel", "parallel"),
        name="retention",
    )(*args)


def _gla_kernel(*refs, has_s0, emit_state, S):
    it = iter(refs)
    q_ref, k_ref, v_ref, gr_ref, ga_ref, wa_ref, ba_ref, gn_ref = (next(it) for _ in range(8))
    if has_s0:
        s0_ref = next(it)
    o_ref = next(it)
    if emit_state:
        st_ref = next(it)
    la_ref, qt_ref, kcat_ref, dec_ref, work_ref, sb_ref = (next(it) for _ in range(6))
    C = GLA_CHUNK
    n = S // C
    U = min(SCAN_GROUP, n)
    ga = ga_ref[...]
    for d in range(2):
        pre = _dot(ga, wa_ref[d]) + ba_ref[d]
        la_ref[:, d * LANES:(d + 1) * LANES] = (
            (jnp.minimum(pre, 0.0) - jnp.log1p(jnp.exp(-jnp.abs(pre)))) * (1.0 / GLA_TAU))

    ii = lax.broadcasted_iota(jnp.int32, (C, C), 0)
    jj = lax.broadcasted_iota(jnp.int32, (C, C), 1)
    lower_b = jnp.where(jj <= ii, 1.0, 0.0).astype(BF16)
    mask_f = jnp.concatenate([jnp.where(jj <= ii, 1.0, 0.0)] * 2, axis=1)
    mask_b = jnp.concatenate([jnp.where(jj >= ii, 1.0, 0.0)] * 2, axis=1)
    lane = lax.broadcasted_iota(jnp.int32, (C, LANES), 1)
    head0_lanes = lane < GLA_DK
    lane_sq = lax.broadcasted_iota(jnp.int32, (LANES, LANES), 1)
    row_sq = lax.broadcasted_iota(jnp.int32, (LANES, LANES), 0)
    lane_wide = lax.broadcasted_iota(jnp.int32, (C, 2 * GLA_DV), 1)

    def rows(c):
        return pl.ds(pl.multiple_of(c * C, C), C)

    def update_rows(c):
        return pl.ds(pl.multiple_of(c * GLA_DV, GLA_DV), GLA_DV)

    def prepare(j, carry):
        chunks = [j * U + u for u in range(U)]
        cums = []
        for c in chunks:
            la = la_ref[rows(c), :]
            hi = la.astype(BF16)
            r1 = la - hi.astype(F32)
            mid = r1.astype(BF16)
            lo = (r1 - mid.astype(F32)).astype(BF16)
            cums.append((la, _dot(lower_b, hi) + _dot(lower_b, mid) + _dot(lower_b, lo)))
        scaled_keys = []
        for c, (la, cum) in zip(chunks, cums):
            sl = rows(c)
            qc = q_ref[sl, :].astype(F32) * (GLA_DK ** -0.5)
            kc = k_ref[sl, :].astype(F32)
            b_f = cum[:, 0:LANES]
            cum_b = cum[:, LANES:]
            b_b = cum_b[C - 1:C, :] - cum_b + la[:, LANES:]
            end_f = b_f[C - 1:C, :]
            end_b = b_b[0:1, :]
            qt_ref[sl, 0:LANES] = (qc * jnp.exp(b_f)).astype(BF16)
            qt_ref[sl, LANES:] = (qc * jnp.exp(b_b)).astype(BF16)
            for d, b in ((0, b_f), (1, b_b)):
                k_t = kc * jnp.exp(-b)
                kcat_ref[d, c, 0:C, :] = jnp.where(head0_lanes, k_t, 0.0).astype(BF16)
                kcat_ref[d, c, C:, :] = jnp.where(head0_lanes, 0.0, k_t).astype(BF16)
            dec_ref[0, c] = jnp.exp(end_f)
            dec_ref[1, c] = jnp.exp(end_b)
            scaled_keys.append(jnp.concatenate(
                [kc * jnp.exp(end_f - b_f), kc * jnp.exp(end_b - b_b)], axis=1).astype(BF16))
        for c, k_s in zip(chunks, scaled_keys):
            kv = _dot_tn(v_ref[rows(c), :], k_s)
            for d in range(2):
                cols = kv[:, d * LANES:(d + 1) * LANES]
                work_ref[d, update_rows(c), :] = jnp.where(lane_sq < GLA_DK, cols[0:GLA_DV, :], cols[GLA_DV:, :])
        return carry

    lax.fori_loop(0, n // U, prepare, 0)

    if has_s0:
        state0 = (s0_ref[0].T, s0_ref[1].T)
    else:
        state0 = (jnp.zeros((GLA_DV, LANES), F32), jnp.zeros((GLA_DV, LANES), F32))

    def scan(c, carry):
        state_f, state_b = carry
        cb = n - 1 - c
        for d, cc, state in ((0, c, state_f), (1, cb, state_b)):
            s2 = state.T
            r0 = d * LANES
            sb_ref[cc, r0:r0 + LANES, 0:GLA_DV] = jnp.where(row_sq < GLA_DK, s2, 0.0).astype(BF16)
            sb_ref[cc, r0:r0 + LANES, GLA_DV:] = jnp.where(row_sq < GLA_DK, 0.0, s2).astype(BF16)
        return (state_f * dec_ref[0, c] + work_ref[0, update_rows(c), :],
                state_b * dec_ref[1, cb] + work_ref[1, update_rows(cb), :])

    states = lax.fori_loop(0, n, scan, state0, unroll=LOOP_UNROLL)
    if emit_state:
        st_ref[0] = states[0].T
        st_ref[1] = states[1].T

    def outputs(j, carry):
        chunks = [j * U + u for u in range(U)]
        scores, cross = [], []
        for c in chunks:
            q_t = qt_ref[rows(c), :]
            att = (_dot_nt(q_t[:, 0:LANES], kcat_ref[0, c]) * mask_f
                   + _dot_nt(q_t[:, LANES:], kcat_ref[1, c]) * mask_b)
            scores.append(att.astype(BF16))
            cross.append(_dot(q_t, sb_ref[c]))
        for c, att, x in zip(chunks, scores, cross):
            v_c = v_ref[rows(c), :]
            zeros = jnp.zeros_like(v_c)
            v_bd = jnp.concatenate([jnp.where(lane_wide < GLA_DV, v_c, zeros),
                                    jnp.where(lane_wide < GLA_DV, zeros, v_c)], axis=0)
            y = _dot(att, v_bd) + x
            for i in range(2):
                work_ref[0, pl.ds(pl.multiple_of(i * S + c * C, C), C), :] = y[:, i * GLA_DV:(i + 1) * GLA_DV]
        return carry

    lax.fori_loop(0, n // U, outputs, 0)

    for i in range(2):
        y = _rms_norm(work_ref[0, i * S:(i + 1) * S, :], gn_ref[...])
        gate = _silu(gr_ref[:, i * GLA_DV:(i + 1) * GLA_DV].astype(F32))
        o_ref[:, i * GLA_DV:(i + 1) * GLA_DV] = (y * gate).astype(o_ref.dtype)


def _gla(P, B, S, wa, ba, gla_norm, l, s0, emit_state):
    pw = 2 * GLA_DV
    in_specs = [
        pl.BlockSpec((S, LANES), lambda b, p: (b, COL_GQ + p)),
        pl.BlockSpec((S, LANES), lambda b, p: (b, COL_GK + p)),
        pl.BlockSpec((S, pw), lambda b, p: (b, COL_GV // 2 + p)),
        pl.BlockSpec((S, pw), lambda b, p: (b, COL_GR // 2 + p)),
        pl.BlockSpec((S, LANES), lambda b, p: (b, COL_GA)),
        pl.BlockSpec((None, 2, LANES, LANES), lambda b, p: (l, 0, 0, p)),
        pl.BlockSpec((None, 2, 1, LANES), lambda b, p: (l, 0, 0, p)),
        pl.BlockSpec((None, 1, GLA_DV), lambda b, p: (l, 0, 0)),
    ]
    args = [P, P, P, P, P, wa, ba, gla_norm]
    if s0 is not None:
        in_specs.append(pl.BlockSpec((None, None, 2, None, LANES, GLA_DV), lambda b, p: (b, l, 0, p, 0, 0)))
        args.append(s0)
    out_specs = [pl.BlockSpec((S, pw), lambda b, p: (b, p))]
    out_shape = [jax.ShapeDtypeStruct((B * S, GLA_WIDTH), BF16)]
    if emit_state:
        out_specs.append(pl.BlockSpec((None, 2, None, LANES, GLA_DV), lambda b, p: (b, 0, p, 0, 0)))
        out_shape.append(jax.ShapeDtypeStruct((B, 2, GLA_PAIRS, LANES, GLA_DV), F32))
    return pl.pallas_call(
        functools.partial(_gla_kernel, has_s0=s0 is not None, emit_state=emit_state, S=S),
        grid=(B, GLA_PAIRS),
        in_specs=in_specs,
        out_specs=out_specs,
        out_shape=out_shape,
        scratch_shapes=[pltpu.VMEM((S, 2 * LANES), F32),
                        pltpu.VMEM((S, 2 * LANES), BF16),
                        pltpu.VMEM((2, S // GLA_CHUNK, 2 * GLA_CHUNK, LANES), BF16),
                        pltpu.VMEM((2, S // GLA_CHUNK, 1, LANES), F32),
                        pltpu.VMEM((2, 2 * S, GLA_DV), F32),
                        pltpu.VMEM((S // GLA_CHUNK, 2 * LANES, 2 * GLA_DV), BF16)],
        compiler_params=_params("parallel", "parallel"),
        name="gla",
    )(*args)


def _outproj_kernel(x_ref, mod_ref, a_ref, r_ref, gl_ref, wa_ref, wr_ref, wg_ref, g_ref, b_ref, o_ref):
    y = _dot(a_ref[...], wa_ref[...]) + _dot(r_ref[...], wr_ref[...]) + _dot(gl_ref[...], wg_ref[...])
    y = DEEPNORM_ALPHA * x_ref[...] + mod_ref[5:6, :] * y
    o_ref[...] = _layer_norm(y, g_ref[...], b_ref[...])


def _outproj(x, mod, who0, rows_per_mod, att, ret, gla, w, ln_g, ln_b, l):
    M, D = x.shape
    return pl.pallas_call(
        _outproj_kernel,
        grid=(M // ROW_TILE,),
        in_specs=[
            pl.BlockSpec((ROW_TILE, D), lambda m: (m, 0)),
            _mod_spec(l, who0, rows_per_mod, 1),
            pl.BlockSpec((ROW_TILE, ATT_WIDTH), lambda m: (m, 0)),
            pl.BlockSpec((ROW_TILE, RET_WIDTH), lambda m: (m, 0)),
            pl.BlockSpec((ROW_TILE, GLA_WIDTH), lambda m: (m, 0)),
            pl.BlockSpec((None, ATT_WIDTH, D), lambda m: (l, 0, 0)),
            pl.BlockSpec((None, RET_WIDTH, D), lambda m: (l, ATT_WIDTH // RET_WIDTH, 0)),
            pl.BlockSpec((None, GLA_WIDTH, D), lambda m: (l, (ATT_WIDTH + RET_WIDTH) // GLA_WIDTH, 0)),
            pl.BlockSpec((None, None, 1, D), lambda m: (l, 1, 0, 0)),
            pl.BlockSpec((None, None, 1, D), lambda m: (l, 1, 0, 0)),
        ],
        out_specs=pl.BlockSpec((ROW_TILE, D), lambda m: (m, 0)),
        out_shape=jax.ShapeDtypeStruct((M, D), F32),
        compiler_params=_params("parallel"),
        name="outproj",
    )(x, mod, att, ret, gla, w, w, w, ln_g, ln_b)


def _trunk_layer(x, B, S, l, who0, w, rope_tabs, ctx, attn_tq):
    is_context = ctx is None
    rows_per_mod = x.shape[0] if is_context else S
    mod = w["mod"]
    x = _ffn(x, mod, who0, rows_per_mod, w["ffn_w_in"], w["ffn_w_out"], w["ln_g"], w["ln_b"], l, 0)
    P = _inproj(x, mod, who0, rows_per_mod, w["mix_w_in"], l)
    cache = None if is_context else (ctx[0], ctx[1])
    s_ret0 = None if is_context else ctx[2]
    s_gla0 = None if is_context else ctx[3]
    att_out = _attention(P, B, S, w["q_norm"], w["k_norm"], l, rope_tabs, cache, is_context, attn_tq)
    ret_out = _retention(P, B, S, w["log_decay"], l, rope_tabs, s_ret0, is_context)
    gla_out = _gla(P, B, S, w["gla_wa"], w["gla_ba"], w["gla_norm"], l, s_gla0, is_context)
    x = _outproj(x, mod, who0, rows_per_mod, att_out[0], ret_out[0], gla_out[0],
                 w["mix_w_out"], w["ln_g"], w["ln_b"], l)
    x = _ffn(x, mod, who0, rows_per_mod, w["ffn_w_in"], w["ffn_w_out"], w["ln_g"], w["ln_b"], l, 1)
    new_ctx = (att_out[1], att_out[2], ret_out[1], gla_out[1]) if is_context else None
    return x, new_ctx


def _rope_tables(rows):
    row = jnp.repeat(jnp.arange(rows, dtype=F32), GRID_W)
    col = jnp.tile(jnp.arange(GRID_W, dtype=F32), rows)
    n_freq = ATT_HEAD_DIM // 4
    inv = ROPE_THETA ** (-jnp.arange(n_freq, dtype=F32) / n_freq)
    ang = jnp.concatenate([row[:, None] * inv, col[:, None] * inv], axis=-1)
    cos, sin = jnp.cos(ang), jnp.sin(ang)
    cos_full = jnp.repeat(cos, 2, axis=-1)
    sin_signed = jnp.stack([-sin, sin], axis=-1).reshape(ang.shape[0], ATT_HEAD_DIM)
    return cos_full, sin_signed


def kernel(x_prompt, x_sample, c, cache_attn_k, cache_attn_v, state_ret, state_gla, c_ctx,
           w_mod, b_mod, ln_g, ln_b, ffn_w_in, ffn_w_out, mix_w_in, mix_w_out,
           att_q_norm, att_k_norm, ret_log_decay, gla_w_a2, gla_b_a, gla_norm):
    B_ctx, S_ctx, D = x_prompt.shape
    B_lat, S_lat, _ = x_sample.shape

    gla_wa = jnp.zeros((DEPTH, 2, LANES, GLA_HEADS * GLA_DK), BF16)
    for d in range(2):
        gla_wa = gla_wa.at[:, d, d * GLA_GATE_RANK:(d + 1) * GLA_GATE_RANK, :].set(gla_w_a2[:, d].astype(BF16))

    cvec = jnp.concatenate([c_ctx[None, :], c, jnp.zeros((MOD_ROWS - 1 - B_lat, D), F32)], axis=0)
    mod = _modulation(cvec, w_mod, b_mod[:, None, :]).reshape(DEPTH, MOD_ROWS, N_MOD, D)

    w = dict(
        mod=mod,
        ffn_w_in=ffn_w_in.astype(BF16),
        ffn_w_out=ffn_w_out.astype(BF16),
        mix_w_in=jnp.pad(mix_w_in.astype(BF16), ((0, 0), (0, 0), (0, PROJ_COLS_PADDED - PROJ_COLS))),
        mix_w_out=mix_w_out.astype(BF16),
        ln_g=ln_g[:, :, None, :], ln_b=ln_b[:, :, None, :],
        q_norm=att_q_norm[:, None, :], k_norm=att_k_norm[:, None, :],
        log_decay=ret_log_decay, gla_wa=gla_wa, gla_ba=gla_b_a[:, :, None, :],
        gla_norm=gla_norm[:, None, :])

    h = x_prompt.reshape(B_ctx * S_ctx, D)
    ks_l, vs_l, sr_l, sg_l = [], [], [], []
    for l in range(DEPTH):
        h, (k_l, v_l, s_r, s_g) = _trunk_layer(h, B_ctx, S_ctx, l, 0, w, None, None, S_ctx)
        ks_l.append(k_l.reshape(B_ctx, S_ctx, ATT_KV_HEADS, ATT_HEAD_DIM))
        vs_l.append(v_l.reshape(B_ctx, S_ctx, ATT_KV_HEADS, ATT_HEAD_DIM))
        sr_l.append(s_r)
        sg_l.append(s_g.reshape(B_ctx, 2, GLA_HEADS, GLA_DK, GLA_DV))
    y_prompt = h.reshape(B_ctx, S_ctx, D)

    rope_tabs = _rope_tables(S_lat // GRID_W)
    ctx = (cache_attn_k.transpose(0, 1, 3, 2, 4), cache_attn_v.transpose(0, 1, 3, 2, 4), state_ret,
           state_gla.reshape(B_lat, DEPTH, 2, GLA_PAIRS, LANES, GLA_DV))
    g = x_sample.reshape(B_lat * S_lat, D)
    for l in range(DEPTH):
        g, _ = _trunk_layer(g, B_lat, S_lat, l, 1, w, rope_tabs, ctx, 256)
    y_sample = g.reshape(B_lat, S_lat, D)

    return (y_prompt, y_sample, jnp.stack(ks_l, axis=1), jnp.stack(vs_l, axis=1),
            jnp.stack(sr_l, axis=1), jnp.stack(sg_l, axis=1))
```

```python
import functools
import math

import jax
import jax.numpy as jnp
from jax import lax
from jax.experimental import pallas as pl
from jax.experimental.pallas import tpu as pltpu

F32 = jnp.float32
BF16 = jnp.bfloat16

D_MODEL = 2048
DEPTH = 2
GRID_W = 64
ATT_HEAD_DIM = 128
ATT_HEADS = 8
ATT_KV_HEADS = 2
ATT_GROUP = ATT_HEADS // ATT_KV_HEADS
ATT_WIDTH = ATT_HEADS * ATT_HEAD_DIM
ATT_KV_CHUNK = 512
ROPE_THETA = 10000.0
RET_DK = 128
RET_DV = 128
RET_HEADS = 4
RET_WIDTH = RET_HEADS * RET_DV
RET_CHUNK = 128
GLA_DK = 64
GLA_DV = 128
GLA_HEADS = 4
GLA_PAIRS = GLA_HEADS // 2
GLA_WIDTH = GLA_HEADS * GLA_DV
GLA_GATE_RANK = 16
GLA_TAU = 16.0
GLA_CHUNK = 64
D_FF = 5632
N_MOD = 9
MACARON_WEIGHT = 0.5
DEEPNORM_ALPHA = (2 * DEPTH) ** 0.25
LN_EPS = 1e-5
RMS_EPS = 1e-6

LANES = 128
PROJ_COLS = 5152
PROJ_COLS_PADDED = 5376
PROJ_TN = 768
COL_AQ, COL_AK, COL_AV = 0, 8, 10
COL_RQ, COL_RK, COL_RV, COL_RG = 12, 16, 20, 24
COL_GQ, COL_GK, COL_GV, COL_GR, COL_GA = 28, 30, 32, 36, 40

VMEM_LIMIT_BYTES = 56 * 1024 * 1024
ROW_TILE = 512
PROJ_ROW_TILE = 1024
FF_TILE = 512
FFN_ROW_TILE = 512
FFN_SPLIT = 2
MOD_TN = 1024
OUTPROJ_SPLIT = 4
LOOP_UNROLL = 8
SCAN_GROUP = 8
MOD_ROWS = 8


def _params(*sem):
    return pltpu.CompilerParams(dimension_semantics=sem, vmem_limit_bytes=VMEM_LIMIT_BYTES)


def _dot(a, b):
    return jnp.dot(a, b, preferred_element_type=F32)


def _dot_nt(a, b):
    return lax.dot_general(a, b, (((1,), (1,)), ((), ())), preferred_element_type=F32)


def _dot_tn(a, b):
    return lax.dot_general(a, b, (((0,), (0,)), ((), ())), preferred_element_type=F32)


def _silu(x):
    return x * jax.nn.sigmoid(x)


def _layer_norm(y, g, b):
    mu = jnp.mean(y, axis=-1, keepdims=True)
    d = y - mu
    var = jnp.mean(d * d, axis=-1, keepdims=True)
    return d * lax.rsqrt(var + LN_EPS) * g + b


def _rms_norm(x, g):
    return x * lax.rsqrt(jnp.mean(x * x, axis=-1, keepdims=True) + RMS_EPS) * g


def _rope(x, cos, sin_signed):
    lane = lax.broadcasted_iota(jnp.int32, x.shape, 1)
    partner = jnp.where((lane & 1) == 0, pltpu.roll(x, LANES - 1, 1), pltpu.roll(x, 1, 1))
    return x * cos + partner * sin_signed


def _mod_spec(l, who0, rows_per_mod, ndim_grid, row_tile=ROW_TILE):
    if ndim_grid == 1:
        return pl.BlockSpec((None, None, N_MOD, D_MODEL),
                            lambda m: (l, who0 + (m * row_tile) // rows_per_mod, 0, 0))
    return pl.BlockSpec((None, None, N_MOD, D_MODEL),
                        lambda m, n: (l, who0 + (m * row_tile) // rows_per_mod, 0, 0))


def _mod_kernel(c_ref, w_ref, b_ref, o_ref):
    a = _silu(c_ref[...]).astype(BF16)
    o_ref[...] = _dot(a, w_ref[...].astype(BF16)) + b_ref[...]


def _modulation(cvec, w_mod, b_mod):
    L, D, N = w_mod.shape
    return pl.pallas_call(
        _mod_kernel,
        grid=(L, N // MOD_TN),
        in_specs=[
            pl.BlockSpec((MOD_ROWS, D), lambda l, n: (0, 0)),
            pl.BlockSpec((None, D, MOD_TN), lambda l, n: (l, 0, n)),
            pl.BlockSpec((None, 1, MOD_TN), lambda l, n: (l, 0, n)),
        ],
        out_specs=pl.BlockSpec((None, MOD_ROWS, MOD_TN), lambda l, n: (l, 0, n)),
        out_shape=jax.ShapeDtypeStruct((L, MOD_ROWS, N), F32),
        compiler_params=_params("parallel", "parallel"),
        name="modulation",
    )(cvec, w_mod, b_mod)


def _ffn_kernel(x_ref, mod_ref, wg_ref, wu_ref, wo_ref, g_ref, b_ref, o_ref, u_ref, *, mod_base):
    f = pl.program_id(1)
    last = pl.num_programs(1) - 1

    def step(sl, first, final):
        if first:
            shift = mod_ref[mod_base:mod_base + 1, :]
            scale = mod_ref[mod_base + 1:mod_base + 2, :]
            u = (x_ref[sl, :] * (1.0 + scale) + shift).astype(BF16)
            u_ref[sl, :] = u
        else:
            u = u_ref[sl, :]
        gate = _dot(u, wg_ref[...])
        up = _dot(u, wu_ref[...])
        act = (_silu(gate) * up).astype(BF16)
        acc = _dot(act, wo_ref[...])
        if not first:
            acc = o_ref[sl, :] + acc
        if final:
            g3 = mod_ref[mod_base + 2:mod_base + 3, :]
            y = DEEPNORM_ALPHA * x_ref[sl, :] + MACARON_WEIGHT * (g3 * acc)
            acc = _layer_norm(y, g_ref[...], b_ref[...])
        o_ref[sl, :] = acc

    rows = x_ref.shape[0] // FFN_SPLIT
    subtiles = [slice(i * rows, (i + 1) * rows) for i in range(FFN_SPLIT)]

    @pl.when(f == 0)
    def _():
        for sl in subtiles:
            step(sl, True, False)

    @pl.when(jnp.logical_and(f > 0, f < last))
    def _():
        step(slice(None), False, False)

    @pl.when(f == last)
    def _():
        for sl in subtiles:
            step(sl, False, True)


def _ffn(x, mod, who0, rows_per_mod, w_in, w_out, ln_g, ln_b, l, half):
    M, D = x.shape
    nf = D_FF // FF_TILE
    sub = 2 * half
    return pl.pallas_call(
        functools.partial(_ffn_kernel, mod_base=3 * sub),
        grid=(M // FFN_ROW_TILE, nf),
        in_specs=[
            pl.BlockSpec((FFN_ROW_TILE, D), lambda m, f: (m, 0)),
            _mod_spec(l, who0, rows_per_mod, 2, FFN_ROW_TILE),
            pl.BlockSpec((None, None, D, FF_TILE), lambda m, f: (l, half, 0, f)),
            pl.BlockSpec((None, None, D, FF_TILE), lambda m, f: (l, half, 0, f + nf)),
            pl.BlockSpec((None, None, FF_TILE, D), lambda m, f: (l, half, f, 0)),
            pl.BlockSpec((None, None, 1, D), lambda m, f: (l, sub, 0, 0)),
            pl.BlockSpec((None, None, 1, D), lambda m, f: (l, sub, 0, 0)),
        ],
        out_specs=pl.BlockSpec((FFN_ROW_TILE, D), lambda m, f: (m, 0)),
        out_shape=jax.ShapeDtypeStruct((M, D), F32),
        scratch_shapes=[pltpu.VMEM((FFN_ROW_TILE, D), BF16)],
        compiler_params=_params("parallel", "arbitrary"),
        name="ffn",
    )(x, mod, w_in, w_in, w_out, ln_g, ln_b)


def _inproj_kernel(x_ref, mod_ref, w_ref, o_ref, u_ref):
    @pl.when(pl.program_id(1) == 0)
    def _():
        shift = mod_ref[3:4, :]
        scale = mod_ref[4:5, :]
        u_ref[...] = (x_ref[...] * (1.0 + scale) + shift).astype(BF16)

    o_ref[...] = _dot(u_ref[...], w_ref[...]).astype(o_ref.dtype)


def _inproj(x, mod, who0, rows_per_mod, w, l):
    M, D = x.shape
    N = w.shape[2]
    return pl.pallas_call(
        _inproj_kernel,
        grid=(M // PROJ_ROW_TILE, N // PROJ_TN),
        in_specs=[
            pl.BlockSpec((PROJ_ROW_TILE, D), lambda m, n: (m, 0)),
            _mod_spec(l, who0, rows_per_mod, 2, PROJ_ROW_TILE),
            pl.BlockSpec((None, D, PROJ_TN), lambda m, n: (l, 0, n)),
        ],
        out_specs=pl.BlockSpec((PROJ_ROW_TILE, PROJ_TN), lambda m, n: (m, n)),
        out_shape=jax.ShapeDtypeStruct((M, N), BF16),
        scratch_shapes=[pltpu.VMEM((PROJ_ROW_TILE, D), BF16)],
        compiler_params=_params("parallel", "arbitrary"),
        name="inproj",
    )(x, mod, w)


def _attn_kernel(*refs, rope, cache_len, emit_kv, tq):
    it = iter(refs)
    q_ref, k_ref, v_ref, qn_ref, kn_ref = (next(it) for _ in range(5))
    if rope:
        cosq_ref, sinq_ref, cosk_ref, sink_ref = (next(it) for _ in range(4))
    if cache_len:
        ck_ref, cv_ref = next(it), next(it)
    o_ref = next(it)
    if emit_kv:
        newk_ref, newv_ref = next(it), next(it)
    kb_ref, vt_ref = next(it), next(it)
    T = kb_ref.shape[0]
    chunks = ([(0, cache_len)] if cache_len else []) + [
        (o, min(ATT_KV_CHUNK, T - o)) for o in range(cache_len, T, ATT_KV_CHUNK)]

    @pl.when(pl.program_id(2) == 0)
    def _():
        k = _rms_norm(k_ref[...].astype(F32), kn_ref[...])
        v = v_ref[...].astype(F32)
        if emit_kv:
            newk_ref[...] = k
            newv_ref[...] = v
        if rope:
            k = _rope(k, cosk_ref[...], sink_ref[...])
        if cache_len:
            k = jnp.concatenate([ck_ref[...], k], axis=0)
            v = jnp.concatenate([cv_ref[...], v], axis=0)
        kb_ref[...] = k.astype(BF16)
        for o, n in chunks:
            vt_ref[:, o:o + n] = v[o:o + n, :].T.astype(BF16)

    q = q_ref[...].astype(F32)
    heads = []
    for g in range(ATT_GROUP):
        qg = _rms_norm(q[:, g * LANES:(g + 1) * LANES], qn_ref[...])
        if rope:
            qg = _rope(qg, cosq_ref[...], sinq_ref[...])
        heads.append(qg.T)
    q_t = jnp.concatenate(heads, axis=1).astype(BF16)

    c2 = (ATT_HEAD_DIM ** -0.5) * math.log2(math.e)
    m = denom = acc = None
    for o, n in chunks:
        s = _dot(kb_ref[o:o + n, :], q_t)
        m_c = jnp.max(s, axis=0, keepdims=True)
        m = m_c if m is None else jnp.maximum(m, m_c)
    for o, n in chunks:
        s = _dot(kb_ref[o:o + n, :], q_t)
        p = jnp.exp2((s - m) * c2)
        d_c = jnp.sum(p, axis=0, keepdims=True)
        a_c = _dot(vt_ref[:, o:o + n], p.astype(BF16))
        denom = d_c if denom is None else denom + d_c
        acc = a_c if acc is None else acc + a_c
    o_t = acc * (1.0 / denom)
    for g in range(ATT_GROUP):
        o_ref[:, g * LANES:(g + 1) * LANES] = o_t[:, g * tq:(g + 1) * tq].T.astype(o_ref.dtype)


def _attention(P, B, S, q_norm, k_norm, l, rope_tabs, cache, emit_kv, tq):
    nq = S // tq
    cache_len = cache[0].shape[3] if cache is not None else 0
    T = cache_len + S
    gw = ATT_GROUP * ATT_HEAD_DIM
    norm_spec = pl.BlockSpec((None, 1, LANES), lambda b, j, i: (l, 0, 0))
    in_specs = [
        pl.BlockSpec((tq, gw), lambda b, j, i: (b * nq + i, j)),
        pl.BlockSpec((S, LANES), lambda b, j, i: (b, COL_AK + j)),
        pl.BlockSpec((S, LANES), lambda b, j, i: (b, COL_AV + j)),
        norm_spec, norm_spec,
    ]
    args = [P, P, P, q_norm, k_norm]
    if rope_tabs is not None:
        cos, sin = rope_tabs
        in_specs += [
            pl.BlockSpec((tq, LANES), lambda b, j, i: (i, 0)),
            pl.BlockSpec((tq, LANES), lambda b, j, i: (i, 0)),
            pl.BlockSpec((S, LANES), lambda b, j, i: (0, 0)),
            pl.BlockSpec((S, LANES), lambda b, j, i: (0, 0)),
        ]
        args += [cos, sin, cos, sin]
    if cache is not None:
        spec = pl.BlockSpec((None, None, None, cache_len, LANES), lambda b, j, i: (b, l, j, 0, 0))
        in_specs += [spec, spec]
        args += list(cache)
    out_specs = [pl.BlockSpec((tq, gw), lambda b, j, i: (b * nq + i, j))]
    out_shape = [jax.ShapeDtypeStruct((B * S, ATT_WIDTH), BF16)]
    if emit_kv:
        kv_spec = pl.BlockSpec((S, LANES), lambda b, j, i: (b, j))
        out_specs += [kv_spec, kv_spec]
        out_shape += [jax.ShapeDtypeStruct((B * S, ATT_KV_HEADS * ATT_HEAD_DIM), F32)] * 2
    return pl.pallas_call(
        functools.partial(_attn_kernel, rope=rope_tabs is not None, cache_len=cache_len,
                          emit_kv=emit_kv, tq=tq),
        grid=(B, ATT_KV_HEADS, nq),
        in_specs=in_specs,
        out_specs=out_specs,
        out_shape=out_shape,
        scratch_shapes=[pltpu.VMEM((T, LANES), BF16), pltpu.VMEM((LANES, T), BF16)],
        compiler_params=_params("parallel", "parallel", "arbitrary"),
        name="attention",
    )(*args)


def _ret_kernel(*refs, l, rope, has_s0, emit_state, S):
    it = iter(refs)
    lg_ref, q_ref, k_ref, v_ref, g_ref = (next(it) for _ in range(5))
    if rope:
        cos_ref, sin_ref = next(it), next(it)
    if has_s0:
        s0_ref = next(it)
    o_ref = next(it)
    if emit_state:
        st_ref = next(it)
    qb_ref, kb_ref, qd_ref, kd_ref, work_ref, sb_ref = (next(it) for _ in range(6))

    h = pl.program_id(1)
    C = RET_CHUNK
    n = S // C
    lg_f = lg_ref[l, 0, h]
    lg_b = lg_ref[l, 1, h]
    q = q_ref[...].astype(F32)
    k = k_ref[...].astype(F32) * (RET_DK ** -0.5)
    if rope:
        q = _rope(q, cos_ref[...], sin_ref[...])
        k = _rope(k, cos_ref[...], sin_ref[...])

    t = lax.broadcasted_iota(jnp.int32, (C, LANES), 0).astype(F32)

    def scaled(x, expo):
        return (x.reshape(n, C, LANES) * jnp.exp(expo)[None]).reshape(S, LANES).astype(BF16)

    qb_ref[...] = q.astype(BF16)
    kb_ref[...] = k.astype(BF16)
    qd_ref[:, 0:RET_DK] = scaled(q, (t + 1.0) * lg_f)
    qd_ref[:, RET_DK:] = scaled(q, (C - t) * lg_b)
    kd_ref[:, 0:RET_DK] = scaled(k, (C - 1.0 - t) * lg_f)
    kd_ref[:, RET_DK:] = scaled(k, t * lg_b)

    ii = lax.broadcasted_iota(jnp.int32, (C, C), 0).astype(F32)
    jj = lax.broadcasted_iota(jnp.int32, (C, C), 1).astype(F32)
    dmat = (jnp.where(ii >= jj, jnp.exp(jnp.maximum(ii - jj, 0.0) * lg_f), 0.0)
            + jnp.where(jj >= ii, jnp.exp(jnp.maximum(jj - ii, 0.0) * lg_b), 0.0))
    decay_f = jnp.exp(jnp.full((RET_DK, RET_DV), C * lg_f, F32))
    decay_b = jnp.exp(jnp.full((RET_DK, RET_DV), C * lg_b, F32))
    if has_s0:
        state0 = (s0_ref[0], s0_ref[1])
    else:
        state0 = (jnp.zeros((RET_DK, RET_DV), F32), jnp.zeros((RET_DK, RET_DV), F32))

    def rows(c):
        return pl.ds(pl.multiple_of(c * C, C), C)

    def state_updates(c, carry):
        sl = rows(c)
        update = _dot_tn(kd_ref[sl, :], v_ref[sl, :])
        work_ref[0, sl, :] = update[0:RET_DK, :]
        work_ref[1, sl, :] = update[RET_DK:, :]
        return carry

    lax.fori_loop(0, n, state_updates, 0, unroll=LOOP_UNROLL)

    def scan(c, carry):
        state_f, state_b = carry
        cb = n - 1 - c
        sb_ref[c, 0:RET_DK, :] = state_f.astype(BF16)
        sb_ref[cb, RET_DK:, :] = state_b.astype(BF16)
        return (decay_f * state_f + work_ref[0, rows(c), :], decay_b * state_b + work_ref[1, rows(cb), :])

    state_f, state_b = lax.fori_loop(0, n, scan, state0, unroll=LOOP_UNROLL)
    if emit_state:
        st_ref[0] = state_f
        st_ref[1] = state_b

    U = min(SCAN_GROUP, n)

    def outputs(j, carry):
        chunks = [j * U + u for u in range(U)]
        scores = [(_dot_nt(qb_ref[rows(c), :], kb_ref[rows(c), :]) * dmat).astype(BF16) for c in chunks]
        cross = [_dot(qd_ref[rows(c), :], sb_ref[c]) for c in chunks]
        for c, sc, x in zip(chunks, scores, cross):
            work_ref[0, rows(c), :] = _dot(sc, v_ref[rows(c), :]) + x
        return carry

    lax.fori_loop(0, n // U, outputs, 0)

    y = work_ref[0]
    mu = jnp.mean(y, axis=-1, keepdims=True)
    yc = y - mu
    var = jnp.mean(yc * yc, axis=-1, keepdims=True)
    gate = _silu(g_ref[...].astype(F32))
    o_ref[...] = (yc * lax.rsqrt(var + LN_EPS) * gate).astype(o_ref.dtype)


def _retention(P, B, S, log_decay, l, rope_tabs, s0, emit_state):
    def col(c0):
        return pl.BlockSpec((S, LANES), lambda b, h: (b, c0 + h))

    in_specs = [pl.BlockSpec(memory_space=pltpu.SMEM), col(COL_RQ), col(COL_RK), col(COL_RV), col(COL_RG)]
    args = [log_decay, P, P, P, P]
    if rope_tabs is not None:
        tab = pl.BlockSpec((S, LANES), lambda b, h: (0, 0))
        in_specs += [tab, tab]
        args += list(rope_tabs)
    if s0 is not None:
        in_specs.append(pl.BlockSpec((None, None, 2, None, RET_DK, RET_DV), lambda b, h: (b, l, 0, h, 0, 0)))
        args.append(s0)
    out_specs = [pl.BlockSpec((S, LANES), lambda b, h: (b, h))]
    out_shape = [jax.ShapeDtypeStruct((B * S, RET_WIDTH), BF16)]
    if emit_state:
        out_specs.append(pl.BlockSpec((None, 2, None, RET_DK, RET_DV), lambda b, h: (b, 0, h, 0, 0)))
        out_shape.append(jax.ShapeDtypeStruct((B, 2, RET_HEADS, RET_DK, RET_DV), F32))
    return pl.pallas_call(
        functools.partial(_ret_kernel, l=l, rope=rope_tabs is not None, has_s0=s0 is not None,
                          emit_state=emit_state, S=S),
        grid=(B, RET_HEADS),
        in_specs=in_specs,
        out_specs=out_specs,
        out_shape=out_shape,
        scratch_shapes=[pltpu.VMEM((S, RET_DK), BF16), pltpu.VMEM((S, RET_DK), BF16),
                        pltpu.VMEM((S, 2 * RET_DK), BF16), pltpu.VMEM((S, 2 * RET_DK), BF16),
                        pltpu.VMEM((2, S, RET_DV), F32),
                        pltpu.VMEM((S // RET_CHUNK, 2 * RET_DK, RET_DV), BF16)],
        compiler_params=_params("parallel", "parallel"),
        name="retention",
    )(*args)


def _gla_kernel(*refs, has_s0, emit_state, S):
    it = iter(refs)
    q_ref, k_ref, v_ref, gr_ref, ga_ref, wa_ref, ba_ref, gn_ref = (next(it) for _ in range(8))
    if has_s0:
        s0_ref = next(it)
    o_ref = next(it)
    if emit_state:
        st_ref = next(it)
    la_ref, qt_ref, kcat_ref, dec_ref, work_ref, sb_ref = (next(it) for _ in range(6))
    C = GLA_CHUNK
    n = S // C
    U = min(SCAN_GROUP, n)
    ga = ga_ref[...]
    for d in range(2):
        pre = _dot(ga, wa_ref[d]) + ba_ref[d]
        la_ref[:, d * LANES:(d + 1) * LANES] = (
            (jnp.minimum(pre, 0.0) - jnp.log1p(jnp.exp(-jnp.abs(pre)))) * (1.0 / GLA_TAU))

    ii = lax.broadcasted_iota(jnp.int32, (C, C), 0)
    jj = lax.broadcasted_iota(jnp.int32, (C, C), 1)
    lower_b = jnp.where(jj <= ii, 1.0, 0.0).astype(BF16)
    mask_f = jnp.concatenate([jnp.where(jj <= ii, 1.0, 0.0)] * 2, axis=1)
    mask_b = jnp.concatenate([jnp.where(jj >= ii, 1.0, 0.0)] * 2, axis=1)
    lane = lax.broadcasted_iota(jnp.int32, (C, LANES), 1)
    head0_lanes = lane < GLA_DK
    lane_sq = lax.broadcasted_iota(jnp.int32, (LANES, LANES), 1)
    row_sq = lax.broadcasted_iota(jnp.int32, (LANES, LANES), 0)
    lane_wide = lax.broadcasted_iota(jnp.int32, (C, 2 * GLA_DV), 1)

    def rows(c):
        return pl.ds(pl.multiple_of(c * C, C), C)

    def update_rows(c):
        return pl.ds(pl.multiple_of(c * GLA_DV, GLA_DV), GLA_DV)

    def prepare(j, carry):
        chunks = [j * U + u for u in range(U)]
        cums = []
        for c in chunks:
            la = la_ref[rows(c), :]
            hi = la.astype(BF16)
            r1 = la - hi.astype(F32)
            mid = r1.astype(BF16)
            lo = (r1 - mid.astype(F32)).astype(BF16)
            cums.append((la, _dot(lower_b, hi) + _dot(lower_b, mid) + _dot(lower_b, lo)))
        scaled_keys = []
        for c, (la, cum) in zip(chunks, cums):
            sl = rows(c)
            qc = q_ref[sl, :].astype(F32) * (GLA_DK ** -0.5)
            kc = k_ref[sl, :].astype(F32)
            b_f = cum[:, 0:LANES]
            cum_b = cum[:, LANES:]
            b_b = cum_b[C - 1:C, :] - cum_b + la[:, LANES:]
            end_f = b_f[C - 1:C, :]
            end_b = b_b[0:1, :]
            qt_ref[sl, 0:LANES] = (qc * jnp.exp(b_f)).astype(BF16)
            qt_ref[sl, LANES:] = (qc * jnp.exp(b_b)).astype(BF16)
            for d, b in ((0, b_f), (1, b_b)):
                k_t = kc * jnp.exp(-b)
                kcat_ref[d, c, 0:C, :] = jnp.where(head0_lanes, k_t, 0.0).astype(BF16)
                kcat_ref[d, c, C:, :] = jnp.where(head0_lanes, 0.0, k_t).astype(BF16)
            dec_ref[0, c] = jnp.exp(end_f)
            dec_ref[1, c] = jnp.exp(end_b)
            scaled_keys.append(jnp.concatenate(
                [kc * jnp.exp(end_f - b_f), kc * jnp.exp(end_b - b_b)], axis=1).astype(BF16))
        for c, k_s in zip(chunks, scaled_keys):
            kv = _dot_tn(v_ref[rows(c), :], k_s)
            for d in range(2):
                cols = kv[:, d * LANES:(d + 1) * LANES]
                work_ref[d, update_rows(c), :] = jnp.where(lane_sq < GLA_DK, cols[0:GLA_DV, :], cols[GLA_DV:, :])
        return carry

    lax.fori_loop(0, n // U, prepare, 0)

    if has_s0:
        state0 = (s0_ref[0].T, s0_ref[1].T)
    else:
        state0 = (jnp.zeros((GLA_DV, LANES), F32), jnp.zeros((GLA_DV, LANES), F32))

    def scan(c, carry):
        state_f, state_b = carry
        cb = n - 1 - c
        for d, cc, state in ((0, c, state_f), (1, cb, state_b)):
            s2 = state.T
            r0 = d * LANES
            sb_ref[cc, r0:r0 + LANES, 0:GLA_DV] = jnp.where(row_sq < GLA_DK, s2, 0.0).astype(BF16)
            sb_ref[cc, r0:r0 + LANES, GLA_DV:] = jnp.where(row_sq < GLA_DK, 0.0, s2).astype(BF16)
        return (state_f * dec_ref[0, c] + work_ref[0, update_rows(c), :],
                state_b * dec_ref[1, cb] + work_ref[1, update_rows(cb), :])

    states = lax.fori_loop(0, n, scan, state0, unroll=LOOP_UNROLL)
    if emit_state:
        st_ref[0] = states[0].T
        st_ref[1] = states[1].T

    def outputs(j, carry):
        chunks = [j * U + u for u in range(U)]
        scores, cross = [], []
        for c in chunks:
            q_t = qt_ref[rows(c), :]
            att = (_dot_nt(q_t[:, 0:LANES], kcat_ref[0, c]) * mask_f
                   + _dot_nt(q_t[:, LANES:], kcat_ref[1, c]) * mask_b)
            scores.append(att.astype(BF16))
            cross.append(_dot(q_t, sb_ref[c]))
        for c, att, x in zip(chunks, scores, cross):
            v_c = v_ref[rows(c), :]
            zeros = jnp.zeros_like(v_c)
            v_bd = jnp.concatenate([jnp.where(lane_wide < GLA_DV, v_c, zeros),
                                    jnp.where(lane_wide < GLA_DV, zeros, v_c)], axis=0)
            y = _dot(att, v_bd) + x
            for i in range(2):
                work_ref[0, pl.ds(pl.multiple_of(i * S + c * C, C), C), :] = y[:, i * GLA_DV:(i + 1) * GLA_DV]
        return carry

    lax.fori_loop(0, n // U, outputs, 0)

    for i in range(2):
        y = _rms_norm(work_ref[0, i * S:(i + 1) * S, :], gn_ref[...])
        gate = _silu(gr_ref[:, i * GLA_DV:(i + 1) * GLA_DV].astype(F32))
        o_ref[:, i * GLA_DV:(i + 1) * GLA_DV] = (y * gate).astype(o_ref.dtype)


def _gla(P, B, S, wa, ba, gla_norm, l, s0, emit_state):
    pw = 2 * GLA_DV
    in_specs = [
        pl.BlockSpec((S, LANES), lambda b, p: (b, COL_GQ + p)),
        pl.BlockSpec((S, LANES), lambda b, p: (b, COL_GK + p)),
        pl.BlockSpec((S, pw), lambda b, p: (b, COL_GV // 2 + p)),
        pl.BlockSpec((S, pw), lambda b, p: (b, COL_GR // 2 + p)),
        pl.BlockSpec((S, LANES), lambda b, p: (b, COL_GA)),
        pl.BlockSpec((None, 2, LANES, LANES), lambda b, p: (l, 0, 0, p)),
        pl.BlockSpec((None, 2, 1, LANES), lambda b, p: (l, 0, 0, p)),
        pl.BlockSpec((None, 1, GLA_DV), lambda b, p: (l, 0, 0)),
    ]
    args = [P, P, P, P, P, wa, ba, gla_norm]
    if s0 is not None:
        in_specs.append(pl.BlockSpec((None, None, 2, None, LANES, GLA_DV), lambda b, p: (b, l, 0, p, 0, 0)))
        args.append(s0)
    out_specs = [pl.BlockSpec((S, pw), lambda b, p: (b, p))]
    out_shape = [jax.ShapeDtypeStruct((B * S, GLA_WIDTH), BF16)]
    if emit_state:
        out_specs.append(pl.BlockSpec((None, 2, None, LANES, GLA_DV), lambda b, p: (b, 0, p, 0, 0)))
        out_shape.append(jax.ShapeDtypeStruct((B, 2, GLA_PAIRS, LANES, GLA_DV), F32))
    return pl.pallas_call(
        functools.partial(_gla_kernel, has_s0=s0 is not None, emit_state=emit_state, S=S),
        grid=(B, GLA_PAIRS),
        in_specs=in_specs,
        out_specs=out_specs,
        out_shape=out_shape,
        scratch_shapes=[pltpu.VMEM((S, 2 * LANES), F32),
                        pltpu.VMEM((S, 2 * LANES), BF16),
                        pltpu.VMEM((2, S // GLA_CHUNK, 2 * GLA_CHUNK, LANES), BF16),
                        pltpu.VMEM((2, S // GLA_CHUNK, 1, LANES), F32),
                        pltpu.VMEM((2, 2 * S, GLA_DV), F32),
                        pltpu.VMEM((S // GLA_CHUNK, 2 * LANES, 2 * GLA_DV), BF16)],
        compiler_params=_params("parallel", "parallel"),
        name="gla",
    )(*args)


def _outproj_kernel(x_ref, mod_ref, a_ref, r_ref, gl_ref, wa_ref, wr_ref, wg_ref, g_ref, b_ref, o_ref):
    rows = x_ref.shape[0] // OUTPROJ_SPLIT
    mixed = []
    for i in range(OUTPROJ_SPLIT):
        sl = slice(i * rows, (i + 1) * rows)
        mixed.append(_dot(a_ref[sl, :], wa_ref[...]) + _dot(r_ref[sl, :], wr_ref[...])
                     + _dot(gl_ref[sl, :], wg_ref[...]))
    for i in range(OUTPROJ_SPLIT):
        sl = slice(i * rows, (i + 1) * rows)
        y = DEEPNORM_ALPHA * x_ref[sl, :] + mod_ref[5:6, :] * mixed[i]
        o_ref[sl, :] = _layer_norm(y, g_ref[...], b_ref[...])


def _outproj(x, mod, who0, rows_per_mod, att, ret, gla, w, ln_g, ln_b, l):
    M, D = x.shape
    return pl.pallas_call(
        _outproj_kernel,
        grid=(M // ROW_TILE,),
        in_specs=[
            pl.BlockSpec((ROW_TILE, D), lambda m: (m, 0)),
            _mod_spec(l, who0, rows_per_mod, 1),
            pl.BlockSpec((ROW_TILE, ATT_WIDTH), lambda m: (m, 0)),
            pl.BlockSpec((ROW_TILE, RET_WIDTH), lambda m: (m, 0)),
            pl.BlockSpec((ROW_TILE, GLA_WIDTH), lambda m: (m, 0)),
            pl.BlockSpec((None, ATT_WIDTH, D), lambda m: (l, 0, 0)),
            pl.BlockSpec((None, RET_WIDTH, D), lambda m: (l, ATT_WIDTH // RET_WIDTH, 0)),
            pl.BlockSpec((None, GLA_WIDTH, D), lambda m: (l, (ATT_WIDTH + RET_WIDTH) // GLA_WIDTH, 0)),
            pl.BlockSpec((None, None, 1, D), lambda m: (l, 1, 0, 0)),
            pl.BlockSpec((None, None, 1, D), lambda m: (l, 1, 0, 0)),
        ],
        out_specs=pl.BlockSpec((ROW_TILE, D), lambda m: (m, 0)),
        out_shape=jax.ShapeDtypeStruct((M, D), F32),
        compiler_params=_params("parallel"),
        name="outproj",
    )(x, mod, att, ret, gla, w, w, w, ln_g, ln_b)


def _trunk_layer(x, B, S, l, who0, w, rope_tabs, ctx, attn_tq):
    is_context = ctx is None
    rows_per_mod = x.shape[0] if is_context else S
    mod = w["mod"]
    x = _ffn(x, mod, who0, rows_per_mod, w["ffn_w_in"], w["ffn_w_out"], w["ln_g"], w["ln_b"], l, 0)
    P = _inproj(x, mod, who0, rows_per_mod, w["mix_w_in"], l)
    cache = None if is_context else (ctx[0], ctx[1])
    s_ret0 = None if is_context else ctx[2]
    s_gla0 = None if is_context else ctx[3]
    att_out = _attention(P, B, S, w["q_norm"], w["k_norm"], l, rope_tabs, cache, is_context, attn_tq)
    ret_out = _retention(P, B, S, w["log_decay"], l, rope_tabs, s_ret0, is_context)
    gla_out = _gla(P, B, S, w["gla_wa"], w["gla_ba"], w["gla_norm"], l, s_gla0, is_context)
    x = _outproj(x, mod, who0, rows_per_mod, att_out[0], ret_out[0], gla_out[0],
                 w["mix_w_out"], w["ln_g"], w["ln_b"], l)
    x = _ffn(x, mod, who0, rows_per_mod, w["ffn_w_in"], w["ffn_w_out"], w["ln_g"], w["ln_b"], l, 1)
    new_ctx = (att_out[1], att_out[2], ret_out[1], gla_out[1]) if is_context else None
    return x, new_ctx


def _rope_tables(rows):
    row = jnp.repeat(jnp.arange(rows, dtype=F32), GRID_W)
    col = jnp.tile(jnp.arange(GRID_W, dtype=F32), rows)
    n_freq = ATT_HEAD_DIM // 4
    inv = ROPE_THETA ** (-jnp.arange(n_freq, dtype=F32) / n_freq)
    ang = jnp.concatenate([row[:, None] * inv, col[:, None] * inv], axis=-1)
    cos, sin = jnp.cos(ang), jnp.sin(ang)
    cos_full = jnp.repeat(cos, 2, axis=-1)
    sin_signed = jnp.stack([-sin, sin], axis=-1).reshape(ang.shape[0], ATT_HEAD_DIM)
    return cos_full, sin_signed


def kernel(x_prompt, x_sample, c, cache_attn_k, cache_attn_v, state_ret, state_gla, c_ctx,
           w_mod, b_mod, ln_g, ln_b, ffn_w_in, ffn_w_out, mix_w_in, mix_w_out,
           att_q_norm, att_k_norm, ret_log_decay, gla_w_a2, gla_b_a, gla_norm):
    B_ctx, S_ctx, D = x_prompt.shape
    B_lat, S_lat, _ = x_sample.shape

    gla_wa = jnp.zeros((DEPTH, 2, LANES, GLA_HEADS * GLA_DK), BF16)
    for d in range(2):
        gla_wa = gla_wa.at[:, d, d * GLA_GATE_RANK:(d + 1) * GLA_GATE_RANK, :].set(gla_w_a2[:, d].astype(BF16))

    cvec = jnp.concatenate([c_ctx[None, :], c, jnp.zeros((MOD_ROWS - 1 - B_lat, D), F32)], axis=0)
    mod = _modulation(cvec, w_mod, b_mod[:, None, :]).reshape(DEPTH, MOD_ROWS, N_MOD, D)

    w = dict(
        mod=mod,
        ffn_w_in=ffn_w_in.astype(BF16),
        ffn_w_out=ffn_w_out.astype(BF16),
        mix_w_in=jnp.pad(mix_w_in.astype(BF16), ((0, 0), (0, 0), (0, PROJ_COLS_PADDED - PROJ_COLS))),
        mix_w_out=mix_w_out.astype(BF16),
        ln_g=ln_g[:, :, None, :], ln_b=ln_b[:, :, None, :],
        q_norm=att_q_norm[:, None, :], k_norm=att_k_norm[:, None, :],
        log_decay=ret_log_decay, gla_wa=gla_wa, gla_ba=gla_b_a[:, :, None, :],
        gla_norm=gla_norm[:, None, :])

    h = x_prompt.reshape(B_ctx * S_ctx, D)
    ks_l, vs_l, sr_l, sg_l = [], [], [], []
    for l in range(DEPTH):
        h, (k_l, v_l, s_r, s_g) = _trunk_layer(h, B_ctx, S_ctx, l, 0, w, None, None, S_ctx)
        ks_l.append(k_l.reshape(B_ctx, S_ctx, ATT_KV_HEADS, ATT_HEAD_DIM))
        vs_l.append(v_l.reshape(B_ctx, S_ctx, ATT_KV_HEADS, ATT_HEAD_DIM))
        sr_l.append(s_r)
        sg_l.append(s_g.reshape(B_ctx, 2, GLA_HEADS, GLA_DK, GLA_DV))
    y_prompt = h.reshape(B_ctx, S_ctx, D)

    rope_tabs = _rope_tables(S_lat // GRID_W)
    ctx = (cache_attn_k.transpose(0, 1, 3, 2, 4), cache_attn_v.transpose(0, 1, 3, 2, 4), state_ret,
           state_gla.reshape(B_lat, DEPTH, 2, GLA_PAIRS, LANES, GLA_DV))
    g = x_sample.reshape(B_lat * S_lat, D)
    for l in range(DEPTH):
        g, _ = _trunk_layer(g, B_lat, S_lat, l, 1, w, rope_tabs, ctx, 256)
    y_sample = g.reshape(B_lat, S_lat, D)

    return (y_prompt, y_sample, jnp.stack(ks_l, axis=1), jnp.stack(vs_l, axis=1),
            jnp.stack(sr_l, axis=1), jnp.stack(sg_l, axis=1))
```

```python
import functools
import math

import jax
import jax.numpy as jnp
from jax import lax
from jax.experimental import pallas as pl
from jax.experimental.pallas import tpu as pltpu

F32 = jnp.float32
BF16 = jnp.bfloat16

D_MODEL = 2048
DEPTH = 2
GRID_W = 64
ATT_HEAD_DIM = 128
ATT_HEADS = 8
ATT_KV_HEADS = 2
ATT_GROUP = ATT_HEADS // ATT_KV_HEADS
ATT_WIDTH = ATT_HEADS * ATT_HEAD_DIM
ATT_KV_CHUNK = 512
ROPE_THETA = 10000.0
RET_DK = 128
RET_DV = 128
RET_HEADS = 4
RET_WIDTH = RET_HEADS * RET_DV
RET_CHUNK = 128
GLA_DK = 64
GLA_DV = 128
GLA_HEADS = 4
GLA_PAIRS = GLA_HEADS // 2
GLA_WIDTH = GLA_HEADS * GLA_DV
GLA_GATE_RANK = 16
GLA_TAU = 16.0
GLA_CHUNK = 64
D_FF = 5632
N_MOD = 9
MACARON_WEIGHT = 0.5
DEEPNORM_ALPHA = (2 * DEPTH) ** 0.25
LN_EPS = 1e-5
RMS_EPS = 1e-6

LANES = 128
PROJ_COLS = 5152
PROJ_COLS_PADDED = 5376
PROJ_TN = 768
COL_AQ, COL_AK, COL_AV = 0, 8, 10
COL_RQ, COL_RK, COL_RV, COL_RG = 12, 16, 20, 24
COL_GQ, COL_GK, COL_GV, COL_GR, COL_GA = 28, 30, 32, 36, 40

VMEM_LIMIT_BYTES = 56 * 1024 * 1024
ROW_TILE = 512
PROJ_ROW_TILE = 1024
FF_TILE = 512
FFN_ROW_TILE = 512
FFN_SPLIT = 2
MOD_TN = 1024
OUTPROJ_SPLIT = 4
LOOP_UNROLL = 8
SCAN_GROUP = 8
MOD_ROWS = 8


def _params(*sem):
    return pltpu.CompilerParams(dimension_semantics=sem, vmem_limit_bytes=VMEM_LIMIT_BYTES)


def _dot(a, b):
    return jnp.dot(a, b, preferred_element_type=F32)


def _dot_nt(a, b):
    return lax.dot_general(a, b, (((1,), (1,)), ((), ())), preferred_element_type=F32)


def _dot_tn(a, b):
    return lax.dot_general(a, b, (((0,), (0,)), ((), ())), preferred_element_type=F32)


def _silu(x):
    return x * jax.nn.sigmoid(x)


def _layer_norm(y, g, b):
    mu = jnp.mean(y, axis=-1, keepdims=True)
    d = y - mu
    var = jnp.mean(d * d, axis=-1, keepdims=True)
    return d * lax.rsqrt(var + LN_EPS) * g + b


def _rms_norm(x, g):
    return x * lax.rsqrt(jnp.mean(x * x, axis=-1, keepdims=True) + RMS_EPS) * g


def _rope(x, cos, sin_signed):
    lane = lax.broadcasted_iota(jnp.int32, x.shape, 1)
    partner = jnp.where((lane & 1) == 0, pltpu.roll(x, LANES - 1, 1), pltpu.roll(x, 1, 1))
    return x * cos + partner * sin_signed


def _mod_spec(l, who0, rows_per_mod, ndim_grid, row_tile=ROW_TILE):
    if ndim_grid == 1:
        return pl.BlockSpec((None, None, N_MOD, D_MODEL),
                            lambda m: (l, who0 + (m * row_tile) // rows_per_mod, 0, 0))
    return pl.BlockSpec((None, None, N_MOD, D_MODEL),
                        lambda m, n: (l, who0 + (m * row_tile) // rows_per_mod, 0, 0))


def _mod_kernel(c_ref, w_ref, b_ref, o_ref):
    a = _silu(c_ref[...]).astype(BF16)
    o_ref[...] = _dot(a, w_ref[...].astype(BF16)) + b_ref[...]


def _modulation(cvec, w_mod, b_mod):
    L, D, N = w_mod.shape
    return pl.pallas_call(
        _mod_kernel,
        grid=(L, N // MOD_TN),
        in_specs=[
            pl.BlockSpec((MOD_ROWS, D), lambda l, n: (0, 0)),
            pl.BlockSpec((None, D, MOD_TN), lambda l, n: (l, 0, n)),
            pl.BlockSpec((None, 1, MOD_TN), lambda l, n: (l, 0, n)),
        ],
        out_specs=pl.BlockSpec((None, MOD_ROWS, MOD_TN), lambda l, n: (l, 0, n)),
        out_shape=jax.ShapeDtypeStruct((L, MOD_ROWS, N), F32),
        compiler_params=_params("parallel", "parallel"),
        name="modulation",
    )(cvec, w_mod, b_mod)


def _ffn_kernel(*refs, mod_base, convert_next):
    if convert_next:
        (x_ref, mod_ref, wg_ref, wu_ref, wo_ref, g_ref, b_ref, next_in_ref, next_out_ref,
         o_ref, next_in_b_ref, next_out_b_ref, u_ref) = refs
        next_in_b_ref[...] = next_in_ref[...].astype(BF16)
        next_out_b_ref[...] = next_out_ref[...].astype(BF16)
    else:
        x_ref, mod_ref, wg_ref, wu_ref, wo_ref, g_ref, b_ref, o_ref, u_ref = refs
    f = pl.program_id(1)
    last = pl.num_programs(1) - 1

    def step(sl, first, final):
        if first:
            shift = mod_ref[mod_base:mod_base + 1, :]
            scale = mod_ref[mod_base + 1:mod_base + 2, :]
            u = (x_ref[sl, :] * (1.0 + scale) + shift).astype(BF16)
            u_ref[sl, :] = u
        else:
            u = u_ref[sl, :]
        gate = _dot(u, wg_ref[...])
        up = _dot(u, wu_ref[...])
        act = (_silu(gate) * up).astype(BF16)
        acc = _dot(act, wo_ref[...])
        if not first:
            acc = o_ref[sl, :] + acc
        if final:
            g3 = mod_ref[mod_base + 2:mod_base + 3, :]
            y = DEEPNORM_ALPHA * x_ref[sl, :] + MACARON_WEIGHT * (g3 * acc)
            acc = _layer_norm(y, g_ref[...], b_ref[...])
        o_ref[sl, :] = acc

    rows = x_ref.shape[0] // FFN_SPLIT
    subtiles = [slice(i * rows, (i + 1) * rows) for i in range(FFN_SPLIT)]

    @pl.when(f == 0)
    def _():
        for sl in subtiles:
            step(sl, True, False)

    @pl.when(jnp.logical_and(f > 0, f < last))
    def _():
        step(slice(None), False, False)

    @pl.when(f == last)
    def _():
        for sl in subtiles:
            step(sl, False, True)


def _ffn(x, mod, who0, rows_per_mod, w_in_b, w_out_b, ln_g, ln_b, l, half, convert_next=None):
    M, D = x.shape
    nm = M // FFN_ROW_TILE
    nf = D_FF // FF_TILE
    sub = 2 * half
    in_specs = [
        pl.BlockSpec((FFN_ROW_TILE, D), lambda m, f: (m, 0)),
        _mod_spec(l, who0, rows_per_mod, 2, FFN_ROW_TILE),
        pl.BlockSpec((D, FF_TILE), lambda m, f: (0, f)),
        pl.BlockSpec((D, FF_TILE), lambda m, f: (0, f + nf)),
        pl.BlockSpec((FF_TILE, D), lambda m, f: (f, 0)),
        pl.BlockSpec((None, None, 1, D), lambda m, f: (l, sub, 0, 0)),
        pl.BlockSpec((None, None, 1, D), lambda m, f: (l, sub, 0, 0)),
    ]
    args = [x, mod, w_in_b, w_in_b, w_out_b, ln_g, ln_b]
    out_specs = [pl.BlockSpec((FFN_ROW_TILE, D), lambda m, f: (m, 0))]
    out_shape = [jax.ShapeDtypeStruct((M, D), F32)]
    if convert_next is not None:
        w_in, w_out, l2, half2 = convert_next
        assert D % nm == 0 and (2 * D_FF) % nf == 0 and D_FF % nf == 0
        in_tile = (D // nm, 2 * D_FF // nf)
        out_tile = (D_FF // nf, D // nm)
        in_specs += [pl.BlockSpec((None, None) + in_tile, lambda m, f: (l2, half2, m, f)),
                     pl.BlockSpec((None, None) + out_tile, lambda m, f: (l2, half2, f, m))]
        args += [w_in, w_out]
        out_specs += [pl.BlockSpec(in_tile, lambda m, f: (m, f)), pl.BlockSpec(out_tile, lambda m, f: (f, m))]
        out_shape += [jax.ShapeDtypeStruct((D, 2 * D_FF), BF16), jax.ShapeDtypeStruct((D_FF, D), BF16)]
    return pl.pallas_call(
        functools.partial(_ffn_kernel, mod_base=3 * sub, convert_next=convert_next is not None),
        grid=(nm, nf),
        in_specs=in_specs,
        out_specs=out_specs,
        out_shape=out_shape,
        scratch_shapes=[pltpu.VMEM((FFN_ROW_TILE, D), BF16)],
        compiler_params=_params("parallel", "arbitrary"),
        name="ffn",
    )(*args)


def _inproj_kernel(x_ref, mod_ref, w_ref, o_ref, u_ref):
    @pl.when(pl.program_id(1) == 0)
    def _():
        shift = mod_ref[3:4, :]
        scale = mod_ref[4:5, :]
        u_ref[...] = (x_ref[...] * (1.0 + scale) + shift).astype(BF16)

    o_ref[...] = _dot(u_ref[...], w_ref[...]).astype(o_ref.dtype)


def _inproj(x, mod, who0, rows_per_mod, w, l):
    M, D = x.shape
    N = w.shape[2]
    return pl.pallas_call(
        _inproj_kernel,
        grid=(M // PROJ_ROW_TILE, N // PROJ_TN),
        in_specs=[
            pl.BlockSpec((PROJ_ROW_TILE, D), lambda m, n: (m, 0)),
            _mod_spec(l, who0, rows_per_mod, 2, PROJ_ROW_TILE),
            pl.BlockSpec((None, D, PROJ_TN), lambda m, n: (l, 0, n)),
        ],
        out_specs=pl.BlockSpec((PROJ_ROW_TILE, PROJ_TN), lambda m, n: (m, n)),
        out_shape=jax.ShapeDtypeStruct((M, N), BF16),
        scratch_shapes=[pltpu.VMEM((PROJ_ROW_TILE, D), BF16)],
        compiler_params=_params("parallel", "arbitrary"),
        name="inproj",
    )(x, mod, w)


def _attn_kernel(*refs, rope, cache_len, emit_kv, tq):
    it = iter(refs)
    q_ref, k_ref, v_ref, qn_ref, kn_ref = (next(it) for _ in range(5))
    if rope:
        cosq_ref, sinq_ref, cosk_ref, sink_ref = (next(it) for _ in range(4))
    if cache_len:
        ck_ref, cv_ref = next(it), next(it)
    o_ref = next(it)
    if emit_kv:
        newk_ref, newv_ref = next(it), next(it)
    kb_ref, vt_ref = next(it), next(it)
    T = kb_ref.shape[0]
    chunks = ([(0, cache_len)] if cache_len else []) + [
        (o, min(ATT_KV_CHUNK, T - o)) for o in range(cache_len, T, ATT_KV_CHUNK)]

    @pl.when(pl.program_id(2) == 0)
    def _():
        k = _rms_norm(k_ref[...].astype(F32), kn_ref[...])
        v = v_ref[...].astype(F32)
        if emit_kv:
            newk_ref[...] = k
            newv_ref[...] = v
        if rope:
            k = _rope(k, cosk_ref[...], sink_ref[...])
        if cache_len:
            k = jnp.concatenate([ck_ref[...], k], axis=0)
            v = jnp.concatenate([cv_ref[...], v], axis=0)
        kb_ref[...] = k.astype(BF16)
        for o, n in chunks:
            vt_ref[:, o:o + n] = v[o:o + n, :].T.astype(BF16)

    q = q_ref[...].astype(F32)
    heads = []
    for g in range(ATT_GROUP):
        qg = _rms_norm(q[:, g * LANES:(g + 1) * LANES], qn_ref[...])
        if rope:
            qg = _rope(qg, cosq_ref[...], sinq_ref[...])
        heads.append(qg.T)
    q_t = jnp.concatenate(heads, axis=1).astype(BF16)

    c2 = (ATT_HEAD_DIM ** -0.5) * math.log2(math.e)
    m = denom = acc = None
    for o, n in chunks:
        s = _dot(kb_ref[o:o + n, :], q_t)
        m_c = jnp.max(s, axis=0, keepdims=True)
        m = m_c if m is None else jnp.maximum(m, m_c)
    for o, n in chunks:
        s = _dot(kb_ref[o:o + n, :], q_t)
        p = jnp.exp2((s - m) * c2)
        d_c = jnp.sum(p, axis=0, keepdims=True)
        a_c = _dot(vt_ref[:, o:o + n], p.astype(BF16))
        denom = d_c if denom is None else denom + d_c
        acc = a_c if acc is None else acc + a_c
    o_t = acc * (1.0 / denom)
    for g in range(ATT_GROUP):
        o_ref[:, g * LANES:(g + 1) * LANES] = o_t[:, g * tq:(g + 1) * tq].T.astype(o_ref.dtype)


def _attention(P, B, S, q_norm, k_norm, l, rope_tabs, cache, emit_kv, tq):
    nq = S // tq
    cache_len = cache[0].shape[3] if cache is not None else 0
    T = cache_len + S
    gw = ATT_GROUP * ATT_HEAD_DIM
    norm_spec = pl.BlockSpec((None, 1, LANES), lambda b, j, i: (l, 0, 0))
    in_specs = [
        pl.BlockSpec((tq, gw), lambda b, j, i: (b * nq + i, j)),
        pl.BlockSpec((S, LANES), lambda b, j, i: (b, COL_AK + j)),
        pl.BlockSpec((S, LANES), lambda b, j, i: (b, COL_AV + j)),
        norm_spec, norm_spec,
    ]
    args = [P, P, P, q_norm, k_norm]
    if rope_tabs is not None:
        cos, sin = rope_tabs
        in_specs += [
            pl.BlockSpec((tq, LANES), lambda b, j, i: (i, 0)),
            pl.BlockSpec((tq, LANES), lambda b, j, i: (i, 0)),
            pl.BlockSpec((S, LANES), lambda b, j, i: (0, 0)),
            pl.BlockSpec((S, LANES), lambda b, j, i: (0, 0)),
        ]
        args += [cos, sin, cos, sin]
    if cache is not None:
        spec = pl.BlockSpec((None, None, None, cache_len, LANES), lambda b, j, i: (b, l, j, 0, 0))
        in_specs += [spec, spec]
        args += list(cache)
    out_specs = [pl.BlockSpec((tq, gw), lambda b, j, i: (b * nq + i, j))]
    out_shape = [jax.ShapeDtypeStruct((B * S, ATT_WIDTH), BF16)]
    if emit_kv:
        kv_spec = pl.BlockSpec((S, LANES), lambda b, j, i: (b, j))
        out_specs += [kv_spec, kv_spec]
        out_shape += [jax.ShapeDtypeStruct((B * S, ATT_KV_HEADS * ATT_HEAD_DIM), F32)] * 2
    return pl.pallas_call(
        functools.partial(_attn_kernel, rope=rope_tabs is not None, cache_len=cache_len,
                          emit_kv=emit_kv, tq=tq),
        grid=(B, ATT_KV_HEADS, nq),
        in_specs=in_specs,
        out_specs=out_specs,
        out_shape=out_shape,
        scratch_shapes=[pltpu.VMEM((T, LANES), BF16), pltpu.VMEM((LANES, T), BF16)],
        compiler_params=_params("parallel", "parallel", "arbitrary"),
        name="attention",
    )(*args)


def _ret_kernel(*refs, l, rope, has_s0, emit_state, S):
    it = iter(refs)
    lg_ref, q_ref, k_ref, v_ref, g_ref = (next(it) for _ in range(5))
    if rope:
        cos_ref, sin_ref = next(it), next(it)
    if has_s0:
        s0_ref = next(it)
    o_ref = next(it)
    if emit_state:
        st_ref = next(it)
    qb_ref, kb_ref, qd_ref, kd_ref, work_ref, sb_ref = (next(it) for _ in range(6))

    h = pl.program_id(1)
    C = RET_CHUNK
    n = S // C
    lg_f = lg_ref[l, 0, h]
    lg_b = lg_ref[l, 1, h]
    q = q_ref[...].astype(F32)
    k = k_ref[...].astype(F32) * (RET_DK ** -0.5)
    if rope:
        q = _rope(q, cos_ref[...], sin_ref[...])
        k = _rope(k, cos_ref[...], sin_ref[...])

    t = lax.broadcasted_iota(jnp.int32, (C, LANES), 0).astype(F32)

    def scaled(x, expo):
        return (x.reshape(n, C, LANES) * jnp.exp(expo)[None]).reshape(S, LANES).astype(BF16)

    qb_ref[...] = q.astype(BF16)
    kb_ref[...] = k.astype(BF16)
    qd_ref[:, 0:RET_DK] = scaled(q, (t + 1.0) * lg_f)
    qd_ref[:, RET_DK:] = scaled(q, (C - t) * lg_b)
    kd_ref[:, 0:RET_DK] = scaled(k, (C - 1.0 - t) * lg_f)
    kd_ref[:, RET_DK:] = scaled(k, t * lg_b)

    ii = lax.broadcasted_iota(jnp.int32, (C, C), 0).astype(F32)
    jj = lax.broadcasted_iota(jnp.int32, (C, C), 1).astype(F32)
    dmat = (jnp.where(ii >= jj, jnp.exp(jnp.maximum(ii - jj, 0.0) * lg_f), 0.0)
            + jnp.where(jj >= ii, jnp.exp(jnp.maximum(jj - ii, 0.0) * lg_b), 0.0))
    decay_f = jnp.exp(jnp.full((RET_DK, RET_DV), C * lg_f, F32))
    decay_b = jnp.exp(jnp.full((RET_DK, RET_DV), C * lg_b, F32))
    if has_s0:
        state0 = (s0_ref[0], s0_ref[1])
    else:
        state0 = (jnp.zeros((RET_DK, RET_DV), F32), jnp.zeros((RET_DK, RET_DV), F32))

    def rows(c):
        return pl.ds(pl.multiple_of(c * C, C), C)

    def state_updates(c, carry):
        sl = rows(c)
        update = _dot_tn(kd_ref[sl, :], v_ref[sl, :])
        work_ref[0, sl, :] = update[0:RET_DK, :]
        work_ref[1, sl, :] = update[RET_DK:, :]
        return carry

    lax.fori_loop(0, n, state_updates, 0, unroll=LOOP_UNROLL)

    def scan(c, carry):
        state_f, state_b = carry
        cb = n - 1 - c
        sb_ref[c, 0:RET_DK, :] = state_f.astype(BF16)
        sb_ref[cb, RET_DK:, :] = state_b.astype(BF16)
        return (decay_f * state_f + work_ref[0, rows(c), :], decay_b * state_b + work_ref[1, rows(cb), :])

    state_f, state_b = lax.fori_loop(0, n, scan, state0, unroll=LOOP_UNROLL)
    if emit_state:
        st_ref[0] = state_f
        st_ref[1] = state_b

    U = min(SCAN_GROUP, n)

    def outputs(j, carry):
        chunks = [j * U + u for u in range(U)]
        scores = [(_dot_nt(qb_ref[rows(c), :], kb_ref[rows(c), :]) * dmat).astype(BF16) for c in chunks]
        cross = [_dot(qd_ref[rows(c), :], sb_ref[c]) for c in chunks]
        for c, sc, x in zip(chunks, scores, cross):
            work_ref[0, rows(c), :] = _dot(sc, v_ref[rows(c), :]) + x
        return carry

    lax.fori_loop(0, n // U, outputs, 0)

    y = work_ref[0]
    mu = jnp.mean(y, axis=-1, keepdims=True)
    yc = y - mu
    var = jnp.mean(yc * yc, axis=-1, keepdims=True)
    gate = _silu(g_ref[...].astype(F32))
    o_ref[...] = (yc * lax.rsqrt(var + LN_EPS) * gate).astype(o_ref.dtype)


def _retention(P, B, S, log_decay, l, rope_tabs, s0, emit_state):
    def col(c0):
        return pl.BlockSpec((S, LANES), lambda b, h: (b, c0 + h))

    in_specs = [pl.BlockSpec(memory_space=pltpu.SMEM), col(COL_RQ), col(COL_RK), col(COL_RV), col(COL_RG)]
    args = [log_decay, P, P, P, P]
    if rope_tabs is not None:
        tab = pl.BlockSpec((S, LANES), lambda b, h: (0, 0))
        in_specs += [tab, tab]
        args += list(rope_tabs)
    if s0 is not None:
        in_specs.append(pl.BlockSpec((None, None, 2, None, RET_DK, RET_DV), lambda b, h: (b, l, 0, h, 0, 0)))
        args.append(s0)
    out_specs = [pl.BlockSpec((S, LANES), lambda b, h: (b, h))]
    out_shape = [jax.ShapeDtypeStruct((B * S, RET_WIDTH), BF16)]
    if emit_state:
        out_specs.append(pl.BlockSpec((None, 2, None, RET_DK, RET_DV), lambda b, h: (b, 0, h, 0, 0)))
        out_shape.append(jax.ShapeDtypeStruct((B, 2, RET_HEADS, RET_DK, RET_DV), F32))
    return pl.pallas_call(
        functools.partial(_ret_kernel, l=l, rope=rope_tabs is not None, has_s0=s0 is not None,
                          emit_state=emit_state, S=S),
        grid=(B, RET_HEADS),
        in_specs=in_specs,
        out_specs=out_specs,
        out_shape=out_shape,
        scratch_shapes=[pltpu.VMEM((S, RET_DK), BF16), pltpu.VMEM((S, RET_DK), BF16),
                        pltpu.VMEM((S, 2 * RET_DK), BF16), pltpu.VMEM((S, 2 * RET_DK), BF16),
                        pltpu.VMEM((2, S, RET_DV), F32),
                        pltpu.VMEM((S // RET_CHUNK, 2 * RET_DK, RET_DV), BF16)],
        compiler_params=_params("parallel", "parallel"),
        name="retention",
    )(*args)


def _gla_kernel(*refs, has_s0, emit_state, S):
    it = iter(refs)
    q_ref, k_ref, v_ref, gr_ref, ga_ref, wa_ref, ba_ref, gn_ref = (next(it) for _ in range(8))
    if has_s0:
        s0_ref = next(it)
    o_ref = next(it)
    if emit_state:
        st_ref = next(it)
    la_ref, qt_ref, kcat_ref, dec_ref, work_ref, sb_ref = (next(it) for _ in range(6))
    C = GLA_CHUNK
    n = S // C
    U = min(SCAN_GROUP, n)
    ga = ga_ref[...]
    for d in range(2):
        pre = _dot(ga, wa_ref[d]) + ba_ref[d]
        la_ref[:, d * LANES:(d + 1) * LANES] = (
            (jnp.minimum(pre, 0.0) - jnp.log1p(jnp.exp(-jnp.abs(pre)))) * (1.0 / GLA_TAU))

    ii = lax.broadcasted_iota(jnp.int32, (C, C), 0)
    jj = lax.broadcasted_iota(jnp.int32, (C, C), 1)
    lower_b = jnp.where(jj <= ii, 1.0, 0.0).astype(BF16)
    mask_f = jnp.concatenate([jnp.where(jj <= ii, 1.0, 0.0)] * 2, axis=1)
    mask_b = jnp.concatenate([jnp.where(jj >= ii, 1.0, 0.0)] * 2, axis=1)
    lane = lax.broadcasted_iota(jnp.int32, (C, LANES), 1)
    head0_lanes = lane < GLA_DK
    lane_sq = lax.broadcasted_iota(jnp.int32, (LANES, LANES), 1)
    row_sq = lax.broadcasted_iota(jnp.int32, (LANES, LANES), 0)
    lane_wide = lax.broadcasted_iota(jnp.int32, (C, 2 * GLA_DV), 1)

    def rows(c):
        return pl.ds(pl.multiple_of(c * C, C), C)

    def update_rows(c):
        return pl.ds(pl.multiple_of(c * GLA_DV, GLA_DV), GLA_DV)

    def prepare(j, carry):
        chunks = [j * U + u for u in range(U)]
        cums = []
        for c in chunks:
            la = la_ref[rows(c), :]
            hi = la.astype(BF16)
            r1 = la - hi.astype(F32)
            mid = r1.astype(BF16)
            lo = (r1 - mid.astype(F32)).astype(BF16)
            cums.append((la, _dot(lower_b, hi) + _dot(lower_b, mid) + _dot(lower_b, lo)))
        scaled_keys = []
        for c, (la, cum) in zip(chunks, cums):
            sl = rows(c)
            qc = q_ref[sl, :].astype(F32) * (GLA_DK ** -0.5)
            kc = k_ref[sl, :].astype(F32)
            b_f = cum[:, 0:LANES]
            cum_b = cum[:, LANES:]
            b_b = cum_b[C - 1:C, :] - cum_b + la[:, LANES:]
            end_f = b_f[C - 1:C, :]
            end_b = b_b[0:1, :]
            qt_ref[sl, 0:LANES] = (qc * jnp.exp(b_f)).astype(BF16)
            qt_ref[sl, LANES:] = (qc * jnp.exp(b_b)).astype(BF16)
            for d, b in ((0, b_f), (1, b_b)):
                k_t = kc * jnp.exp(-b)
                kcat_ref[d, c, 0:C, :] = jnp.where(head0_lanes, k_t, 0.0).astype(BF16)
                kcat_ref[d, c, C:, :] = jnp.where(head0_lanes, 0.0, k_t).astype(BF16)
            dec_ref[0, c] = jnp.exp(end_f)
            dec_ref[1, c] = jnp.exp(end_b)
            scaled_keys.append(jnp.concatenate(
                [kc * jnp.exp(end_f - b_f), kc * jnp.exp(end_b - b_b)], axis=1).astype(BF16))
        for c, k_s in zip(chunks, scaled_keys):
            kv = _dot_tn(v_ref[rows(c), :], k_s)
            for d in range(2):
                cols = kv[:, d * LANES:(d + 1) * LANES]
                work_ref[d, update_rows(c), :] = jnp.where(lane_sq < GLA_DK, cols[0:GLA_DV, :], cols[GLA_DV:, :])
        return carry

    lax.fori_loop(0, n // U, prepare, 0)

    if has_s0:
        state0 = (s0_ref[0].T, s0_ref[1].T)
    else:
        state0 = (jnp.zeros((GLA_DV, LANES), F32), jnp.zeros((GLA_DV, LANES), F32))

    def scan(c, carry):
        state_f, state_b = carry
        cb = n - 1 - c
        for d, cc, state in ((0, c, state_f), (1, cb, state_b)):
            s2 = state.T
            r0 = d * LANES
            sb_ref[cc, r0:r0 + LANES, 0:GLA_DV] = jnp.where(row_sq < GLA_DK, s2, 0.0).astype(BF16)
            sb_ref[cc, r0:r0 + LANES, GLA_DV:] = jnp.where(row_sq < GLA_DK, 0.0, s2).astype(BF16)
        return (state_f * dec_ref[0, c] + work_ref[0, update_rows(c), :],
                state_b * dec_ref[1, cb] + work_ref[1, update_rows(cb), :])

    states = lax.fori_loop(0, n, scan, state0, unroll=LOOP_UNROLL)
    if emit_state:
        st_ref[0] = states[0].T
        st_ref[1] = states[1].T

    def outputs(j, carry):
        chunks = [j * U + u for u in range(U)]
        scores, cross = [], []
        for c in chunks:
            q_t = qt_ref[rows(c), :]
            att = (_dot_nt(q_t[:, 0:LANES], kcat_ref[0, c]) * mask_f
                   + _dot_nt(q_t[:, LANES:], kcat_ref[1, c]) * mask_b)
            scores.append(att.astype(BF16))
            cross.append(_dot(q_t, sb_ref[c]))
        for c, att, x in zip(chunks, scores, cross):
            v_c = v_ref[rows(c), :]
            zeros = jnp.zeros_like(v_c)
            v_bd = jnp.concatenate([jnp.where(lane_wide < GLA_DV, v_c, zeros),
                                    jnp.where(lane_wide < GLA_DV, zeros, v_c)], axis=0)
            y = _dot(att, v_bd) + x
            for i in range(2):
                work_ref[0, pl.ds(pl.multiple_of(i * S + c * C, C), C), :] = y[:, i * GLA_DV:(i + 1) * GLA_DV]
        return carry

    lax.fori_loop(0, n // U, outputs, 0)

    for i in range(2):
        y = _rms_norm(work_ref[0, i * S:(i + 1) * S, :], gn_ref[...])
        gate = _silu(gr_ref[:, i * GLA_DV:(i + 1) * GLA_DV].astype(F32))
        o_ref[:, i * GLA_DV:(i + 1) * GLA_DV] = (y * gate).astype(o_ref.dtype)


def _gla(P, B, S, wa, ba, gla_norm, l, s0, emit_state):
    pw = 2 * GLA_DV
    in_specs = [
        pl.BlockSpec((S, LANES), lambda b, p: (b, COL_GQ + p)),
        pl.BlockSpec((S, LANES), lambda b, p: (b, COL_GK + p)),
        pl.BlockSpec((S, pw), lambda b, p: (b, COL_GV // 2 + p)),
        pl.BlockSpec((S, pw), lambda b, p: (b, COL_GR // 2 + p)),
        pl.BlockSpec((S, LANES), lambda b, p: (b, COL_GA)),
        pl.BlockSpec((None, 2, LANES, LANES), lambda b, p: (l, 0, 0, p)),
        pl.BlockSpec((None, 2, 1, LANES), lambda b, p: (l, 0, 0, p)),
        pl.BlockSpec((None, 1, GLA_DV), lambda b, p: (l, 0, 0)),
    ]
    args = [P, P, P, P, P, wa, ba, gla_norm]
    if s0 is not None:
        in_specs.append(pl.BlockSpec((None, None, 2, None, LANES, GLA_DV), lambda b, p: (b, l, 0, p, 0, 0)))
        args.append(s0)
    out_specs = [pl.BlockSpec((S, pw), lambda b, p: (b, p))]
    out_shape = [jax.ShapeDtypeStruct((B * S, GLA_WIDTH), BF16)]
    if emit_state:
        out_specs.append(pl.BlockSpec((None, 2, None, LANES, GLA_DV), lambda b, p: (b, 0, p, 0, 0)))
        out_shape.append(jax.ShapeDtypeStruct((B, 2, GLA_PAIRS, LANES, GLA_DV), F32))
    return pl.pallas_call(
        functools.partial(_gla_kernel, has_s0=s0 is not None, emit_state=emit_state, S=S),
        grid=(B, GLA_PAIRS),
        in_specs=in_specs,
        out_specs=out_specs,
        out_shape=out_shape,
        scratch_shapes=[pltpu.VMEM((S, 2 * LANES), F32),
                        pltpu.VMEM((S, 2 * LANES), BF16),
                        pltpu.VMEM((2, S // GLA_CHUNK, 2 * GLA_CHUNK, LANES), BF16),
                        pltpu.VMEM((2, S // GLA_CHUNK, 1, LANES), F32),
                        pltpu.VMEM((2, 2 * S, GLA_DV), F32),
                        pltpu.VMEM((S // GLA_CHUNK, 2 * LANES, 2 * GLA_DV), BF16)],
        compiler_params=_params("parallel", "parallel"),
        name="gla",
    )(*args)


def _outproj_kernel(x_ref, mod_ref, a_ref, r_ref, gl_ref, wa_ref, wr_ref, wg_ref, g_ref, b_ref, o_ref):
    rows = x_ref.shape[0] // OUTPROJ_SPLIT
    mixed = []
    for i in range(OUTPROJ_SPLIT):
        sl = slice(i * rows, (i + 1) * rows)
        mixed.append(_dot(a_ref[sl, :], wa_ref[...]) + _dot(r_ref[sl, :], wr_ref[...])
                     + _dot(gl_ref[sl, :], wg_ref[...]))
    for i in range(OUTPROJ_SPLIT):
        sl = slice(i * rows, (i + 1) * rows)
        y = DEEPNORM_ALPHA * x_ref[sl, :] + mod_ref[5:6, :] * mixed[i]
        o_ref[sl, :] = _layer_norm(y, g_ref[...], b_ref[...])


def _outproj(x, mod, who0, rows_per_mod, att, ret, gla, w, ln_g, ln_b, l):
    M, D = x.shape
    return pl.pallas_call(
        _outproj_kernel,
        grid=(M // ROW_TILE,),
        in_specs=[
            pl.BlockSpec((ROW_TILE, D), lambda m: (m, 0)),
            _mod_spec(l, who0, rows_per_mod, 1),
            pl.BlockSpec((ROW_TILE, ATT_WIDTH), lambda m: (m, 0)),
            pl.BlockSpec((ROW_TILE, RET_WIDTH), lambda m: (m, 0)),
            pl.BlockSpec((ROW_TILE, GLA_WIDTH), lambda m: (m, 0)),
            pl.BlockSpec((None, ATT_WIDTH, D), lambda m: (l, 0, 0)),
            pl.BlockSpec((None, RET_WIDTH, D), lambda m: (l, ATT_WIDTH // RET_WIDTH, 0)),
            pl.BlockSpec((None, GLA_WIDTH, D), lambda m: (l, (ATT_WIDTH + RET_WIDTH) // GLA_WIDTH, 0)),
            pl.BlockSpec((None, None, 1, D), lambda m: (l, 1, 0, 0)),
            pl.BlockSpec((None, None, 1, D), lambda m: (l, 1, 0, 0)),
        ],
        out_specs=pl.BlockSpec((ROW_TILE, D), lambda m: (m, 0)),
        out_shape=jax.ShapeDtypeStruct((M, D), F32),
        compiler_params=_params("parallel"),
        name="outproj",
    )(x, mod, att, ret, gla, w, w, w, ln_g, ln_b)


def _trunk_layer(x, B, S, l, who0, w, rope_tabs, ctx, attn_tq):
    is_context = ctx is None
    rows_per_mod = x.shape[0] if is_context else S
    mod = w["mod"]

    def ffn(x, half):
        nxt = (l, 1) if half == 0 else (l + 1, 0)
        convert = None
        if nxt[0] < DEPTH and nxt not in w["ffn_bf16"]:
            convert = (w["ffn_w_in"], w["ffn_w_out"], nxt[0], nxt[1])
        outs = _ffn(x, mod, who0, rows_per_mod, *w["ffn_bf16"][(l, half)], w["ln_g"], w["ln_b"], l, half, convert)
        if convert is not None:
            w["ffn_bf16"][nxt] = (outs[1], outs[2])
        return outs[0]

    x = ffn(x, 0)
    P = _inproj(x, mod, who0, rows_per_mod, w["mix_w_in"], l)
    cache = None if is_context else (ctx[0], ctx[1])
    s_ret0 = None if is_context else ctx[2]
    s_gla0 = None if is_context else ctx[3]
    att_out = _attention(P, B, S, w["q_norm"], w["k_norm"], l, rope_tabs, cache, is_context, attn_tq)
    ret_out = _retention(P, B, S, w["log_decay"], l, rope_tabs, s_ret0, is_context)
    gla_out = _gla(P, B, S, w["gla_wa"], w["gla_ba"], w["gla_norm"], l, s_gla0, is_context)
    x = _outproj(x, mod, who0, rows_per_mod, att_out[0], ret_out[0], gla_out[0],
                 w["mix_w_out"], w["ln_g"], w["ln_b"], l)
    x = ffn(x, 1)
    new_ctx = (att_out[1], att_out[2], ret_out[1], gla_out[1]) if is_context else None
    return x, new_ctx


def _rope_tables(rows):
    row = jnp.repeat(jnp.arange(rows, dtype=F32), GRID_W)
    col = jnp.tile(jnp.arange(GRID_W, dtype=F32), rows)
    n_freq = ATT_HEAD_DIM // 4
    inv = ROPE_THETA ** (-jnp.arange(n_freq, dtype=F32) / n_freq)
    ang = jnp.concatenate([row[:, None] * inv, col[:, None] * inv], axis=-1)
    cos, sin = jnp.cos(ang), jnp.sin(ang)
    cos_full = jnp.repeat(cos, 2, axis=-1)
    sin_signed = jnp.stack([-sin, sin], axis=-1).reshape(ang.shape[0], ATT_HEAD_DIM)
    return cos_full, sin_signed


def kernel(x_prompt, x_sample, c, cache_attn_k, cache_attn_v, state_ret, state_gla, c_ctx,
           w_mod, b_mod, ln_g, ln_b, ffn_w_in, ffn_w_out, mix_w_in, mix_w_out,
           att_q_norm, att_k_norm, ret_log_decay, gla_w_a2, gla_b_a, gla_norm):
    B_ctx, S_ctx, D = x_prompt.shape
    B_lat, S_lat, _ = x_sample.shape

    gla_wa = jnp.zeros((DEPTH, 2, LANES, GLA_HEADS * GLA_DK), BF16)
    for d in range(2):
        gla_wa = gla_wa.at[:, d, d * GLA_GATE_RANK:(d + 1) * GLA_GATE_RANK, :].set(gla_w_a2[:, d].astype(BF16))

    cvec = jnp.concatenate([c_ctx[None, :], c, jnp.zeros((MOD_ROWS - 1 - B_lat, D), F32)], axis=0)
    mod = _modulation(cvec, w_mod, b_mod[:, None, :]).reshape(DEPTH, MOD_ROWS, N_MOD, D)

    w = dict(
        mod=mod,
        ffn_w_in=ffn_w_in, ffn_w_out=ffn_w_out,
        ffn_bf16={(0, 0): (ffn_w_in[0, 0].astype(BF16), ffn_w_out[0, 0].astype(BF16))},
        mix_w_in=jnp.pad(mix_w_in.astype(BF16), ((0, 0), (0, 0), (0, PROJ_COLS_PADDED - PROJ_COLS))),
        mix_w_out=mix_w_out.astype(BF16),
        ln_g=ln_g[:, :, None, :], ln_b=ln_b[:, :, None, :],
        q_norm=att_q_norm[:, None, :], k_norm=att_k_norm[:, None, :],
        log_decay=ret_log_decay, gla_wa=gla_wa, gla_ba=gla_b_a[:, :, None, :],
        gla_norm=gla_norm[:, None, :])

    h = x_prompt.reshape(B_ctx * S_ctx, D)
    ks_l, vs_l, sr_l, sg_l = [], [], [], []
    for l in range(DEPTH):
        h, (k_l, v_l, s_r, s_g) = _trunk_layer(h, B_ctx, S_ctx, l, 0, w, None, None, S_ctx)
        ks_l.append(k_l.reshape(B_ctx, S_ctx, ATT_KV_HEADS, ATT_HEAD_DIM))
        vs_l.append(v_l.reshape(B_ctx, S_ctx, ATT_KV_HEADS, ATT_HEAD_DIM))
        sr_l.append(s_r)
        sg_l.append(s_g.reshape(B_ctx, 2, GLA_HEADS, GLA_DK, GLA_DV))
    y_prompt = h.reshape(B_ctx, S_ctx, D)

    rope_tabs = _rope_tables(S_lat // GRID_W)
    ctx = (cache_attn_k.transpose(0, 1, 3, 2, 4), cache_attn_v.transpose(0, 1, 3, 2, 4), state_ret,
           state_gla.reshape(B_lat, DEPTH, 2, GLA_PAIRS, LANES, GLA_DV))
    g = x_sample.reshape(B_lat * S_lat, D)
    for l in range(DEPTH):
        g, _ = _trunk_layer(g, B_lat, S_lat, l, 1, w, rope_tabs, ctx, 256)
    y_sample = g.reshape(B_lat, S_lat, D)

    return (y_prompt, y_sample, jnp.stack(ks_l, axis=1), jnp.stack(vs_l, axis=1),
            jnp.stack(sr_l, axis=1), jnp.stack(sg_l, axis=1))
```

```python
import functools
import math

import jax
import jax.numpy as jnp
from jax import lax
from jax.experimental import pallas as pl
from jax.experimental.pallas import tpu as pltpu

F32 = jnp.float32
BF16 = jnp.bfloat16

D_MODEL = 2048
DEPTH = 2
GRID_W = 64
ATT_HEAD_DIM = 128
ATT_HEADS = 8
ATT_KV_HEADS = 2
ATT_GROUP = ATT_HEADS // ATT_KV_HEADS
ATT_WIDTH = ATT_HEADS * ATT_HEAD_DIM
ATT_KV_CHUNK = 512
ROPE_THETA = 10000.0
RET_DK = 128
RET_DV = 128
RET_HEADS = 4
RET_WIDTH = RET_HEADS * RET_DV
RET_CHUNK = 128
GLA_DK = 64
GLA_DV = 128
GLA_HEADS = 4
GLA_PAIRS = GLA_HEADS // 2
GLA_WIDTH = GLA_HEADS * GLA_DV
GLA_GATE_RANK = 16
GLA_TAU = 16.0
GLA_CHUNK = 64
D_FF = 5632
N_MOD = 9
MACARON_WEIGHT = 0.5
DEEPNORM_ALPHA = (2 * DEPTH) ** 0.25
LN_EPS = 1e-5
RMS_EPS = 1e-6

LANES = 128
PROJ_COLS = 5152
PROJ_COLS_PADDED = 5376
PROJ_TN = 1792
COL_AQ, COL_AK, COL_AV = 0, 8, 10
COL_RQ, COL_RK, COL_RV, COL_RG = 12, 16, 20, 24
COL_GQ, COL_GK, COL_GV, COL_GR, COL_GA = 28, 30, 32, 36, 40

VMEM_LIMIT_BYTES = 56 * 1024 * 1024
ROW_TILE = 512
PROJ_ROW_TILE = 1024
FF_TILE = 512
FFN_ROW_TILE = 512
FFN_SPLIT = 2
MOD_TN = 2048
OUTPROJ_SPLIT = 4
LOOP_UNROLL = 8
SCAN_GROUP = 8
MOD_ROWS = 8


def _params(*sem):
    return pltpu.CompilerParams(dimension_semantics=sem, vmem_limit_bytes=VMEM_LIMIT_BYTES)


def _dot(a, b):
    return jnp.dot(a, b, preferred_element_type=F32)


def _dot_nt(a, b):
    return lax.dot_general(a, b, (((1,), (1,)), ((), ())), preferred_element_type=F32)


def _dot_tn(a, b):
    return lax.dot_general(a, b, (((0,), (0,)), ((), ())), preferred_element_type=F32)


def _silu(x):
    return x * jax.nn.sigmoid(x)


def _layer_norm(y, g, b):
    mu = jnp.mean(y, axis=-1, keepdims=True)
    d = y - mu
    var = jnp.mean(d * d, axis=-1, keepdims=True)
    return d * lax.rsqrt(var + LN_EPS) * g + b


def _rms_norm(x, g):
    return x * lax.rsqrt(jnp.mean(x * x, axis=-1, keepdims=True) + RMS_EPS) * g


def _rope(x, cos, sin_signed):
    lane = lax.broadcasted_iota(jnp.int32, x.shape, 1)
    partner = jnp.where((lane & 1) == 0, pltpu.roll(x, LANES - 1, 1), pltpu.roll(x, 1, 1))
    return x * cos + partner * sin_signed


def _mod_spec(l, who0, rows_per_mod, ndim_grid, row_tile=ROW_TILE):
    if ndim_grid == 1:
        return pl.BlockSpec((None, None, N_MOD, D_MODEL),
                            lambda m: (l, who0 + (m * row_tile) // rows_per_mod, 0, 0))
    return pl.BlockSpec((None, None, N_MOD, D_MODEL),
                        lambda m, n: (l, who0 + (m * row_tile) // rows_per_mod, 0, 0))


def _mod_kernel(c_ref, w_ref, b_ref, o_ref):
    a = _silu(c_ref[...]).astype(BF16)
    o_ref[...] = _dot(a, w_ref[...].astype(BF16)) + b_ref[...]


def _modulation(cvec, w_mod, b_mod):
    L, D, N = w_mod.shape
    return pl.pallas_call(
        _mod_kernel,
        grid=(L, N // MOD_TN),
        in_specs=[
            pl.BlockSpec((MOD_ROWS, D), lambda l, n: (0, 0)),
            pl.BlockSpec((None, D, MOD_TN), lambda l, n: (l, 0, n)),
            pl.BlockSpec((None, 1, MOD_TN), lambda l, n: (l, 0, n)),
        ],
        out_specs=pl.BlockSpec((None, MOD_ROWS, MOD_TN), lambda l, n: (l, 0, n)),
        out_shape=jax.ShapeDtypeStruct((L, MOD_ROWS, N), F32),
        compiler_params=_params("parallel", "parallel"),
        name="modulation",
    )(cvec, w_mod, b_mod)


def _ffn_kernel(*refs, mod_base, convert_next):
    if convert_next:
        (x_ref, mod_ref, wg_ref, wu_ref, wo_ref, g_ref, b_ref, next_in_ref, next_out_ref,
         o_ref, next_in_b_ref, next_out_b_ref, u_ref) = refs
        next_in_b_ref[...] = next_in_ref[...].astype(BF16)
        next_out_b_ref[...] = next_out_ref[...].astype(BF16)
    else:
        x_ref, mod_ref, wg_ref, wu_ref, wo_ref, g_ref, b_ref, o_ref, u_ref = refs
    f = pl.program_id(1)
    last = pl.num_programs(1) - 1

    def step(sl, first, final):
        if first:
            shift = mod_ref[mod_base:mod_base + 1, :]
            scale = mod_ref[mod_base + 1:mod_base + 2, :]
            u = (x_ref[sl, :] * (1.0 + scale) + shift).astype(BF16)
            u_ref[sl, :] = u
        else:
            u = u_ref[sl, :]
        gate = _dot(u, wg_ref[...])
        up = _dot(u, wu_ref[...])
        act = (_silu(gate) * up).astype(BF16)
        acc = _dot(act, wo_ref[...])
        if not first:
            acc = o_ref[sl, :] + acc
        if final:
            g3 = mod_ref[mod_base + 2:mod_base + 3, :]
            y = DEEPNORM_ALPHA * x_ref[sl, :] + MACARON_WEIGHT * (g3 * acc)
            acc = _layer_norm(y, g_ref[...], b_ref[...])
        o_ref[sl, :] = acc

    rows = x_ref.shape[0] // FFN_SPLIT
    subtiles = [slice(i * rows, (i + 1) * rows) for i in range(FFN_SPLIT)]

    @pl.when(f == 0)
    def _():
        for sl in subtiles:
            step(sl, True, False)

    @pl.when(jnp.logical_and(f > 0, f < last))
    def _():
        step(slice(None), False, False)

    @pl.when(f == last)
    def _():
        for sl in subtiles:
            step(sl, False, True)


def _ffn(x, mod, who0, rows_per_mod, w_in_b, w_out_b, ln_g, ln_b, l, half, convert_next=None):
    M, D = x.shape
    nm = M // FFN_ROW_TILE
    nf = D_FF // FF_TILE
    sub = 2 * half
    in_specs = [
        pl.BlockSpec((FFN_ROW_TILE, D), lambda m, f: (m, 0)),
        _mod_spec(l, who0, rows_per_mod, 2, FFN_ROW_TILE),
        pl.BlockSpec((D, FF_TILE), lambda m, f: (0, f)),
        pl.BlockSpec((D, FF_TILE), lambda m, f: (0, f + nf)),
        pl.BlockSpec((FF_TILE, D), lambda m, f: (f, 0)),
        pl.BlockSpec((None, None, 1, D), lambda m, f: (l, sub, 0, 0)),
        pl.BlockSpec((None, None, 1, D), lambda m, f: (l, sub, 0, 0)),
    ]
    args = [x, mod, w_in_b, w_in_b, w_out_b, ln_g, ln_b]
    out_specs = [pl.BlockSpec((FFN_ROW_TILE, D), lambda m, f: (m, 0))]
    out_shape = [jax.ShapeDtypeStruct((M, D), F32)]
    if convert_next is not None:
        w_in, w_out, l2, half2 = convert_next
        assert D % nm == 0 and (2 * D_FF) % nf == 0 and D_FF % nf == 0
        in_tile = (D // nm, 2 * D_FF // nf)
        out_tile = (D_FF // nf, D // nm)
        in_specs += [pl.BlockSpec((None, None) + in_tile, lambda m, f: (l2, half2, m, f)),
                     pl.BlockSpec((None, None) + out_tile, lambda m, f: (l2, half2, f, m))]
        args += [w_in, w_out]
        out_specs += [pl.BlockSpec(in_tile, lambda m, f: (m, f)), pl.BlockSpec(out_tile, lambda m, f: (f, m))]
        out_shape += [jax.ShapeDtypeStruct((D, 2 * D_FF), BF16), jax.ShapeDtypeStruct((D_FF, D), BF16)]
    return pl.pallas_call(
        functools.partial(_ffn_kernel, mod_base=3 * sub, convert_next=convert_next is not None),
        grid=(nm, nf),
        in_specs=in_specs,
        out_specs=out_specs,
        out_shape=out_shape,
        scratch_shapes=[pltpu.VMEM((FFN_ROW_TILE, D), BF16)],
        compiler_params=_params("parallel", "arbitrary"),
        name="ffn",
    )(*args)


def _inproj_kernel(x_ref, mod_ref, w_ref, o_ref, u_ref):
    @pl.when(pl.program_id(1) == 0)
    def _():
        shift = mod_ref[3:4, :]
        scale = mod_ref[4:5, :]
        u_ref[...] = (x_ref[...] * (1.0 + scale) + shift).astype(BF16)

    o_ref[...] = _dot(u_ref[...], w_ref[...]).astype(o_ref.dtype)


def _inproj(x, mod, who0, rows_per_mod, w, l):
    M, D = x.shape
    N = w.shape[2]
    return pl.pallas_call(
        _inproj_kernel,
        grid=(M // PROJ_ROW_TILE, N // PROJ_TN),
        in_specs=[
            pl.BlockSpec((PROJ_ROW_TILE, D), lambda m, n: (m, 0)),
            _mod_spec(l, who0, rows_per_mod, 2, PROJ_ROW_TILE),
            pl.BlockSpec((None, D, PROJ_TN), lambda m, n: (l, 0, n)),
        ],
        out_specs=pl.BlockSpec((PROJ_ROW_TILE, PROJ_TN), lambda m, n: (m, n)),
        out_shape=jax.ShapeDtypeStruct((M, N), BF16),
        scratch_shapes=[pltpu.VMEM((PROJ_ROW_TILE, D), BF16)],
        compiler_params=_params("parallel", "arbitrary"),
        name="inproj",
    )(x, mod, w)


def _attn_kernel(*refs, rope, cache_len, emit_kv, tq):
    it = iter(refs)
    q_ref, k_ref, v_ref, qn_ref, kn_ref = (next(it) for _ in range(5))
    if rope:
        cosq_ref, sinq_ref, cosk_ref, sink_ref = (next(it) for _ in range(4))
    if cache_len:
        ck_ref, cv_ref = next(it), next(it)
    o_ref = next(it)
    if emit_kv:
        newk_ref, newv_ref = next(it), next(it)
    kb_ref, vt_ref = next(it), next(it)
    T = kb_ref.shape[0]
    chunks = ([(0, cache_len)] if cache_len else []) + [
        (o, min(ATT_KV_CHUNK, T - o)) for o in range(cache_len, T, ATT_KV_CHUNK)]

    @pl.when(pl.program_id(2) == 0)
    def _():
        k = _rms_norm(k_ref[...].astype(F32), kn_ref[...])
        v = v_ref[...].astype(F32)
        if emit_kv:
            newk_ref[...] = k
            newv_ref[...] = v
        if rope:
            k = _rope(k, cosk_ref[...], sink_ref[...])
        if cache_len:
            k = jnp.concatenate([ck_ref[...], k], axis=0)
            v = jnp.concatenate([cv_ref[...], v], axis=0)
        kb_ref[...] = k.astype(BF16)
        for o, n in chunks:
            vt_ref[:, o:o + n] = v[o:o + n, :].T.astype(BF16)

    q = q_ref[...].astype(F32)
    heads = []
    for g in range(ATT_GROUP):
        qg = _rms_norm(q[:, g * LANES:(g + 1) * LANES], qn_ref[...])
        if rope:
            qg = _rope(qg, cosq_ref[...], sinq_ref[...])
        heads.append(qg.T)
    q_t = jnp.concatenate(heads, axis=1).astype(BF16)

    c2 = (ATT_HEAD_DIM ** -0.5) * math.log2(math.e)
    m = denom = acc = None
    for o, n in chunks:
        s = _dot(kb_ref[o:o + n, :], q_t)
        m_c = jnp.max(s, axis=0, keepdims=True)
        m = m_c if m is None else jnp.maximum(m, m_c)
    for o, n in chunks:
        s = _dot(kb_ref[o:o + n, :], q_t)
        p = jnp.exp2((s - m) * c2)
        d_c = jnp.sum(p, axis=0, keepdims=True)
        a_c = _dot(vt_ref[:, o:o + n], p.astype(BF16))
        denom = d_c if denom is None else denom + d_c
        acc = a_c if acc is None else acc + a_c
    o_t = acc * (1.0 / denom)
    for g in range(ATT_GROUP):
        o_ref[:, g * LANES:(g + 1) * LANES] = o_t[:, g * tq:(g + 1) * tq].T.astype(o_ref.dtype)


def _attention(P, B, S, q_norm, k_norm, l, rope_tabs, cache, emit_kv, tq):
    nq = S // tq
    cache_len = cache[0].shape[3] if cache is not None else 0
    T = cache_len + S
    gw = ATT_GROUP * ATT_HEAD_DIM
    norm_spec = pl.BlockSpec((None, 1, LANES), lambda b, j, i: (l, 0, 0))
    in_specs = [
        pl.BlockSpec((tq, gw), lambda b, j, i: (b * nq + i, j)),
        pl.BlockSpec((S, LANES), lambda b, j, i: (b, COL_AK + j)),
        pl.BlockSpec((S, LANES), lambda b, j, i: (b, COL_AV + j)),
        norm_spec, norm_spec,
    ]
    args = [P, P, P, q_norm, k_norm]
    if rope_tabs is not None:
        cos, sin = rope_tabs
        in_specs += [
            pl.BlockSpec((tq, LANES), lambda b, j, i: (i, 0)),
            pl.BlockSpec((tq, LANES), lambda b, j, i: (i, 0)),
            pl.BlockSpec((S, LANES), lambda b, j, i: (0, 0)),
            pl.BlockSpec((S, LANES), lambda b, j, i: (0, 0)),
        ]
        args += [cos, sin, cos, sin]
    if cache is not None:
        spec = pl.BlockSpec((None, None, None, cache_len, LANES), lambda b, j, i: (b, l, j, 0, 0))
        in_specs += [spec, spec]
        args += list(cache)
    out_specs = [pl.BlockSpec((tq, gw), lambda b, j, i: (b * nq + i, j))]
    out_shape = [jax.ShapeDtypeStruct((B * S, ATT_WIDTH), BF16)]
    if emit_kv:
        kv_spec = pl.BlockSpec((S, LANES), lambda b, j, i: (b, j))
        out_specs += [kv_spec, kv_spec]
        out_shape += [jax.ShapeDtypeStruct((B * S, ATT_KV_HEADS * ATT_HEAD_DIM), F32)] * 2
    return pl.pallas_call(
        functools.partial(_attn_kernel, rope=rope_tabs is not None, cache_len=cache_len,
                          emit_kv=emit_kv, tq=tq),
        grid=(B, ATT_KV_HEADS, nq),
        in_specs=in_specs,
        out_specs=out_specs,
        out_shape=out_shape,
        scratch_shapes=[pltpu.VMEM((T, LANES), BF16), pltpu.VMEM((LANES, T), BF16)],
        compiler_params=_params("parallel", "parallel", "arbitrary"),
        name="attention",
    )(*args)


def _ret_kernel(*refs, l, rope, has_s0, emit_state, S):
    it = iter(refs)
    lg_ref, q_ref, k_ref, v_ref, g_ref = (next(it) for _ in range(5))
    if rope:
        cos_ref, sin_ref = next(it), next(it)
    if has_s0:
        s0_ref = next(it)
    o_ref = next(it)
    if emit_state:
        st_ref = next(it)
    qb_ref, kb_ref, qd_ref, kd_ref, work_ref, sb_ref = (next(it) for _ in range(6))

    h = pl.program_id(1)
    C = RET_CHUNK
    n = S // C
    lg_f = lg_ref[l, 0, h]
    lg_b = lg_ref[l, 1, h]
    q = q_ref[...].astype(F32)
    k = k_ref[...].astype(F32) * (RET_DK ** -0.5)
    if rope:
        q = _rope(q, cos_ref[...], sin_ref[...])
        k = _rope(k, cos_ref[...], sin_ref[...])

    t = lax.broadcasted_iota(jnp.int32, (C, LANES), 0).astype(F32)

    def scaled(x, expo):
        return (x.reshape(n, C, LANES) * jnp.exp(expo)[None]).reshape(S, LANES).astype(BF16)

    qb_ref[...] = q.astype(BF16)
    kb_ref[...] = k.astype(BF16)
    qd_ref[:, 0:RET_DK] = scaled(q, (t + 1.0) * lg_f)
    qd_ref[:, RET_DK:] = scaled(q, (C - t) * lg_b)
    kd_ref[:, 0:RET_DK] = scaled(k, (C - 1.0 - t) * lg_f)
    kd_ref[:, RET_DK:] = scaled(k, t * lg_b)

    ii = lax.broadcasted_iota(jnp.int32, (C, C), 0).astype(F32)
    jj = lax.broadcasted_iota(jnp.int32, (C, C), 1).astype(F32)
    dmat = (jnp.where(ii >= jj, jnp.exp(jnp.maximum(ii - jj, 0.0) * lg_f), 0.0)
            + jnp.where(jj >= ii, jnp.exp(jnp.maximum(jj - ii, 0.0) * lg_b), 0.0))
    decay_f = jnp.exp(jnp.full((RET_DK, RET_DV), C * lg_f, F32))
    decay_b = jnp.exp(jnp.full((RET_DK, RET_DV), C * lg_b, F32))
    if has_s0:
        state0 = (s0_ref[0], s0_ref[1])
    else:
        state0 = (jnp.zeros((RET_DK, RET_DV), F32), jnp.zeros((RET_DK, RET_DV), F32))

    def rows(c):
        return pl.ds(pl.multiple_of(c * C, C), C)

    def state_updates(c, carry):
        sl = rows(c)
        update = _dot_tn(kd_ref[sl, :], v_ref[sl, :])
        work_ref[0, sl, :] = update[0:RET_DK, :]
        work_ref[1, sl, :] = update[RET_DK:, :]
        return carry

    lax.fori_loop(0, n, state_updates, 0, unroll=LOOP_UNROLL)

    def scan(c, carry):
        state_f, state_b = carry
        cb = n - 1 - c
        sb_ref[c, 0:RET_DK, :] = state_f.astype(BF16)
        sb_ref[cb, RET_DK:, :] = state_b.astype(BF16)
        return (decay_f * state_f + work_ref[0, rows(c), :], decay_b * state_b + work_ref[1, rows(cb), :])

    state_f, state_b = lax.fori_loop(0, n, scan, state0, unroll=LOOP_UNROLL)
    if emit_state:
        st_ref[0] = state_f
        st_ref[1] = state_b

    U = min(SCAN_GROUP, n)

    def outputs(j, carry):
        chunks = [j * U + u for u in range(U)]
        scores = [(_dot_nt(qb_ref[rows(c), :], kb_ref[rows(c), :]) * dmat).astype(BF16) for c in chunks]
        cross = [_dot(qd_ref[rows(c), :], sb_ref[c]) for c in chunks]
        for c, sc, x in zip(chunks, scores, cross):
            work_ref[0, rows(c), :] = _dot(sc, v_ref[rows(c), :]) + x
        return carry

    lax.fori_loop(0, n // U, outputs, 0)

    y = work_ref[0]
    mu = jnp.mean(y, axis=-1, keepdims=True)
    yc = y - mu
    var = jnp.mean(yc * yc, axis=-1, keepdims=True)
    gate = _silu(g_ref[...].astype(F32))
    o_ref[...] = (yc * lax.rsqrt(var + LN_EPS) * gate).astype(o_ref.dtype)


def _retention(P, B, S, log_decay, l, rope_tabs, s0, emit_state):
    def col(c0):
        return pl.BlockSpec((S, LANES), lambda b, h: (b, c0 + h))

    in_specs = [pl.BlockSpec(memory_space=pltpu.SMEM), col(COL_RQ), col(COL_RK), col(COL_RV), col(COL_RG)]
    args = [log_decay, P, P, P, P]
    if rope_tabs is not None:
        tab = pl.BlockSpec((S, LANES), lambda b, h: (0, 0))
        in_specs += [tab, tab]
        args += list(rope_tabs)
    if s0 is not None:
        in_specs.append(pl.BlockSpec((None, None, 2, None, RET_DK, RET_DV), lambda b, h: (b, l, 0, h, 0, 0)))
        args.append(s0)
    out_specs = [pl.BlockSpec((S, LANES), lambda b, h: (b, h))]
    out_shape = [jax.ShapeDtypeStruct((B * S, RET_WIDTH), BF16)]
    if emit_state:
        out_specs.append(pl.BlockSpec((None, 2, None, RET_DK, RET_DV), lambda b, h: (b, 0, h, 0, 0)))
        out_shape.append(jax.ShapeDtypeStruct((B, 2, RET_HEADS, RET_DK, RET_DV), F32))
    return pl.pallas_call(
        functools.partial(_ret_kernel, l=l, rope=rope_tabs is not None, has_s0=s0 is not None,
                          emit_state=emit_state, S=S),
        grid=(B, RET_HEADS),
        in_specs=in_specs,
        out_specs=out_specs,
        out_shape=out_shape,
        scratch_shapes=[pltpu.VMEM((S, RET_DK), BF16), pltpu.VMEM((S, RET_DK), BF16),
                        pltpu.VMEM((S, 2 * RET_DK), BF16), pltpu.VMEM((S, 2 * RET_DK), BF16),
                        pltpu.VMEM((2, S, RET_DV), F32),
                        pltpu.VMEM((S // RET_CHUNK, 2 * RET_DK, RET_DV), BF16)],
        compiler_params=_params("parallel", "parallel"),
        name="retention",
    )(*args)


def _gla_kernel(*refs, pairs, **static):
    for pp in range(pairs):
        _gla_pair(pp, *refs, **static)


def _gla_pair(pp, *refs, has_s0, emit_state, S):
    it = iter(refs)
    q_ref, k_ref, v_ref, gr_ref, ga_ref, wa_ref, ba_ref, gn_ref = (next(it) for _ in range(8))
    if has_s0:
        s0_ref = next(it).at[:, pp]
    o_ref = next(it)
    if emit_state:
        st_ref = next(it).at[:, pp]
    la_ref, qt_ref, kcat_ref, dec_ref, work_ref, sb_ref = (next(it) for _ in range(6))
    narrow = slice(pp * LANES, (pp + 1) * LANES)
    wide = slice(pp * 2 * GLA_DV, (pp + 1) * 2 * GLA_DV)
    q_ref, k_ref = q_ref.at[:, narrow], k_ref.at[:, narrow]
    v_ref, gr_ref, o_ref = v_ref.at[:, wide], gr_ref.at[:, wide], o_ref.at[:, wide]
    wa_ref, ba_ref = wa_ref.at[:, :, narrow], ba_ref.at[:, :, narrow]
    C = GLA_CHUNK
    n = S // C
    U = min(SCAN_GROUP, n)
    ga = ga_ref[...]
    for d in range(2):
        pre = _dot(ga, wa_ref[d]) + ba_ref[d]
        la_ref[:, d * LANES:(d + 1) * LANES] = (
            (jnp.minimum(pre, 0.0) - jnp.log1p(jnp.exp(-jnp.abs(pre)))) * (1.0 / GLA_TAU))

    ii = lax.broadcasted_iota(jnp.int32, (C, C), 0)
    jj = lax.broadcasted_iota(jnp.int32, (C, C), 1)
    lower_b = jnp.where(jj <= ii, 1.0, 0.0).astype(BF16)
    mask_f = jnp.concatenate([jnp.where(jj <= ii, 1.0, 0.0)] * 2, axis=1)
    mask_b = jnp.concatenate([jnp.where(jj >= ii, 1.0, 0.0)] * 2, axis=1)
    lane = lax.broadcasted_iota(jnp.int32, (C, LANES), 1)
    head0_lanes = lane < GLA_DK
    lane_sq = lax.broadcasted_iota(jnp.int32, (LANES, LANES), 1)
    row_sq = lax.broadcasted_iota(jnp.int32, (LANES, LANES), 0)
    lane_wide = lax.broadcasted_iota(jnp.int32, (C, 2 * GLA_DV), 1)

    def rows(c):
        return pl.ds(pl.multiple_of(c * C, C), C)

    def update_rows(c):
        return pl.ds(pl.multiple_of(c * GLA_DV, GLA_DV), GLA_DV)

    def prepare(j, carry):
        chunks = [j * U + u for u in range(U)]
        cums = []
        for c in chunks:
            la = la_ref[rows(c), :]
            hi = la.astype(BF16)
            r1 = la - hi.astype(F32)
            mid = r1.astype(BF16)
            lo = (r1 - mid.astype(F32)).astype(BF16)
            cums.append((la, _dot(lower_b, hi) + _dot(lower_b, mid) + _dot(lower_b, lo)))
        scaled_keys = []
        for c, (la, cum) in zip(chunks, cums):
            sl = rows(c)
            qc = q_ref[sl, :].astype(F32) * (GLA_DK ** -0.5)
            kc = k_ref[sl, :].astype(F32)
            b_f = cum[:, 0:LANES]
            cum_b = cum[:, LANES:]
            b_b = cum_b[C - 1:C, :] - cum_b + la[:, LANES:]
            end_f = b_f[C - 1:C, :]
            end_b = b_b[0:1, :]
            qt_ref[sl, 0:LANES] = (qc * jnp.exp(b_f)).astype(BF16)
            qt_ref[sl, LANES:] = (qc * jnp.exp(b_b)).astype(BF16)
            for d, b in ((0, b_f), (1, b_b)):
                k_t = kc * jnp.exp(-b)
                kcat_ref[d, c, 0:C, :] = jnp.where(head0_lanes, k_t, 0.0).astype(BF16)
                kcat_ref[d, c, C:, :] = jnp.where(head0_lanes, 0.0, k_t).astype(BF16)
            dec_ref[0, c] = jnp.exp(end_f)
            dec_ref[1, c] = jnp.exp(end_b)
            scaled_keys.append(jnp.concatenate(
                [kc * jnp.exp(end_f - b_f), kc * jnp.exp(end_b - b_b)], axis=1).astype(BF16))
        for c, k_s in zip(chunks, scaled_keys):
            kv = _dot_tn(v_ref[rows(c), :], k_s)
            for d in range(2):
                cols = kv[:, d * LANES:(d + 1) * LANES]
                work_ref[d, update_rows(c), :] = jnp.where(lane_sq < GLA_DK, cols[0:GLA_DV, :], cols[GLA_DV:, :])
        return carry

    lax.fori_loop(0, n // U, prepare, 0)

    if has_s0:
        state0 = (s0_ref[0].T, s0_ref[1].T)
    else:
        state0 = (jnp.zeros((GLA_DV, LANES), F32), jnp.zeros((GLA_DV, LANES), F32))

    def scan(c, carry):
        state_f, state_b = carry
        cb = n - 1 - c
        for d, cc, state in ((0, c, state_f), (1, cb, state_b)):
            s2 = state.T
            r0 = d * LANES
            sb_ref[cc, r0:r0 + LANES, 0:GLA_DV] = jnp.where(row_sq < GLA_DK, s2, 0.0).astype(BF16)
            sb_ref[cc, r0:r0 + LANES, GLA_DV:] = jnp.where(row_sq < GLA_DK, 0.0, s2).astype(BF16)
        return (state_f * dec_ref[0, c] + work_ref[0, update_rows(c), :],
                state_b * dec_ref[1, cb] + work_ref[1, update_rows(cb), :])

    states = lax.fori_loop(0, n, scan, state0, unroll=LOOP_UNROLL)
    if emit_state:
        st_ref[0] = states[0].T
        st_ref[1] = states[1].T

    def outputs(j, carry):
        chunks = [j * U + u for u in range(U)]
        scores, cross = [], []
        for c in chunks:
            q_t = qt_ref[rows(c), :]
            att = (_dot_nt(q_t[:, 0:LANES], kcat_ref[0, c]) * mask_f
                   + _dot_nt(q_t[:, LANES:], kcat_ref[1, c]) * mask_b)
            scores.append(att.astype(BF16))
            cross.append(_dot(q_t, sb_ref[c]))
        for c, att, x in zip(chunks, scores, cross):
            v_c = v_ref[rows(c), :]
            zeros = jnp.zeros_like(v_c)
            v_bd = jnp.concatenate([jnp.where(lane_wide < GLA_DV, v_c, zeros),
                                    jnp.where(lane_wide < GLA_DV, zeros, v_c)], axis=0)
            y = _dot(att, v_bd) + x
            for i in range(2):
                work_ref[0, pl.ds(pl.multiple_of(i * S + c * C, C), C), :] = y[:, i * GLA_DV:(i + 1) * GLA_DV]
        return carry

    lax.fori_loop(0, n // U, outputs, 0)

    for i in range(2):
        y = _rms_norm(work_ref[0, i * S:(i + 1) * S, :], gn_ref[...])
        gate = _silu(gr_ref[:, i * GLA_DV:(i + 1) * GLA_DV].astype(F32))
        o_ref[:, i * GLA_DV:(i + 1) * GLA_DV] = (y * gate).astype(o_ref.dtype)


def _gla(P, B, S, wa, ba, gla_norm, l, s0, emit_state, pairs):
    nw = pairs * LANES
    pw = pairs * 2 * GLA_DV
    assert COL_GQ % pairs == 0 and COL_GK % pairs == 0
    in_specs = [
        pl.BlockSpec((S, nw), lambda b, p: (b, COL_GQ // pairs + p)),
        pl.BlockSpec((S, nw), lambda b, p: (b, COL_GK // pairs + p)),
        pl.BlockSpec((S, pw), lambda b, p: (b, COL_GV // (2 * pairs) + p)),
        pl.BlockSpec((S, pw), lambda b, p: (b, COL_GR // (2 * pairs) + p)),
        pl.BlockSpec((S, LANES), lambda b, p: (b, COL_GA)),
        pl.BlockSpec((None, 2, LANES, nw), lambda b, p: (l, 0, 0, p)),
        pl.BlockSpec((None, 2, 1, nw), lambda b, p: (l, 0, 0, p)),
        pl.BlockSpec((None, 1, GLA_DV), lambda b, p: (l, 0, 0)),
    ]
    args = [P, P, P, P, P, wa, ba, gla_norm]
    if s0 is not None:
        in_specs.append(pl.BlockSpec((None, None, 2, pairs, LANES, GLA_DV), lambda b, p: (b, l, 0, p, 0, 0)))
        args.append(s0)
    out_specs = [pl.BlockSpec((S, pw), lambda b, p: (b, p))]
    out_shape = [jax.ShapeDtypeStruct((B * S, GLA_WIDTH), BF16)]
    if emit_state:
        out_specs.append(pl.BlockSpec((None, 2, pairs, LANES, GLA_DV), lambda b, p: (b, 0, p, 0, 0)))
        out_shape.append(jax.ShapeDtypeStruct((B, 2, GLA_PAIRS, LANES, GLA_DV), F32))
    return pl.pallas_call(
        functools.partial(_gla_kernel, pairs=pairs, has_s0=s0 is not None, emit_state=emit_state, S=S),
        grid=(B, GLA_PAIRS // pairs),
        in_specs=in_specs,
        out_specs=out_specs,
        out_shape=out_shape,
        scratch_shapes=[pltpu.VMEM((S, 2 * LANES), F32),
                        pltpu.VMEM((S, 2 * LANES), BF16),
                        pltpu.VMEM((2, S // GLA_CHUNK, 2 * GLA_CHUNK, LANES), BF16),
                        pltpu.VMEM((2, S // GLA_CHUNK, 1, LANES), F32),
                        pltpu.VMEM((2, 2 * S, GLA_DV), F32),
                        pltpu.VMEM((S // GLA_CHUNK, 2 * LANES, 2 * GLA_DV), BF16)],
        compiler_params=_params("parallel", "parallel"),
        name="gla",
    )(*args)


def _outproj_kernel(x_ref, mod_ref, a_ref, r_ref, gl_ref, wa_ref, wr_ref, wg_ref, g_ref, b_ref, o_ref):
    rows = x_ref.shape[0] // OUTPROJ_SPLIT
    mixed = []
    for i in range(OUTPROJ_SPLIT):
        sl = slice(i * rows, (i + 1) * rows)
        mixed.append(_dot(a_ref[sl, :], wa_ref[...]) + _dot(r_ref[sl, :], wr_ref[...])
                     + _dot(gl_ref[sl, :], wg_ref[...]))
    for i in range(OUTPROJ_SPLIT):
        sl = slice(i * rows, (i + 1) * rows)
        y = DEEPNORM_ALPHA * x_ref[sl, :] + mod_ref[5:6, :] * mixed[i]
        o_ref[sl, :] = _layer_norm(y, g_ref[...], b_ref[...])


def _outproj(x, mod, who0, rows_per_mod, att, ret, gla, w, ln_g, ln_b, l):
    M, D = x.shape
    return pl.pallas_call(
        _outproj_kernel,
        grid=(M // ROW_TILE,),
        in_specs=[
            pl.BlockSpec((ROW_TILE, D), lambda m: (m, 0)),
            _mod_spec(l, who0, rows_per_mod, 1),
            pl.BlockSpec((ROW_TILE, ATT_WIDTH), lambda m: (m, 0)),
            pl.BlockSpec((ROW_TILE, RET_WIDTH), lambda m: (m, 0)),
            pl.BlockSpec((ROW_TILE, GLA_WIDTH), lambda m: (m, 0)),
            pl.BlockSpec((None, ATT_WIDTH, D), lambda m: (l, 0, 0)),
            pl.BlockSpec((None, RET_WIDTH, D), lambda m: (l, ATT_WIDTH // RET_WIDTH, 0)),
            pl.BlockSpec((None, GLA_WIDTH, D), lambda m: (l, (ATT_WIDTH + RET_WIDTH) // GLA_WIDTH, 0)),
            pl.BlockSpec((None, None, 1, D), lambda m: (l, 1, 0, 0)),
            pl.BlockSpec((None, None, 1, D), lambda m: (l, 1, 0, 0)),
        ],
        out_specs=pl.BlockSpec((ROW_TILE, D), lambda m: (m, 0)),
        out_shape=jax.ShapeDtypeStruct((M, D), F32),
        compiler_params=_params("parallel"),
        name="outproj",
    )(x, mod, att, ret, gla, w, w, w, ln_g, ln_b)


def _trunk_layer(x, B, S, l, who0, w, rope_tabs, ctx, attn_tq):
    is_context = ctx is None
    rows_per_mod = x.shape[0] if is_context else S
    mod = w["mod"]

    def ffn(x, half):
        nxt = (l, 1) if half == 0 else (l + 1, 0)
        convert = None
        if nxt[0] < DEPTH and nxt not in w["ffn_bf16"]:
            convert = (w["ffn_w_in"], w["ffn_w_out"], nxt[0], nxt[1])
        outs = _ffn(x, mod, who0, rows_per_mod, *w["ffn_bf16"][(l, half)], w["ln_g"], w["ln_b"], l, half, convert)
        if convert is not None:
            w["ffn_bf16"][nxt] = (outs[1], outs[2])
        return outs[0]

    x = ffn(x, 0)
    P = _inproj(x, mod, who0, rows_per_mod, w["mix_w_in"], l)
    cache = None if is_context else (ctx[0], ctx[1])
    s_ret0 = None if is_context else ctx[2]
    s_gla0 = None if is_context else ctx[3]
    att_out = _attention(P, B, S, w["q_norm"], w["k_norm"], l, rope_tabs, cache, is_context, attn_tq)
    ret_out = _retention(P, B, S, w["log_decay"], l, rope_tabs, s_ret0, is_context)
    gla_out = _gla(P, B, S, w["gla_wa"], w["gla_ba"], w["gla_norm"], l, s_gla0, is_context,
                   GLA_PAIRS if is_context else 1)
    x = _outproj(x, mod, who0, rows_per_mod, att_out[0], ret_out[0], gla_out[0],
                 w["mix_w_out"], w["ln_g"], w["ln_b"], l)
    x = ffn(x, 1)
    new_ctx = (att_out[1], att_out[2], ret_out[1], gla_out[1]) if is_context else None
    return x, new_ctx


def _rope_tables(rows):
    row = jnp.repeat(jnp.arange(rows, dtype=F32), GRID_W)
    col = jnp.tile(jnp.arange(GRID_W, dtype=F32), rows)
    n_freq = ATT_HEAD_DIM // 4
    inv = ROPE_THETA ** (-jnp.arange(n_freq, dtype=F32) / n_freq)
    ang = jnp.concatenate([row[:, None] * inv, col[:, None] * inv], axis=-1)
    cos, sin = jnp.cos(ang), jnp.sin(ang)
    cos_full = jnp.repeat(cos, 2, axis=-1)
    sin_signed = jnp.stack([-sin, sin], axis=-1).reshape(ang.shape[0], ATT_HEAD_DIM)
    return cos_full, sin_signed


def kernel(x_prompt, x_sample, c, cache_attn_k, cache_attn_v, state_ret, state_gla, c_ctx,
           w_mod, b_mod, ln_g, ln_b, ffn_w_in, ffn_w_out, mix_w_in, mix_w_out,
           att_q_norm, att_k_norm, ret_log_decay, gla_w_a2, gla_b_a, gla_norm):
    B_ctx, S_ctx, D = x_prompt.shape
    B_lat, S_lat, _ = x_sample.shape

    gla_wa = jnp.zeros((DEPTH, 2, LANES, GLA_HEADS * GLA_DK), BF16)
    for d in range(2):
        gla_wa = gla_wa.at[:, d, d * GLA_GATE_RANK:(d + 1) * GLA_GATE_RANK, :].set(gla_w_a2[:, d].astype(BF16))

    cvec = jnp.concatenate([c_ctx[None, :], c, jnp.zeros((MOD_ROWS - 1 - B_lat, D), F32)], axis=0)
    mod = _modulation(cvec, w_mod, b_mod[:, None, :]).reshape(DEPTH, MOD_ROWS, N_MOD, D)

    w = dict(
        mod=mod,
        ffn_w_in=ffn_w_in, ffn_w_out=ffn_w_out,
        ffn_bf16={(0, 0): (ffn_w_in[0, 0].astype(BF16), ffn_w_out[0, 0].astype(BF16))},
        mix_w_in=jnp.pad(mix_w_in.astype(BF16), ((0, 0), (0, 0), (0, PROJ_COLS_PADDED - PROJ_COLS))),
        mix_w_out=mix_w_out.astype(BF16),
        ln_g=ln_g[:, :, None, :], ln_b=ln_b[:, :, None, :],
        q_norm=att_q_norm[:, None, :], k_norm=att_k_norm[:, None, :],
        log_decay=ret_log_decay, gla_wa=gla_wa, gla_ba=gla_b_a[:, :, None, :],
        gla_norm=gla_norm[:, None, :])

    h = x_prompt.reshape(B_ctx * S_ctx, D)
    ks_l, vs_l, sr_l, sg_l = [], [], [], []
    for l in range(DEPTH):
        h, (k_l, v_l, s_r, s_g) = _trunk_layer(h, B_ctx, S_ctx, l, 0, w, None, None, S_ctx)
        ks_l.append(k_l.reshape(B_ctx, S_ctx, ATT_KV_HEADS, ATT_HEAD_DIM))
        vs_l.append(v_l.reshape(B_ctx, S_ctx, ATT_KV_HEADS, ATT_HEAD_DIM))
        sr_l.append(s_r)
        sg_l.append(s_g.reshape(B_ctx, 2, GLA_HEADS, GLA_DK, GLA_DV))
    y_prompt = h.reshape(B_ctx, S_ctx, D)

    rope_tabs = _rope_tables(S_lat // GRID_W)
    ctx = (cache_attn_k.transpose(0, 1, 3, 2, 4), cache_attn_v.transpose(0, 1, 3, 2, 4), state_ret,
           state_gla.reshape(B_lat, DEPTH, 2, GLA_PAIRS, LANES, GLA_DV))
    g = x_sample.reshape(B_lat * S_lat, D)
    for l in range(DEPTH):
        g, _ = _trunk_layer(g, B_lat, S_lat, l, 1, w, rope_tabs, ctx, 256)
    y_sample = g.reshape(B_lat, S_lat, D)

    return (y_prompt, y_sample, jnp.stack(ks_l, axis=1), jnp.stack(vs_l, axis=1),
            jnp.stack(sr_l, axis=1), jnp.stack(sg_l, axis=1))
```

```python
import functools
import math

import jax
import jax.numpy as jnp
from jax import lax
from jax.experimental import pallas as pl
from jax.experimental.pallas import tpu as pltpu

F32 = jnp.float32
BF16 = jnp.bfloat16

D_MODEL = 2048
DEPTH = 2
GRID_W = 64
ATT_HEAD_DIM = 128
ATT_HEADS = 8
ATT_KV_HEADS = 2
ATT_GROUP = ATT_HEADS // ATT_KV_HEADS
ATT_WIDTH = ATT_HEADS * ATT_HEAD_DIM
ATT_KV_CHUNK = 512
ROPE_THETA = 10000.0
RET_DK = 128
RET_DV = 128
RET_HEADS = 4
RET_WIDTH = RET_HEADS * RET_DV
RET_CHUNK = 128
GLA_DK = 64
GLA_DV = 128
GLA_HEADS = 4
GLA_PAIRS = GLA_HEADS // 2
GLA_WIDTH = GLA_HEADS * GLA_DV
GLA_GATE_RANK = 16
GLA_TAU = 16.0
GLA_CHUNK = 64
D_FF = 5632
N_MOD = 9
MACARON_WEIGHT = 0.5
DEEPNORM_ALPHA = (2 * DEPTH) ** 0.25
LN_EPS = 1e-5
RMS_EPS = 1e-6

LANES = 128
PROJ_COLS = 5152
PROJ_COLS_PADDED = 5376
PROJ_TN = 1792
COL_AQ, COL_AK, COL_AV = 0, 8, 10
COL_RQ, COL_RK, COL_RV, COL_RG = 12, 16, 20, 24
COL_GQ, COL_GK, COL_GV, COL_GR, COL_GA = 28, 30, 32, 36, 40

VMEM_LIMIT_BYTES = 56 * 1024 * 1024
ROW_TILE = 512
PROJ_ROW_TILE = 1024
FF_TILE = 512
FFN_ROW_TILE = 512
FFN_SPLIT = 2
MOD_TN = 2048
OUTPROJ_SPLIT = 4
LOOP_UNROLL = 8
SCAN_GROUP = 8
MOD_ROWS = 8


def _params(*sem):
    return pltpu.CompilerParams(dimension_semantics=sem, vmem_limit_bytes=VMEM_LIMIT_BYTES)


def _dot(a, b):
    return jnp.dot(a, b, preferred_element_type=F32)


def _dot_nt(a, b):
    return lax.dot_general(a, b, (((1,), (1,)), ((), ())), preferred_element_type=F32)


def _dot_tn(a, b):
    return lax.dot_general(a, b, (((0,), (0,)), ((), ())), preferred_element_type=F32)


def _silu(x):
    return x * jax.nn.sigmoid(x)


def _layer_norm(y, g, b):
    mu = jnp.mean(y, axis=-1, keepdims=True)
    d = y - mu
    var = jnp.mean(d * d, axis=-1, keepdims=True)
    return d * lax.rsqrt(var + LN_EPS) * g + b


def _rms_norm(x, g):
    return x * lax.rsqrt(jnp.mean(x * x, axis=-1, keepdims=True) + RMS_EPS) * g


def _rope(x, cos, sin_signed):
    lane = lax.broadcasted_iota(jnp.int32, x.shape, 1)
    partner = jnp.where((lane & 1) == 0, pltpu.roll(x, LANES - 1, 1), pltpu.roll(x, 1, 1))
    return x * cos + partner * sin_signed


def _mod_spec(l, who0, rows_per_mod, ndim_grid, row_tile=ROW_TILE):
    if ndim_grid == 1:
        return pl.BlockSpec((None, None, N_MOD, D_MODEL),
                            lambda m: (l, who0 + (m * row_tile) // rows_per_mod, 0, 0))
    return pl.BlockSpec((None, None, N_MOD, D_MODEL),
                        lambda m, n: (l, who0 + (m * row_tile) // rows_per_mod, 0, 0))


def _mod_kernel(c_ref, w_ref, b_ref, o_ref):
    a = _silu(c_ref[...]).astype(BF16)
    o_ref[...] = _dot(a, w_ref[...].astype(BF16)) + b_ref[...]


def _modulation(cvec, w_mod, b_mod):
    L, D, N = w_mod.shape
    return pl.pallas_call(
        _mod_kernel,
        grid=(L, N // MOD_TN),
        in_specs=[
            pl.BlockSpec((MOD_ROWS, D), lambda l, n: (0, 0)),
            pl.BlockSpec((None, D, MOD_TN), lambda l, n: (l, 0, n)),
            pl.BlockSpec((None, 1, MOD_TN), lambda l, n: (l, 0, n)),
        ],
        out_specs=pl.BlockSpec((None, MOD_ROWS, MOD_TN), lambda l, n: (l, 0, n)),
        out_shape=jax.ShapeDtypeStruct((L, MOD_ROWS, N), F32),
        compiler_params=_params("parallel", "parallel"),
        name="modulation",
    )(cvec, w_mod, b_mod)


def _ffn_kernel(*refs, mod_base, convert_next):
    if convert_next:
        (x_ref, mod_ref, wg_ref, wu_ref, wo_ref, g_ref, b_ref, next_in_ref, next_out_ref,
         o_ref, next_in_b_ref, next_out_b_ref, u_ref) = refs
        next_in_b_ref[...] = next_in_ref[...].astype(BF16)
        next_out_b_ref[...] = next_out_ref[...].astype(BF16)
    else:
        x_ref, mod_ref, wg_ref, wu_ref, wo_ref, g_ref, b_ref, o_ref, u_ref = refs
    f = pl.program_id(1)
    last = pl.num_programs(1) - 1

    def step(sl, first, final):
        if first:
            shift = mod_ref[mod_base:mod_base + 1, :]
            scale = mod_ref[mod_base + 1:mod_base + 2, :]
            u = (x_ref[sl, :] * (1.0 + scale) + shift).astype(BF16)
            u_ref[sl, :] = u
        else:
            u = u_ref[sl, :]
        gate = _dot(u, wg_ref[...])
        up = _dot(u, wu_ref[...])
        act = (_silu(gate) * up).astype(BF16)
        acc = _dot(act, wo_ref[...])
        if not first:
            acc = o_ref[sl, :] + acc
        if final:
            g3 = mod_ref[mod_base + 2:mod_base + 3, :]
            y = DEEPNORM_ALPHA * x_ref[sl, :] + MACARON_WEIGHT * (g3 * acc)
            acc = _layer_norm(y, g_ref[...], b_ref[...])
        o_ref[sl, :] = acc

    rows = x_ref.shape[0] // FFN_SPLIT
    subtiles = [slice(i * rows, (i + 1) * rows) for i in range(FFN_SPLIT)]

    @pl.when(f == 0)
    def _():
        for sl in subtiles:
            step(sl, True, False)

    @pl.when(jnp.logical_and(f > 0, f < last))
    def _():
        step(slice(None), False, False)

    @pl.when(f == last)
    def _():
        for sl in subtiles:
            step(sl, False, True)


def _ffn(x, mod, who0, rows_per_mod, w_in_b, w_out_b, ln_g, ln_b, l, half, convert_next=None):
    M, D = x.shape
    nm = M // FFN_ROW_TILE
    nf = D_FF // FF_TILE
    sub = 2 * half
    in_specs = [
        pl.BlockSpec((FFN_ROW_TILE, D), lambda m, f: (m, 0)),
        _mod_spec(l, who0, rows_per_mod, 2, FFN_ROW_TILE),
        pl.BlockSpec((D, FF_TILE), lambda m, f: (0, f)),
        pl.BlockSpec((D, FF_TILE), lambda m, f: (0, f + nf)),
        pl.BlockSpec((FF_TILE, D), lambda m, f: (f, 0)),
        pl.BlockSpec((None, None, 1, D), lambda m, f: (l, sub, 0, 0)),
        pl.BlockSpec((None, None, 1, D), lambda m, f: (l, sub, 0, 0)),
    ]
    args = [x, mod, w_in_b, w_in_b, w_out_b, ln_g, ln_b]
    out_specs = [pl.BlockSpec((FFN_ROW_TILE, D), lambda m, f: (m, 0))]
    out_shape = [jax.ShapeDtypeStruct((M, D), F32)]
    if convert_next is not None:
        w_in, w_out, l2, half2 = convert_next
        assert D % nm == 0 and (2 * D_FF) % nf == 0 and D_FF % nf == 0
        in_tile = (D // nm, 2 * D_FF // nf)
        out_tile = (D_FF // nf, D // nm)
        in_specs += [pl.BlockSpec((None, None) + in_tile, lambda m, f: (l2, half2, m, f)),
                     pl.BlockSpec((None, None) + out_tile, lambda m, f: (l2, half2, f, m))]
        args += [w_in, w_out]
        out_specs += [pl.BlockSpec(in_tile, lambda m, f: (m, f)), pl.BlockSpec(out_tile, lambda m, f: (f, m))]
        out_shape += [jax.ShapeDtypeStruct((D, 2 * D_FF), BF16), jax.ShapeDtypeStruct((D_FF, D), BF16)]
    return pl.pallas_call(
        functools.partial(_ffn_kernel, mod_base=3 * sub, convert_next=convert_next is not None),
        grid=(nm, nf),
        in_specs=in_specs,
        out_specs=out_specs,
        out_shape=out_shape,
        scratch_shapes=[pltpu.VMEM((FFN_ROW_TILE, D), BF16)],
        compiler_params=_params("parallel", "arbitrary"),
        name="ffn",
    )(*args)


def _inproj_kernel(x_ref, mod_ref, w_ref, o_ref, u_ref):
    @pl.when(pl.program_id(1) == 0)
    def _():
        shift = mod_ref[3:4, :]
        scale = mod_ref[4:5, :]
        u_ref[...] = (x_ref[...] * (1.0 + scale) + shift).astype(BF16)

    o_ref[...] = _dot(u_ref[...], w_ref[...]).astype(o_ref.dtype)


def _inproj(x, mod, who0, rows_per_mod, w, l):
    M, D = x.shape
    N = w.shape[2]
    return pl.pallas_call(
        _inproj_kernel,
        grid=(M // PROJ_ROW_TILE, N // PROJ_TN),
        in_specs=[
            pl.BlockSpec((PROJ_ROW_TILE, D), lambda m, n: (m, 0)),
            _mod_spec(l, who0, rows_per_mod, 2, PROJ_ROW_TILE),
            pl.BlockSpec((None, D, PROJ_TN), lambda m, n: (l, 0, n)),
        ],
        out_specs=pl.BlockSpec((PROJ_ROW_TILE, PROJ_TN), lambda m, n: (m, n)),
        out_shape=jax.ShapeDtypeStruct((M, N), BF16),
        scratch_shapes=[pltpu.VMEM((PROJ_ROW_TILE, D), BF16)],
        compiler_params=_params("parallel", "arbitrary"),
        name="inproj",
    )(x, mod, w)


def _attn_kernel(*refs, rope, cache_len, emit_kv, tq):
    it = iter(refs)
    q_ref, k_ref, v_ref, qn_ref, kn_ref = (next(it) for _ in range(5))
    if rope:
        cosq_ref, sinq_ref, cosk_ref, sink_ref = (next(it) for _ in range(4))
    if cache_len:
        ck_ref, cv_ref = next(it), next(it)
    o_ref = next(it)
    if emit_kv:
        newk_ref, newv_ref = next(it), next(it)
    kb_ref, vt_ref = next(it), next(it)
    T = kb_ref.shape[0]
    chunks = ([(0, cache_len)] if cache_len else []) + [
        (o, min(ATT_KV_CHUNK, T - o)) for o in range(cache_len, T, ATT_KV_CHUNK)]

    @pl.when(pl.program_id(2) == 0)
    def _():
        k = _rms_norm(k_ref[...].astype(F32), kn_ref[...])
        v = v_ref[...].astype(F32)
        if emit_kv:
            newk_ref[...] = k
            newv_ref[...] = v
        if rope:
            k = _rope(k, cosk_ref[...], sink_ref[...])
        if cache_len:
            k = jnp.concatenate([ck_ref[...], k], axis=0)
            v = jnp.concatenate([cv_ref[...], v], axis=0)
        kb_ref[...] = k.astype(BF16)
        for o, n in chunks:
            vt_ref[:, o:o + n] = v[o:o + n, :].T.astype(BF16)

    q = q_ref[...].astype(F32)
    heads = []
    for g in range(ATT_GROUP):
        qg = _rms_norm(q[:, g * LANES:(g + 1) * LANES], qn_ref[...])
        if rope:
            qg = _rope(qg, cosq_ref[...], sinq_ref[...])
        heads.append(qg.T)
    q_t = jnp.concatenate(heads, axis=1).astype(BF16)

    c2 = (ATT_HEAD_DIM ** -0.5) * math.log2(math.e)
    m = denom = acc = None
    for o, n in chunks:
        s = _dot(kb_ref[o:o + n, :], q_t)
        m_c = jnp.max(s, axis=0, keepdims=True)
        m = m_c if m is None else jnp.maximum(m, m_c)
    for o, n in chunks:
        s = _dot(kb_ref[o:o + n, :], q_t)
        p = jnp.exp2((s - m) * c2)
        d_c = jnp.sum(p, axis=0, keepdims=True)
        a_c = _dot(vt_ref[:, o:o + n], p.astype(BF16))
        denom = d_c if denom is None else denom + d_c
        acc = a_c if acc is None else acc + a_c
    o_t = acc * (1.0 / denom)
    for g in range(ATT_GROUP):
        o_ref[:, g * LANES:(g + 1) * LANES] = o_t[:, g * tq:(g + 1) * tq].T.astype(o_ref.dtype)


def _attention(P, B, S, q_norm, k_norm, l, rope_tabs, cache, emit_kv, tq):
    nq = S // tq
    cache_len = cache[0].shape[3] if cache is not None else 0
    T = cache_len + S
    gw = ATT_GROUP * ATT_HEAD_DIM
    norm_spec = pl.BlockSpec((None, 1, LANES), lambda b, j, i: (l, 0, 0))
    in_specs = [
        pl.BlockSpec((tq, gw), lambda b, j, i: (b * nq + i, j)),
        pl.BlockSpec((S, LANES), lambda b, j, i: (b, COL_AK + j)),
        pl.BlockSpec((S, LANES), lambda b, j, i: (b, COL_AV + j)),
        norm_spec, norm_spec,
    ]
    args = [P, P, P, q_norm, k_norm]
    if rope_tabs is not None:
        cos, sin = rope_tabs
        in_specs += [
            pl.BlockSpec((tq, LANES), lambda b, j, i: (i, 0)),
            pl.BlockSpec((tq, LANES), lambda b, j, i: (i, 0)),
            pl.BlockSpec((S, LANES), lambda b, j, i: (0, 0)),
            pl.BlockSpec((S, LANES), lambda b, j, i: (0, 0)),
        ]
        args += [cos, sin, cos, sin]
    if cache is not None:
        spec = pl.BlockSpec((None, None, None, cache_len, LANES), lambda b, j, i: (b, l, j, 0, 0))
        in_specs += [spec, spec]
        args += list(cache)
    out_specs = [pl.BlockSpec((tq, gw), lambda b, j, i: (b * nq + i, j))]
    out_shape = [jax.ShapeDtypeStruct((B * S, ATT_WIDTH), BF16)]
    if emit_kv:
        kv_spec = pl.BlockSpec((S, LANES), lambda b, j, i: (b, j))
        out_specs += [kv_spec, kv_spec]
        out_shape += [jax.ShapeDtypeStruct((B * S, ATT_KV_HEADS * ATT_HEAD_DIM), F32)] * 2
    return pl.pallas_call(
        functools.partial(_attn_kernel, rope=rope_tabs is not None, cache_len=cache_len,
                          emit_kv=emit_kv, tq=tq),
        grid=(B, ATT_KV_HEADS, nq),
        in_specs=in_specs,
        out_specs=out_specs,
        out_shape=out_shape,
        scratch_shapes=[pltpu.VMEM((T, LANES), BF16), pltpu.VMEM((LANES, T), BF16)],
        compiler_params=_params("parallel", "parallel", "arbitrary"),
        name="attention",
    )(*args)


def _ret_kernel(*refs, heads, **static):
    for hh in range(heads):
        _ret_head(hh, heads, *refs, **static)


def _ret_head(hh, heads, *refs, l, rope, has_s0, emit_state, S):
    it = iter(refs)
    lg_ref, q_ref, k_ref, v_ref, g_ref = (next(it) for _ in range(5))
    if rope:
        cos_ref, sin_ref = next(it), next(it)
    if has_s0:
        s0_ref = next(it).at[:, hh]
    o_ref = next(it)
    if emit_state:
        st_ref = next(it).at[:, hh]
    qb_ref, kb_ref, qd_ref, kd_ref, work_ref, sb_ref = (next(it).at[hh] for _ in range(6))
    lanes = slice(hh * LANES, (hh + 1) * LANES)
    q_ref, k_ref, v_ref, g_ref, o_ref = (r.at[:, lanes] for r in (q_ref, k_ref, v_ref, g_ref, o_ref))

    h = pl.program_id(1) * heads + hh
    C = RET_CHUNK
    n = S // C
    lg_f = lg_ref[l, 0, h]
    lg_b = lg_ref[l, 1, h]
    q = q_ref[...].astype(F32)
    k = k_ref[...].astype(F32) * (RET_DK ** -0.5)
    if rope:
        q = _rope(q, cos_ref[...], sin_ref[...])
        k = _rope(k, cos_ref[...], sin_ref[...])

    t = lax.broadcasted_iota(jnp.int32, (C, LANES), 0).astype(F32)

    def scaled(x, expo):
        return (x.reshape(n, C, LANES) * jnp.exp(expo)[None]).reshape(S, LANES).astype(BF16)

    qb_ref[...] = q.astype(BF16)
    kb_ref[...] = k.astype(BF16)
    qd_ref[:, 0:RET_DK] = scaled(q, (t + 1.0) * lg_f)
    qd_ref[:, RET_DK:] = scaled(q, (C - t) * lg_b)
    kd_ref[:, 0:RET_DK] = scaled(k, (C - 1.0 - t) * lg_f)
    kd_ref[:, RET_DK:] = scaled(k, t * lg_b)

    ii = lax.broadcasted_iota(jnp.int32, (C, C), 0).astype(F32)
    jj = lax.broadcasted_iota(jnp.int32, (C, C), 1).astype(F32)
    dmat = (jnp.where(ii >= jj, jnp.exp(jnp.maximum(ii - jj, 0.0) * lg_f), 0.0)
            + jnp.where(jj >= ii, jnp.exp(jnp.maximum(jj - ii, 0.0) * lg_b), 0.0))
    decay_f = jnp.exp(jnp.full((RET_DK, RET_DV), C * lg_f, F32))
    decay_b = jnp.exp(jnp.full((RET_DK, RET_DV), C * lg_b, F32))
    if has_s0:
        state0 = (s0_ref[0], s0_ref[1])
    else:
        state0 = (jnp.zeros((RET_DK, RET_DV), F32), jnp.zeros((RET_DK, RET_DV), F32))

    def rows(c):
        return pl.ds(pl.multiple_of(c * C, C), C)

    def state_updates(c, carry):
        sl = rows(c)
        update = _dot_tn(kd_ref[sl, :], v_ref[sl, :])
        work_ref[0, sl, :] = update[0:RET_DK, :]
        work_ref[1, sl, :] = update[RET_DK:, :]
        return carry

    lax.fori_loop(0, n, state_updates, 0, unroll=LOOP_UNROLL)

    def scan(c, carry):
        state_f, state_b = carry
        cb = n - 1 - c
        sb_ref[c, 0:RET_DK, :] = state_f.astype(BF16)
        sb_ref[cb, RET_DK:, :] = state_b.astype(BF16)
        return (decay_f * state_f + work_ref[0, rows(c), :], decay_b * state_b + work_ref[1, rows(cb), :])

    state_f, state_b = lax.fori_loop(0, n, scan, state0, unroll=LOOP_UNROLL)
    if emit_state:
        st_ref[0] = state_f
        st_ref[1] = state_b

    U = min(SCAN_GROUP, n)

    def outputs(j, carry):
        chunks = [j * U + u for u in range(U)]
        scores = [(_dot_nt(qb_ref[rows(c), :], kb_ref[rows(c), :]) * dmat).astype(BF16) for c in chunks]
        cross = [_dot(qd_ref[rows(c), :], sb_ref[c]) for c in chunks]
        for c, sc, x in zip(chunks, scores, cross):
            work_ref[0, rows(c), :] = _dot(sc, v_ref[rows(c), :]) + x
        return carry

    lax.fori_loop(0, n // U, outputs, 0)

    y = work_ref[0]
    mu = jnp.mean(y, axis=-1, keepdims=True)
    yc = y - mu
    var = jnp.mean(yc * yc, axis=-1, keepdims=True)
    gate = _silu(g_ref[...].astype(F32))
    o_ref[...] = (yc * lax.rsqrt(var + LN_EPS) * gate).astype(o_ref.dtype)


def _retention(P, B, S, log_decay, l, rope_tabs, s0, emit_state, heads):
    width = heads * LANES

    def col(c0):
        assert c0 % heads == 0
        return pl.BlockSpec((S, width), lambda b, h: (b, c0 // heads + h))

    in_specs = [pl.BlockSpec(memory_space=pltpu.SMEM), col(COL_RQ), col(COL_RK), col(COL_RV), col(COL_RG)]
    args = [log_decay, P, P, P, P]
    if rope_tabs is not None:
        tab = pl.BlockSpec((S, LANES), lambda b, h: (0, 0))
        in_specs += [tab, tab]
        args += list(rope_tabs)
    if s0 is not None:
        in_specs.append(pl.BlockSpec((None, None, 2, heads, RET_DK, RET_DV), lambda b, h: (b, l, 0, h, 0, 0)))
        args.append(s0)
    out_specs = [pl.BlockSpec((S, width), lambda b, h: (b, h))]
    out_shape = [jax.ShapeDtypeStruct((B * S, RET_WIDTH), BF16)]
    if emit_state:
        out_specs.append(pl.BlockSpec((None, 2, heads, RET_DK, RET_DV), lambda b, h: (b, 0, h, 0, 0)))
        out_shape.append(jax.ShapeDtypeStruct((B, 2, RET_HEADS, RET_DK, RET_DV), F32))
    return pl.pallas_call(
        functools.partial(_ret_kernel, heads=heads, l=l, rope=rope_tabs is not None, has_s0=s0 is not None,
                          emit_state=emit_state, S=S),
        grid=(B, RET_HEADS // heads),
        in_specs=in_specs,
        out_specs=out_specs,
        out_shape=out_shape,
        scratch_shapes=[pltpu.VMEM((heads, S, RET_DK), BF16), pltpu.VMEM((heads, S, RET_DK), BF16),
                        pltpu.VMEM((heads, S, 2 * RET_DK), BF16), pltpu.VMEM((heads, S, 2 * RET_DK), BF16),
                        pltpu.VMEM((heads, 2, S, RET_DV), F32),
                        pltpu.VMEM((heads, S // RET_CHUNK, 2 * RET_DK, RET_DV), BF16)],
        compiler_params=_params("parallel", "parallel"),
        name="retention",
    )(*args)


def _gla_kernel(*refs, pairs, **static):
    for pp in range(pairs):
        _gla_pair(pp, *refs, **static)


def _gla_pair(pp, *refs, has_s0, emit_state, S):
    it = iter(refs)
    q_ref, k_ref, v_ref, gr_ref, ga_ref, wa_ref, ba_ref, gn_ref = (next(it) for _ in range(8))
    if has_s0:
        s0_ref = next(it).at[:, pp]
    o_ref = next(it)
    if emit_state:
        st_ref = next(it).at[:, pp]
    la_ref, qt_ref, kcat_ref, dec_ref, work_ref, sb_ref = (next(it).at[pp] for _ in range(6))
    narrow = slice(pp * LANES, (pp + 1) * LANES)
    wide = slice(pp * 2 * GLA_DV, (pp + 1) * 2 * GLA_DV)
    q_ref, k_ref = q_ref.at[:, narrow], k_ref.at[:, narrow]
    v_ref, gr_ref, o_ref = v_ref.at[:, wide], gr_ref.at[:, wide], o_ref.at[:, wide]
    wa_ref, ba_ref = wa_ref.at[:, :, narrow], ba_ref.at[:, :, narrow]
    C = GLA_CHUNK
    n = S // C
    U = min(SCAN_GROUP, n)
    ga = ga_ref[...]
    for d in range(2):
        pre = _dot(ga, wa_ref[d]) + ba_ref[d]
        la_ref[:, d * LANES:(d + 1) * LANES] = (
            (jnp.minimum(pre, 0.0) - jnp.log1p(jnp.exp(-jnp.abs(pre)))) * (1.0 / GLA_TAU))

    ii = lax.broadcasted_iota(jnp.int32, (C, C), 0)
    jj = lax.broadcasted_iota(jnp.int32, (C, C), 1)
    lower_b = jnp.where(jj <= ii, 1.0, 0.0).astype(BF16)
    mask_f = jnp.concatenate([jnp.where(jj <= ii, 1.0, 0.0)] * 2, axis=1)
    mask_b = jnp.concatenate([jnp.where(jj >= ii, 1.0, 0.0)] * 2, axis=1)
    lane = lax.broadcasted_iota(jnp.int32, (C, LANES), 1)
    head0_lanes = lane < GLA_DK
    lane_sq = lax.broadcasted_iota(jnp.int32, (LANES, LANES), 1)
    row_sq = lax.broadcasted_iota(jnp.int32, (LANES, LANES), 0)
    lane_wide = lax.broadcasted_iota(jnp.int32, (C, 2 * GLA_DV), 1)

    def rows(c):
        return pl.ds(pl.multiple_of(c * C, C), C)

    def update_rows(c):
        return pl.ds(pl.multiple_of(c * GLA_DV, GLA_DV), GLA_DV)

    def prepare(j, carry):
        chunks = [j * U + u for u in range(U)]
        cums = []
        for c in chunks:
            la = la_ref[rows(c), :]
            hi = la.astype(BF16)
            r1 = la - hi.astype(F32)
            mid = r1.astype(BF16)
            lo = (r1 - mid.astype(F32)).astype(BF16)
            cums.append((la, _dot(lower_b, hi) + _dot(lower_b, mid) + _dot(lower_b, lo)))
        scaled_keys = []
        for c, (la, cum) in zip(chunks, cums):
            sl = rows(c)
            qc = q_ref[sl, :].astype(F32) * (GLA_DK ** -0.5)
            kc = k_ref[sl, :].astype(F32)
            b_f = cum[:, 0:LANES]
            cum_b = cum[:, LANES:]
            b_b = cum_b[C - 1:C, :] - cum_b + la[:, LANES:]
            end_f = b_f[C - 1:C, :]
            end_b = b_b[0:1, :]
            qt_ref[sl, 0:LANES] = (qc * jnp.exp(b_f)).astype(BF16)
            qt_ref[sl, LANES:] = (qc * jnp.exp(b_b)).astype(BF16)
            for d, b in ((0, b_f), (1, b_b)):
                k_t = kc * jnp.exp(-b)
                kcat_ref[d, c, 0:C, :] = jnp.where(head0_lanes, k_t, 0.0).astype(BF16)
                kcat_ref[d, c, C:, :] = jnp.where(head0_lanes, 0.0, k_t).astype(BF16)
            dec_ref[0, c] = jnp.exp(end_f)
            dec_ref[1, c] = jnp.exp(end_b)
            scaled_keys.append(jnp.concatenate(
                [kc * jnp.exp(end_f - b_f), kc * jnp.exp(end_b - b_b)], axis=1).astype(BF16))
        for c, k_s in zip(chunks, scaled_keys):
            kv = _dot_tn(v_ref[rows(c), :], k_s)
            for d in range(2):
                cols = kv[:, d * LANES:(d + 1) * LANES]
                work_ref[d, update_rows(c), :] = jnp.where(lane_sq < GLA_DK, cols[0:GLA_DV, :], cols[GLA_DV:, :])
        return carry

    lax.fori_loop(0, n // U, prepare, 0)

    if has_s0:
        state0 = (s0_ref[0].T, s0_ref[1].T)
    else:
        state0 = (jnp.zeros((GLA_DV, LANES), F32), jnp.zeros((GLA_DV, LANES), F32))

    def scan(c, carry):
        state_f, state_b = carry
        cb = n - 1 - c
        for d, cc, state in ((0, c, state_f), (1, cb, state_b)):
            s2 = state.T
            r0 = d * LANES
            sb_ref[cc, r0:r0 + LANES, 0:GLA_DV] = jnp.where(row_sq < GLA_DK, s2, 0.0).astype(BF16)
            sb_ref[cc, r0:r0 + LANES, GLA_DV:] = jnp.where(row_sq < GLA_DK, 0.0, s2).astype(BF16)
        return (state_f * dec_ref[0, c] + work_ref[0, update_rows(c), :],
                state_b * dec_ref[1, cb] + work_ref[1, update_rows(cb), :])

    states = lax.fori_loop(0, n, scan, state0, unroll=LOOP_UNROLL)
    if emit_state:
        st_ref[0] = states[0].T
        st_ref[1] = states[1].T

    def outputs(j, carry):
        chunks = [j * U + u for u in range(U)]
        scores, cross = [], []
        for c in chunks:
            q_t = qt_ref[rows(c), :]
            att = (_dot_nt(q_t[:, 0:LANES], kcat_ref[0, c]) * mask_f
                   + _dot_nt(q_t[:, LANES:], kcat_ref[1, c]) * mask_b)
            scores.append(att.astype(BF16))
            cross.append(_dot(q_t, sb_ref[c]))
        for c, att, x in zip(chunks, scores, cross):
            v_c = v_ref[rows(c), :]
            zeros = jnp.zeros_like(v_c)
            v_bd = jnp.concatenate([jnp.where(lane_wide < GLA_DV, v_c, zeros),
                                    jnp.where(lane_wide < GLA_DV, zeros, v_c)], axis=0)
            y = _dot(att, v_bd) + x
            for i in range(2):
                work_ref[0, pl.ds(pl.multiple_of(i * S + c * C, C), C), :] = y[:, i * GLA_DV:(i + 1) * GLA_DV]
        return carry

    lax.fori_loop(0, n // U, outputs, 0)

    for i in range(2):
        y = _rms_norm(work_ref[0, i * S:(i + 1) * S, :], gn_ref[...])
        gate = _silu(gr_ref[:, i * GLA_DV:(i + 1) * GLA_DV].astype(F32))
        o_ref[:, i * GLA_DV:(i + 1) * GLA_DV] = (y * gate).astype(o_ref.dtype)


def _gla(P, B, S, wa, ba, gla_norm, l, s0, emit_state, pairs):
    nw = pairs * LANES
    pw = pairs * 2 * GLA_DV
    assert COL_GQ % pairs == 0 and COL_GK % pairs == 0
    in_specs = [
        pl.BlockSpec((S, nw), lambda b, p: (b, COL_GQ // pairs + p)),
        pl.BlockSpec((S, nw), lambda b, p: (b, COL_GK // pairs + p)),
        pl.BlockSpec((S, pw), lambda b, p: (b, COL_GV // (2 * pairs) + p)),
        pl.BlockSpec((S, pw), lambda b, p: (b, COL_GR // (2 * pairs) + p)),
        pl.BlockSpec((S, LANES), lambda b, p: (b, COL_GA)),
        pl.BlockSpec((None, 2, LANES, nw), lambda b, p: (l, 0, 0, p)),
        pl.BlockSpec((None, 2, 1, nw), lambda b, p: (l, 0, 0, p)),
        pl.BlockSpec((None, 1, GLA_DV), lambda b, p: (l, 0, 0)),
    ]
    args = [P, P, P, P, P, wa, ba, gla_norm]
    if s0 is not None:
        in_specs.append(pl.BlockSpec((None, None, 2, pairs, LANES, GLA_DV), lambda b, p: (b, l, 0, p, 0, 0)))
        args.append(s0)
    out_specs = [pl.BlockSpec((S, pw), lambda b, p: (b, p))]
    out_shape = [jax.ShapeDtypeStruct((B * S, GLA_WIDTH), BF16)]
    if emit_state:
        out_specs.append(pl.BlockSpec((None, 2, pairs, LANES, GLA_DV), lambda b, p: (b, 0, p, 0, 0)))
        out_shape.append(jax.ShapeDtypeStruct((B, 2, GLA_PAIRS, LANES, GLA_DV), F32))
    return pl.pallas_call(
        functools.partial(_gla_kernel, pairs=pairs, has_s0=s0 is not None, emit_state=emit_state, S=S),
        grid=(B, GLA_PAIRS // pairs),
        in_specs=in_specs,
        out_specs=out_specs,
        out_shape=out_shape,
        scratch_shapes=[pltpu.VMEM((pairs, S, 2 * LANES), F32),
                        pltpu.VMEM((pairs, S, 2 * LANES), BF16),
                        pltpu.VMEM((pairs, 2, S // GLA_CHUNK, 2 * GLA_CHUNK, LANES), BF16),
                        pltpu.VMEM((pairs, 2, S // GLA_CHUNK, 1, LANES), F32),
                        pltpu.VMEM((pairs, 2, 2 * S, GLA_DV), F32),
                        pltpu.VMEM((pairs, S // GLA_CHUNK, 2 * LANES, 2 * GLA_DV), BF16)],
        compiler_params=_params("parallel", "parallel"),
        name="gla",
    )(*args)


def _outproj_kernel(x_ref, mod_ref, a_ref, r_ref, gl_ref, wa_ref, wr_ref, wg_ref, g_ref, b_ref, o_ref):
    rows = x_ref.shape[0] // OUTPROJ_SPLIT
    mixed = []
    for i in range(OUTPROJ_SPLIT):
        sl = slice(i * rows, (i + 1) * rows)
        mixed.append(_dot(a_ref[sl, :], wa_ref[...]) + _dot(r_ref[sl, :], wr_ref[...])
                     + _dot(gl_ref[sl, :], wg_ref[...]))
    for i in range(OUTPROJ_SPLIT):
        sl = slice(i * rows, (i + 1) * rows)
        y = DEEPNORM_ALPHA * x_ref[sl, :] + mod_ref[5:6, :] * mixed[i]
        o_ref[sl, :] = _layer_norm(y, g_ref[...], b_ref[...])


def _outproj(x, mod, who0, rows_per_mod, att, ret, gla, w, ln_g, ln_b, l):
    M, D = x.shape
    return pl.pallas_call(
        _outproj_kernel,
        grid=(M // ROW_TILE,),
        in_specs=[
            pl.BlockSpec((ROW_TILE, D), lambda m: (m, 0)),
            _mod_spec(l, who0, rows_per_mod, 1),
            pl.BlockSpec((ROW_TILE, ATT_WIDTH), lambda m: (m, 0)),
            pl.BlockSpec((ROW_TILE, RET_WIDTH), lambda m: (m, 0)),
            pl.BlockSpec((ROW_TILE, GLA_WIDTH), lambda m: (m, 0)),
            pl.BlockSpec((None, ATT_WIDTH, D), lambda m: (l, 0, 0)),
            pl.BlockSpec((None, RET_WIDTH, D), lambda m: (l, ATT_WIDTH // RET_WIDTH, 0)),
            pl.BlockSpec((None, GLA_WIDTH, D), lambda m: (l, (ATT_WIDTH + RET_WIDTH) // GLA_WIDTH, 0)),
            pl.BlockSpec((None, None, 1, D), lambda m: (l, 1, 0, 0)),
            pl.BlockSpec((None, None, 1, D), lambda m: (l, 1, 0, 0)),
        ],
        out_specs=pl.BlockSpec((ROW_TILE, D), lambda m: (m, 0)),
        out_shape=jax.ShapeDtypeStruct((M, D), F32),
        compiler_params=_params("parallel"),
        name="outproj",
    )(x, mod, att, ret, gla, w, w, w, ln_g, ln_b)


def _trunk_layer(x, B, S, l, who0, w, rope_tabs, ctx, attn_tq):
    is_context = ctx is None
    rows_per_mod = x.shape[0] if is_context else S
    mod = w["mod"]

    def ffn(x, half):
        nxt = (l, 1) if half == 0 else (l + 1, 0)
        convert = None
        if nxt[0] < DEPTH and nxt not in w["ffn_bf16"]:
            convert = (w["ffn_w_in"], w["ffn_w_out"], nxt[0], nxt[1])
        outs = _ffn(x, mod, who0, rows_per_mod, *w["ffn_bf16"][(l, half)], w["ln_g"], w["ln_b"], l, half, convert)
        if convert is not None:
            w["ffn_bf16"][nxt] = (outs[1], outs[2])
        return outs[0]

    x = ffn(x, 0)
    P = _inproj(x, mod, who0, rows_per_mod, w["mix_w_in"], l)
    cache = None if is_context else (ctx[0], ctx[1])
    s_ret0 = None if is_context else ctx[2]
    s_gla0 = None if is_context else ctx[3]
    att_out = _attention(P, B, S, w["q_norm"], w["k_norm"], l, rope_tabs, cache, is_context, attn_tq)
    ret_out = _retention(P, B, S, w["log_decay"], l, rope_tabs, s_ret0, is_context,
                         RET_HEADS if is_context else 1)
    gla_out = _gla(P, B, S, w["gla_wa"], w["gla_ba"], w["gla_norm"], l, s_gla0, is_context,
                   GLA_PAIRS if is_context else 1)
    x = _outproj(x, mod, who0, rows_per_mod, att_out[0], ret_out[0], gla_out[0],
                 w["mix_w_out"], w["ln_g"], w["ln_b"], l)
    x = ffn(x, 1)
    new_ctx = (att_out[1], att_out[2], ret_out[1], gla_out[1]) if is_context else None
    return x, new_ctx


def _rope_tables(rows):
    row = jnp.repeat(jnp.arange(rows, dtype=F32), GRID_W)
    col = jnp.tile(jnp.arange(GRID_W, dtype=F32), rows)
    n_freq = ATT_HEAD_DIM // 4
    inv = ROPE_THETA ** (-jnp.arange(n_freq, dtype=F32) / n_freq)
    ang = jnp.concatenate([row[:, None] * inv, col[:, None] * inv], axis=-1)
    cos, sin = jnp.cos(ang), jnp.sin(ang)
    cos_full = jnp.repeat(cos, 2, axis=-1)
    sin_signed = jnp.stack([-sin, sin], axis=-1).reshape(ang.shape[0], ATT_HEAD_DIM)
    return cos_full, sin_signed


def kernel(x_prompt, x_sample, c, cache_attn_k, cache_attn_v, state_ret, state_gla, c_ctx,
           w_mod, b_mod, ln_g, ln_b, ffn_w_in, ffn_w_out, mix_w_in, mix_w_out,
           att_q_norm, att_k_norm, ret_log_decay, gla_w_a2, gla_b_a, gla_norm):
    B_ctx, S_ctx, D = x_prompt.shape
    B_lat, S_lat, _ = x_sample.shape

    gla_wa = jnp.zeros((DEPTH, 2, LANES, GLA_HEADS * GLA_DK), BF16)
    for d in range(2):
        gla_wa = gla_wa.at[:, d, d * GLA_GATE_RANK:(d + 1) * GLA_GATE_RANK, :].set(gla_w_a2[:, d].astype(BF16))

    cvec = jnp.concatenate([c_ctx[None, :], c, jnp.zeros((MOD_ROWS - 1 - B_lat, D), F32)], axis=0)
    mod = _modulation(cvec, w_mod, b_mod[:, None, :]).reshape(DEPTH, MOD_ROWS, N_MOD, D)

    w = dict(
        mod=mod,
        ffn_w_in=ffn_w_in, ffn_w_out=ffn_w_out,
        ffn_bf16={(0, 0): (ffn_w_in[0, 0].astype(BF16), ffn_w_out[0, 0].astype(BF16))},
        mix_w_in=jnp.pad(mix_w_in.astype(BF16), ((0, 0), (0, 0), (0, PROJ_COLS_PADDED - PROJ_COLS))),
        mix_w_out=mix_w_out.astype(BF16),
        ln_g=ln_g[:, :, None, :], ln_b=ln_b[:, :, None, :],
        q_norm=att_q_norm[:, None, :], k_norm=att_k_norm[:, None, :],
        log_decay=ret_log_decay, gla_wa=gla_wa, gla_ba=gla_b_a[:, :, None, :],
        gla_norm=gla_norm[:, None, :])

    h = x_prompt.reshape(B_ctx * S_ctx, D)
    ks_l, vs_l, sr_l, sg_l = [], [], [], []
    for l in range(DEPTH):
        h, (k_l, v_l, s_r, s_g) = _trunk_layer(h, B_ctx, S_ctx, l, 0, w, None, None, S_ctx)
        ks_l.append(k_l.reshape(B_ctx, S_ctx, ATT_KV_HEADS, ATT_HEAD_DIM))
        vs_l.append(v_l.reshape(B_ctx, S_ctx, ATT_KV_HEADS, ATT_HEAD_DIM))
        sr_l.append(s_r)
        sg_l.append(s_g.reshape(B_ctx, 2, GLA_HEADS, GLA_DK, GLA_DV))
    y_prompt = h.reshape(B_ctx, S_ctx, D)

    rope_tabs = _rope_tables(S_lat // GRID_W)
    ctx = (cache_attn_k.transpose(0, 1, 3, 2, 4), cache_attn_v.transpose(0, 1, 3, 2, 4), state_ret,
           state_gla.reshape(B_lat, DEPTH, 2, GLA_PAIRS, LANES, GLA_DV))
    g = x_sample.reshape(B_lat * S_lat, D)
    for l in range(DEPTH):
        g, _ = _trunk_layer(g, B_lat, S_lat, l, 1, w, rope_tabs, ctx, 256)
    y_sample = g.reshape(B_lat, S_lat, D)

    return (y_prompt, y_sample, jnp.stack(ks_l, axis=1), jnp.stack(vs_l, axis=1),
            jnp.stack(sr_l, axis=1), jnp.stack(sg_l, axis=1))
```

```python
import functools
import math

import jax
import jax.numpy as jnp
from jax import lax
from jax.experimental import pallas as pl
from jax.experimental.pallas import tpu as pltpu

F32 = jnp.float32
BF16 = jnp.bfloat16

D_MODEL = 2048
DEPTH = 2
GRID_W = 64
ATT_HEAD_DIM = 128
ATT_HEADS = 8
ATT_KV_HEADS = 2
ATT_GROUP = ATT_HEADS // ATT_KV_HEADS
ATT_WIDTH = ATT_HEADS * ATT_HEAD_DIM
ATT_KV_CHUNK = 512
ROPE_THETA = 10000.0
RET_DK = 128
RET_DV = 128
RET_HEADS = 4
RET_WIDTH = RET_HEADS * RET_DV
RET_CHUNK = 128
GLA_DK = 64
GLA_DV = 128
GLA_HEADS = 4
GLA_PAIRS = GLA_HEADS // 2
GLA_WIDTH = GLA_HEADS * GLA_DV
GLA_GATE_RANK = 16
GLA_TAU = 16.0
GLA_CHUNK = 64
D_FF = 5632
N_MOD = 9
MACARON_WEIGHT = 0.5
DEEPNORM_ALPHA = (2 * DEPTH) ** 0.25
LN_EPS = 1e-5
RMS_EPS = 1e-6

LANES = 128
PROJ_COLS = 5152
PROJ_COLS_PADDED = 5376
PROJ_TN = 1792
COL_AQ, COL_AK, COL_AV = 0, 8, 10
COL_RQ, COL_RK, COL_RV, COL_RG = 12, 16, 20, 24
COL_GQ, COL_GK, COL_GV, COL_GR, COL_GA = 28, 30, 32, 36, 40

VMEM_LIMIT_BYTES = 56 * 1024 * 1024
ROW_TILE = 512
PROJ_ROW_TILE = 1024
FF_TILE = 512
FFN_ROW_TILE = 512
FFN_SPLIT = 2
MOD_TN = 2048
OUTPROJ_SPLIT = 4
LOOP_UNROLL = 8
SCAN_GROUP = 16
MOD_ROWS = 8


def _params(*sem):
    return pltpu.CompilerParams(dimension_semantics=sem, vmem_limit_bytes=VMEM_LIMIT_BYTES)


def _dot(a, b):
    return jnp.dot(a, b, preferred_element_type=F32)


def _dot_nt(a, b):
    return lax.dot_general(a, b, (((1,), (1,)), ((), ())), preferred_element_type=F32)


def _dot_tn(a, b):
    return lax.dot_general(a, b, (((0,), (0,)), ((), ())), preferred_element_type=F32)


def _silu(x):
    return x * jax.nn.sigmoid(x)


def _layer_norm(y, g, b):
    mu = jnp.mean(y, axis=-1, keepdims=True)
    d = y - mu
    var = jnp.mean(d * d, axis=-1, keepdims=True)
    return d * lax.rsqrt(var + LN_EPS) * g + b


def _rms_norm(x, g):
    return x * lax.rsqrt(jnp.mean(x * x, axis=-1, keepdims=True) + RMS_EPS) * g


def _rope(x, cos, sin_signed):
    lane = lax.broadcasted_iota(jnp.int32, x.shape, 1)
    partner = jnp.where((lane & 1) == 0, pltpu.roll(x, LANES - 1, 1), pltpu.roll(x, 1, 1))
    return x * cos + partner * sin_signed


def _mod_spec(l, who0, rows_per_mod, ndim_grid, row_tile=ROW_TILE):
    if ndim_grid == 1:
        return pl.BlockSpec((None, None, N_MOD, D_MODEL),
                            lambda m: (l, who0 + (m * row_tile) // rows_per_mod, 0, 0))
    return pl.BlockSpec((None, None, N_MOD, D_MODEL),
                        lambda m, n: (l, who0 + (m * row_tile) // rows_per_mod, 0, 0))


def _mod_kernel(c_ref, w_ref, b_ref, o_ref):
    a = _silu(c_ref[...]).astype(BF16)
    o_ref[...] = _dot(a, w_ref[...].astype(BF16)) + b_ref[...]


def _modulation(cvec, w_mod, b_mod):
    L, D, N = w_mod.shape
    return pl.pallas_call(
        _mod_kernel,
        grid=(L, N // MOD_TN),
        in_specs=[
            pl.BlockSpec((MOD_ROWS, D), lambda l, n: (0, 0)),
            pl.BlockSpec((None, D, MOD_TN), lambda l, n: (l, 0, n)),
            pl.BlockSpec((None, 1, MOD_TN), lambda l, n: (l, 0, n)),
        ],
        out_specs=pl.BlockSpec((None, MOD_ROWS, MOD_TN), lambda l, n: (l, 0, n)),
        out_shape=jax.ShapeDtypeStruct((L, MOD_ROWS, N), F32),
        compiler_params=_params("parallel", "parallel"),
        name="modulation",
    )(cvec, w_mod, b_mod)


def _ffn_kernel(*refs, mod_base, convert_next):
    if convert_next:
        (x_ref, mod_ref, wg_ref, wu_ref, wo_ref, g_ref, b_ref, next_in_ref, next_out_ref,
         o_ref, next_in_b_ref, next_out_b_ref, u_ref) = refs
        next_in_b_ref[...] = next_in_ref[...].astype(BF16)
        next_out_b_ref[...] = next_out_ref[...].astype(BF16)
    else:
        x_ref, mod_ref, wg_ref, wu_ref, wo_ref, g_ref, b_ref, o_ref, u_ref = refs
    f = pl.program_id(1)
    last = pl.num_programs(1) - 1

    def step(sl, first, final):
        if first:
            shift = mod_ref[mod_base:mod_base + 1, :]
            scale = mod_ref[mod_base + 1:mod_base + 2, :]
            u = (x_ref[sl, :] * (1.0 + scale) + shift).astype(BF16)
            u_ref[sl, :] = u
        else:
            u = u_ref[sl, :]
        gate = _dot(u, wg_ref[...])
        up = _dot(u, wu_ref[...])
        act = (_silu(gate) * up).astype(BF16)
        acc = _dot(act, wo_ref[...])
        if not first:
            acc = o_ref[sl, :] + acc
        if final:
            g3 = mod_ref[mod_base + 2:mod_base + 3, :]
            y = DEEPNORM_ALPHA * x_ref[sl, :] + MACARON_WEIGHT * (g3 * acc)
            acc = _layer_norm(y, g_ref[...], b_ref[...])
        o_ref[sl, :] = acc

    rows = x_ref.shape[0] // FFN_SPLIT
    subtiles = [slice(i * rows, (i + 1) * rows) for i in range(FFN_SPLIT)]

    @pl.when(f == 0)
    def _():
        for sl in subtiles:
            step(sl, True, False)

    @pl.when(jnp.logical_and(f > 0, f < last))
    def _():
        step(slice(None), False, False)

    @pl.when(f == last)
    def _():
        for sl in subtiles:
            step(sl, False, True)


def _ffn(x, mod, who0, rows_per_mod, w_in_b, w_out_b, ln_g, ln_b, l, half, convert_next=None):
    M, D = x.shape
    nm = M // FFN_ROW_TILE
    nf = D_FF // FF_TILE
    sub = 2 * half
    in_specs = [
        pl.BlockSpec((FFN_ROW_TILE, D), lambda m, f: (m, 0)),
        _mod_spec(l, who0, rows_per_mod, 2, FFN_ROW_TILE),
        pl.BlockSpec((D, FF_TILE), lambda m, f: (0, f)),
        pl.BlockSpec((D, FF_TILE), lambda m, f: (0, f + nf)),
        pl.BlockSpec((FF_TILE, D), lambda m, f: (f, 0)),
        pl.BlockSpec((None, None, 1, D), lambda m, f: (l, sub, 0, 0)),
        pl.BlockSpec((None, None, 1, D), lambda m, f: (l, sub, 0, 0)),
    ]
    args = [x, mod, w_in_b, w_in_b, w_out_b, ln_g, ln_b]
    out_specs = [pl.BlockSpec((FFN_ROW_TILE, D), lambda m, f: (m, 0))]
    out_shape = [jax.ShapeDtypeStruct((M, D), F32)]
    if convert_next is not None:
        w_in, w_out, l2, half2 = convert_next
        assert D % nm == 0 and (2 * D_FF) % nf == 0 and D_FF % nf == 0
        in_tile = (D // nm, 2 * D_FF // nf)
        out_tile = (D_FF // nf, D // nm)
        in_specs += [pl.BlockSpec((None, None) + in_tile, lambda m, f: (l2, half2, m, f)),
                     pl.BlockSpec((None, None) + out_tile, lambda m, f: (l2, half2, f, m))]
        args += [w_in, w_out]
        out_specs += [pl.BlockSpec(in_tile, lambda m, f: (m, f)), pl.BlockSpec(out_tile, lambda m, f: (f, m))]
        out_shape += [jax.ShapeDtypeStruct((D, 2 * D_FF), BF16), jax.ShapeDtypeStruct((D_FF, D), BF16)]
    return pl.pallas_call(
        functools.partial(_ffn_kernel, mod_base=3 * sub, convert_next=convert_next is not None),
        grid=(nm, nf),
        in_specs=in_specs,
        out_specs=out_specs,
        out_shape=out_shape,
        scratch_shapes=[pltpu.VMEM((FFN_ROW_TILE, D), BF16)],
        compiler_params=_params("parallel", "arbitrary"),
        name="ffn",
    )(*args)


def _inproj_kernel(x_ref, mod_ref, w_ref, o_ref, u_ref):
    @pl.when(pl.program_id(1) == 0)
    def _():
        shift = mod_ref[3:4, :]
        scale = mod_ref[4:5, :]
        u_ref[...] = (x_ref[...] * (1.0 + scale) + shift).astype(BF16)

    o_ref[...] = _dot(u_ref[...], w_ref[...]).astype(o_ref.dtype)


def _inproj(x, mod, who0, rows_per_mod, w, l):
    M, D = x.shape
    N = w.shape[2]
    return pl.pallas_call(
        _inproj_kernel,
        grid=(M // PROJ_ROW_TILE, N // PROJ_TN),
        in_specs=[
            pl.BlockSpec((PROJ_ROW_TILE, D), lambda m, n: (m, 0)),
            _mod_spec(l, who0, rows_per_mod, 2, PROJ_ROW_TILE),
            pl.BlockSpec((None, D, PROJ_TN), lambda m, n: (l, 0, n)),
        ],
        out_specs=pl.BlockSpec((PROJ_ROW_TILE, PROJ_TN), lambda m, n: (m, n)),
        out_shape=jax.ShapeDtypeStruct((M, N), BF16),
        scratch_shapes=[pltpu.VMEM((PROJ_ROW_TILE, D), BF16)],
        compiler_params=_params("parallel", "arbitrary"),
        name="inproj",
    )(x, mod, w)


def _attn_kernel(*refs, rope, cache_len, emit_kv, tq):
    it = iter(refs)
    q_ref, k_ref, v_ref, qn_ref, kn_ref = (next(it) for _ in range(5))
    if rope:
        cosq_ref, sinq_ref, cosk_ref, sink_ref = (next(it) for _ in range(4))
    if cache_len:
        ck_ref, cv_ref = next(it), next(it)
    o_ref = next(it)
    if emit_kv:
        newk_ref, newv_ref = next(it), next(it)
    kb_ref, vt_ref = next(it), next(it)
    T = kb_ref.shape[0]
    chunks = ([(0, cache_len)] if cache_len else []) + [
        (o, min(ATT_KV_CHUNK, T - o)) for o in range(cache_len, T, ATT_KV_CHUNK)]

    @pl.when(pl.program_id(2) == 0)
    def _():
        k = _rms_norm(k_ref[...].astype(F32), kn_ref[...])
        v = v_ref[...].astype(F32)
        if emit_kv:
            newk_ref[...] = k
            newv_ref[...] = v
        if rope:
            k = _rope(k, cosk_ref[...], sink_ref[...])
        if cache_len:
            k = jnp.concatenate([ck_ref[...], k], axis=0)
            v = jnp.concatenate([cv_ref[...], v], axis=0)
        kb_ref[...] = k.astype(BF16)
        for o, n in chunks:
            vt_ref[:, o:o + n] = v[o:o + n, :].T.astype(BF16)

    q = q_ref[...].astype(F32)
    heads = []
    for g in range(ATT_GROUP):
        qg = _rms_norm(q[:, g * LANES:(g + 1) * LANES], qn_ref[...])
        if rope:
            qg = _rope(qg, cosq_ref[...], sinq_ref[...])
        heads.append(qg.T)
    q_t = jnp.concatenate(heads, axis=1).astype(BF16)

    c2 = (ATT_HEAD_DIM ** -0.5) * math.log2(math.e)
    m = denom = acc = None
    for o, n in chunks:
        s = _dot(kb_ref[o:o + n, :], q_t)
        m_c = jnp.max(s, axis=0, keepdims=True)
        m = m_c if m is None else jnp.maximum(m, m_c)
    for o, n in chunks:
        s = _dot(kb_ref[o:o + n, :], q_t)
        p = jnp.exp2((s - m) * c2)
        d_c = jnp.sum(p, axis=0, keepdims=True)
        a_c = _dot(vt_ref[:, o:o + n], p.astype(BF16))
        denom = d_c if denom is None else denom + d_c
        acc = a_c if acc is None else acc + a_c
    o_t = acc * (1.0 / denom)
    for g in range(ATT_GROUP):
        o_ref[:, g * LANES:(g + 1) * LANES] = o_t[:, g * tq:(g + 1) * tq].T.astype(o_ref.dtype)


def _attention(P, B, S, q_norm, k_norm, l, rope_tabs, cache, emit_kv, tq):
    nq = S // tq
    cache_len = cache[0].shape[3] if cache is not None else 0
    T = cache_len + S
    gw = ATT_GROUP * ATT_HEAD_DIM
    norm_spec = pl.BlockSpec((None, 1, LANES), lambda b, j, i: (l, 0, 0))
    in_specs = [
        pl.BlockSpec((tq, gw), lambda b, j, i: (b * nq + i, j)),
        pl.BlockSpec((S, LANES), lambda b, j, i: (b, COL_AK + j)),
        pl.BlockSpec((S, LANES), lambda b, j, i: (b, COL_AV + j)),
        norm_spec, norm_spec,
    ]
    args = [P, P, P, q_norm, k_norm]
    if rope_tabs is not None:
        cos, sin = rope_tabs
        in_specs += [
            pl.BlockSpec((tq, LANES), lambda b, j, i: (i, 0)),
            pl.BlockSpec((tq, LANES), lambda b, j, i: (i, 0)),
            pl.BlockSpec((S, LANES), lambda b, j, i: (0, 0)),
            pl.BlockSpec((S, LANES), lambda b, j, i: (0, 0)),
        ]
        args += [cos, sin, cos, sin]
    if cache is not None:
        spec = pl.BlockSpec((None, None, None, cache_len, LANES), lambda b, j, i: (b, l, j, 0, 0))
        in_specs += [spec, spec]
        args += list(cache)
    out_specs = [pl.BlockSpec((tq, gw), lambda b, j, i: (b * nq + i, j))]
    out_shape = [jax.ShapeDtypeStruct((B * S, ATT_WIDTH), BF16)]
    if emit_kv:
        kv_spec = pl.BlockSpec((S, LANES), lambda b, j, i: (b, j))
        out_specs += [kv_spec, kv_spec]
        out_shape += [jax.ShapeDtypeStruct((B * S, ATT_KV_HEADS * ATT_HEAD_DIM), F32)] * 2
    return pl.pallas_call(
        functools.partial(_attn_kernel, rope=rope_tabs is not None, cache_len=cache_len,
                          emit_kv=emit_kv, tq=tq),
        grid=(B, ATT_KV_HEADS, nq),
        in_specs=in_specs,
        out_specs=out_specs,
        out_shape=out_shape,
        scratch_shapes=[pltpu.VMEM((T, LANES), BF16), pltpu.VMEM((LANES, T), BF16)],
        compiler_params=_params("parallel", "parallel", "arbitrary"),
        name="attention",
    )(*args)


def _ret_kernel(*refs, heads, **static):
    for hh in range(heads):
        _ret_head(hh, heads, *refs, **static)


def _ret_head(hh, heads, *refs, l, rope, has_s0, emit_state, S):
    it = iter(refs)
    lg_ref, q_ref, k_ref, v_ref, g_ref = (next(it) for _ in range(5))
    if rope:
        cos_ref, sin_ref = next(it), next(it)
    if has_s0:
        s0_ref = next(it).at[:, hh]
    o_ref = next(it)
    if emit_state:
        st_ref = next(it).at[:, hh]
    qb_ref, kb_ref, qd_ref, kd_ref, work_ref, sb_ref = (next(it).at[hh] for _ in range(6))
    lanes = slice(hh * LANES, (hh + 1) * LANES)
    q_ref, k_ref, v_ref, g_ref, o_ref = (r.at[:, lanes] for r in (q_ref, k_ref, v_ref, g_ref, o_ref))

    h = pl.program_id(1) * heads + hh
    C = RET_CHUNK
    n = S // C
    lg_f = lg_ref[l, 0, h]
    lg_b = lg_ref[l, 1, h]
    q = q_ref[...].astype(F32)
    k = k_ref[...].astype(F32) * (RET_DK ** -0.5)
    if rope:
        q = _rope(q, cos_ref[...], sin_ref[...])
        k = _rope(k, cos_ref[...], sin_ref[...])

    t = lax.broadcasted_iota(jnp.int32, (C, LANES), 0).astype(F32)

    def scaled(x, expo):
        return (x.reshape(n, C, LANES) * jnp.exp(expo)[None]).reshape(S, LANES).astype(BF16)

    qb_ref[...] = q.astype(BF16)
    kb_ref[...] = k.astype(BF16)
    qd_ref[:, 0:RET_DK] = scaled(q, (t + 1.0) * lg_f)
    qd_ref[:, RET_DK:] = scaled(q, (C - t) * lg_b)
    kd_ref[:, 0:RET_DK] = scaled(k, (C - 1.0 - t) * lg_f)
    kd_ref[:, RET_DK:] = scaled(k, t * lg_b)

    ii = lax.broadcasted_iota(jnp.int32, (C, C), 0).astype(F32)
    jj = lax.broadcasted_iota(jnp.int32, (C, C), 1).astype(F32)
    dmat = (jnp.where(ii >= jj, jnp.exp(jnp.maximum(ii - jj, 0.0) * lg_f), 0.0)
            + jnp.where(jj >= ii, jnp.exp(jnp.maximum(jj - ii, 0.0) * lg_b), 0.0))
    decay_f = jnp.exp(jnp.full((RET_DK, RET_DV), C * lg_f, F32))
    decay_b = jnp.exp(jnp.full((RET_DK, RET_DV), C * lg_b, F32))
    if has_s0:
        state0 = (s0_ref[0], s0_ref[1])
    else:
        state0 = (jnp.zeros((RET_DK, RET_DV), F32), jnp.zeros((RET_DK, RET_DV), F32))

    def rows(c):
        return pl.ds(pl.multiple_of(c * C, C), C)

    def state_updates(c, carry):
        sl = rows(c)
        update = _dot_tn(kd_ref[sl, :], v_ref[sl, :])
        work_ref[0, sl, :] = update[0:RET_DK, :]
        work_ref[1, sl, :] = update[RET_DK:, :]
        return carry

    lax.fori_loop(0, n, state_updates, 0, unroll=LOOP_UNROLL)

    def scan(c, carry):
        state_f, state_b = carry
        cb = n - 1 - c
        sb_ref[c, 0:RET_DK, :] = state_f.astype(BF16)
        sb_ref[cb, RET_DK:, :] = state_b.astype(BF16)
        return (decay_f * state_f + work_ref[0, rows(c), :], decay_b * state_b + work_ref[1, rows(cb), :])

    state_f, state_b = lax.fori_loop(0, n, scan, state0, unroll=LOOP_UNROLL)
    if emit_state:
        st_ref[0] = state_f
        st_ref[1] = state_b

    U = min(SCAN_GROUP, n)

    def outputs(j, carry):
        chunks = [j * U + u for u in range(U)]
        scores = [(_dot_nt(qb_ref[rows(c), :], kb_ref[rows(c), :]) * dmat).astype(BF16) for c in chunks]
        cross = [_dot(qd_ref[rows(c), :], sb_ref[c]) for c in chunks]
        for c, sc, x in zip(chunks, scores, cross):
            work_ref[0, rows(c), :] = _dot(sc, v_ref[rows(c), :]) + x
        return carry

    lax.fori_loop(0, n // U, outputs, 0)

    y = work_ref[0]
    mu = jnp.mean(y, axis=-1, keepdims=True)
    yc = y - mu
    var = jnp.mean(yc * yc, axis=-1, keepdims=True)
    gate = _silu(g_ref[...].astype(F32))
    o_ref[...] = (yc * lax.rsqrt(var + LN_EPS) * gate).astype(o_ref.dtype)


def _retention(P, B, S, log_decay, l, rope_tabs, s0, emit_state, heads):
    width = heads * LANES

    def col(c0):
        assert c0 % heads == 0
        return pl.BlockSpec((S, width), lambda b, h: (b, c0 // heads + h))

    in_specs = [pl.BlockSpec(memory_space=pltpu.SMEM), col(COL_RQ), col(COL_RK), col(COL_RV), col(COL_RG)]
    args = [log_decay, P, P, P, P]
    if rope_tabs is not None:
        tab = pl.BlockSpec((S, LANES), lambda b, h: (0, 0))
        in_specs += [tab, tab]
        args += list(rope_tabs)
    if s0 is not None:
        in_specs.append(pl.BlockSpec((None, None, 2, heads, RET_DK, RET_DV), lambda b, h: (b, l, 0, h, 0, 0)))
        args.append(s0)
    out_specs = [pl.BlockSpec((S, width), lambda b, h: (b, h))]
    out_shape = [jax.ShapeDtypeStruct((B * S, RET_WIDTH), BF16)]
    if emit_state:
        out_specs.append(pl.BlockSpec((None, 2, heads, RET_DK, RET_DV), lambda b, h: (b, 0, h, 0, 0)))
        out_shape.append(jax.ShapeDtypeStruct((B, 2, RET_HEADS, RET_DK, RET_DV), F32))
    return pl.pallas_call(
        functools.partial(_ret_kernel, heads=heads, l=l, rope=rope_tabs is not None, has_s0=s0 is not None,
                          emit_state=emit_state, S=S),
        grid=(B, RET_HEADS // heads),
        in_specs=in_specs,
        out_specs=out_specs,
        out_shape=out_shape,
        scratch_shapes=[pltpu.VMEM((heads, S, RET_DK), BF16), pltpu.VMEM((heads, S, RET_DK), BF16),
                        pltpu.VMEM((heads, S, 2 * RET_DK), BF16), pltpu.VMEM((heads, S, 2 * RET_DK), BF16),
                        pltpu.VMEM((heads, 2, S, RET_DV), F32),
                        pltpu.VMEM((heads, S // RET_CHUNK, 2 * RET_DK, RET_DV), BF16)],
        compiler_params=_params("parallel", "parallel"),
        name="retention",
    )(*args)


def _gla_kernel(*refs, pairs, **static):
    for pp in range(pairs):
        _gla_pair(pp, *refs, **static)


def _gla_pair(pp, *refs, has_s0, emit_state, S):
    it = iter(refs)
    q_ref, k_ref, v_ref, gr_ref, ga_ref, wa_ref, ba_ref, gn_ref = (next(it) for _ in range(8))
    if has_s0:
        s0_ref = next(it).at[:, pp]
    o_ref = next(it)
    if emit_state:
        st_ref = next(it).at[:, pp]
    la_ref, qt_ref, kcat_ref, dec_ref, work_ref, sb_ref = (next(it).at[pp] for _ in range(6))
    narrow = slice(pp * LANES, (pp + 1) * LANES)
    wide = slice(pp * 2 * GLA_DV, (pp + 1) * 2 * GLA_DV)
    q_ref, k_ref = q_ref.at[:, narrow], k_ref.at[:, narrow]
    v_ref, gr_ref, o_ref = v_ref.at[:, wide], gr_ref.at[:, wide], o_ref.at[:, wide]
    wa_ref, ba_ref = wa_ref.at[:, :, narrow], ba_ref.at[:, :, narrow]
    C = GLA_CHUNK
    n = S // C
    U = min(SCAN_GROUP, n)
    ga = ga_ref[...]
    for d in range(2):
        pre = _dot(ga, wa_ref[d]) + ba_ref[d]
        la_ref[:, d * LANES:(d + 1) * LANES] = (
            (jnp.minimum(pre, 0.0) - jnp.log1p(jnp.exp(-jnp.abs(pre)))) * (1.0 / GLA_TAU))

    ii = lax.broadcasted_iota(jnp.int32, (C, C), 0)
    jj = lax.broadcasted_iota(jnp.int32, (C, C), 1)
    lower_b = jnp.where(jj <= ii, 1.0, 0.0).astype(BF16)
    mask_f = jnp.concatenate([jnp.where(jj <= ii, 1.0, 0.0)] * 2, axis=1)
    mask_b = jnp.concatenate([jnp.where(jj >= ii, 1.0, 0.0)] * 2, axis=1)
    lane = lax.broadcasted_iota(jnp.int32, (C, LANES), 1)
    head0_lanes = lane < GLA_DK
    lane_sq = lax.broadcasted_iota(jnp.int32, (LANES, LANES), 1)
    row_sq = lax.broadcasted_iota(jnp.int32, (LANES, LANES), 0)
    lane_wide = lax.broadcasted_iota(jnp.int32, (C, 2 * GLA_DV), 1)

    def rows(c):
        return pl.ds(pl.multiple_of(c * C, C), C)

    def update_rows(c):
        return pl.ds(pl.multiple_of(c * GLA_DV, GLA_DV), GLA_DV)

    def prepare(j, carry):
        chunks = [j * U + u for u in range(U)]
        cums = []
        for c in chunks:
            la = la_ref[rows(c), :]
            hi = la.astype(BF16)
            r1 = la - hi.astype(F32)
            mid = r1.astype(BF16)
            lo = (r1 - mid.astype(F32)).astype(BF16)
            cums.append((la, _dot(lower_b, hi) + _dot(lower_b, mid) + _dot(lower_b, lo)))
        scaled_keys = []
        for c, (la, cum) in zip(chunks, cums):
            sl = rows(c)
            qc = q_ref[sl, :].astype(F32) * (GLA_DK ** -0.5)
            kc = k_ref[sl, :].astype(F32)
            b_f = cum[:, 0:LANES]
            cum_b = cum[:, LANES:]
            b_b = cum_b[C - 1:C, :] - cum_b + la[:, LANES:]
            end_f = b_f[C - 1:C, :]
            end_b = b_b[0:1, :]
            qt_ref[sl, 0:LANES] = (qc * jnp.exp(b_f)).astype(BF16)
            qt_ref[sl, LANES:] = (qc * jnp.exp(b_b)).astype(BF16)
            for d, b in ((0, b_f), (1, b_b)):
                k_t = kc * jnp.exp(-b)
                kcat_ref[d, c, 0:C, :] = jnp.where(head0_lanes, k_t, 0.0).astype(BF16)
                kcat_ref[d, c, C:, :] = jnp.where(head0_lanes, 0.0, k_t).astype(BF16)
            dec_ref[0, c] = jnp.exp(end_f)
            dec_ref[1, c] = jnp.exp(end_b)
            scaled_keys.append(jnp.concatenate(
                [kc * jnp.exp(end_f - b_f), kc * jnp.exp(end_b - b_b)], axis=1).astype(BF16))
        for c, k_s in zip(chunks, scaled_keys):
            kv = _dot_tn(v_ref[rows(c), :], k_s)
            for d in range(2):
                cols = kv[:, d * LANES:(d + 1) * LANES]
                work_ref[d, update_rows(c), :] = jnp.where(lane_sq < GLA_DK, cols[0:GLA_DV, :], cols[GLA_DV:, :])
        return carry

    lax.fori_loop(0, n // U, prepare, 0)

    if has_s0:
        state0 = (s0_ref[0].T, s0_ref[1].T)
    else:
        state0 = (jnp.zeros((GLA_DV, LANES), F32), jnp.zeros((GLA_DV, LANES), F32))

    def scan(c, carry):
        state_f, state_b = carry
        cb = n - 1 - c
        for d, cc, state in ((0, c, state_f), (1, cb, state_b)):
            s2 = state.T
            r0 = d * LANES
            sb_ref[cc, r0:r0 + LANES, 0:GLA_DV] = jnp.where(row_sq < GLA_DK, s2, 0.0).astype(BF16)
            sb_ref[cc, r0:r0 + LANES, GLA_DV:] = jnp.where(row_sq < GLA_DK, 0.0, s2).astype(BF16)
        return (state_f * dec_ref[0, c] + work_ref[0, update_rows(c), :],
                state_b * dec_ref[1, cb] + work_ref[1, update_rows(cb), :])

    states = lax.fori_loop(0, n, scan, state0, unroll=LOOP_UNROLL)
    if emit_state:
        st_ref[0] = states[0].T
        st_ref[1] = states[1].T

    def outputs(j, carry):
        chunks = [j * U + u for u in range(U)]
        scores, cross = [], []
        for c in chunks:
            q_t = qt_ref[rows(c), :]
            att = (_dot_nt(q_t[:, 0:LANES], kcat_ref[0, c]) * mask_f
                   + _dot_nt(q_t[:, LANES:], kcat_ref[1, c]) * mask_b)
            scores.append(att.astype(BF16))
            cross.append(_dot(q_t, sb_ref[c]))
        for c, att, x in zip(chunks, scores, cross):
            v_c = v_ref[rows(c), :]
            zeros = jnp.zeros_like(v_c)
            v_bd = jnp.concatenate([jnp.where(lane_wide < GLA_DV, v_c, zeros),
                                    jnp.where(lane_wide < GLA_DV, zeros, v_c)], axis=0)
            y = _dot(att, v_bd) + x
            for i in range(2):
                work_ref[0, pl.ds(pl.multiple_of(i * S + c * C, C), C), :] = y[:, i * GLA_DV:(i + 1) * GLA_DV]
        return carry

    lax.fori_loop(0, n // U, outputs, 0)

    for i in range(2):
        y = _rms_norm(work_ref[0, i * S:(i + 1) * S, :], gn_ref[...])
        gate = _silu(gr_ref[:, i * GLA_DV:(i + 1) * GLA_DV].astype(F32))
        o_ref[:, i * GLA_DV:(i + 1) * GLA_DV] = (y * gate).astype(o_ref.dtype)


def _gla(P, B, S, wa, ba, gla_norm, l, s0, emit_state, pairs):
    nw = pairs * LANES
    pw = pairs * 2 * GLA_DV
    assert COL_GQ % pairs == 0 and COL_GK % pairs == 0
    in_specs = [
        pl.BlockSpec((S, nw), lambda b, p: (b, COL_GQ // pairs + p)),
        pl.BlockSpec((S, nw), lambda b, p: (b, COL_GK // pairs + p)),
        pl.BlockSpec((S, pw), lambda b, p: (b, COL_GV // (2 * pairs) + p)),
        pl.BlockSpec((S, pw), lambda b, p: (b, COL_GR // (2 * pairs) + p)),
        pl.BlockSpec((S, LANES), lambda b, p: (b, COL_GA)),
        pl.BlockSpec((None, 2, LANES, nw), lambda b, p: (l, 0, 0, p)),
        pl.BlockSpec((None, 2, 1, nw), lambda b, p: (l, 0, 0, p)),
        pl.BlockSpec((None, 1, GLA_DV), lambda b, p: (l, 0, 0)),
    ]
    args = [P, P, P, P, P, wa, ba, gla_norm]
    if s0 is not None:
        in_specs.append(pl.BlockSpec((None, None, 2, pairs, LANES, GLA_DV), lambda b, p: (b, l, 0, p, 0, 0)))
        args.append(s0)
    out_specs = [pl.BlockSpec((S, pw), lambda b, p: (b, p))]
    out_shape = [jax.ShapeDtypeStruct((B * S, GLA_WIDTH), BF16)]
    if emit_state:
        out_specs.append(pl.BlockSpec((None, 2, pairs, LANES, GLA_DV), lambda b, p: (b, 0, p, 0, 0)))
        out_shape.append(jax.ShapeDtypeStruct((B, 2, GLA_PAIRS, LANES, GLA_DV), F32))
    return pl.pallas_call(
        functools.partial(_gla_kernel, pairs=pairs, has_s0=s0 is not None, emit_state=emit_state, S=S),
        grid=(B, GLA_PAIRS // pairs),
        in_specs=in_specs,
        out_specs=out_specs,
        out_shape=out_shape,
        scratch_shapes=[pltpu.VMEM((pairs, S, 2 * LANES), F32),
                        pltpu.VMEM((pairs, S, 2 * LANES), BF16),
                        pltpu.VMEM((pairs, 2, S // GLA_CHUNK, 2 * GLA_CHUNK, LANES), BF16),
                        pltpu.VMEM((pairs, 2, S // GLA_CHUNK, 1, LANES), F32),
                        pltpu.VMEM((pairs, 2, 2 * S, GLA_DV), F32),
                        pltpu.VMEM((pairs, S // GLA_CHUNK, 2 * LANES, 2 * GLA_DV), BF16)],
        compiler_params=_params("parallel", "parallel"),
        name="gla",
    )(*args)


def _outproj_kernel(x_ref, mod_ref, a_ref, r_ref, gl_ref, wa_ref, wr_ref, wg_ref, g_ref, b_ref, o_ref):
    rows = x_ref.shape[0] // OUTPROJ_SPLIT
    mixed = []
    for i in range(OUTPROJ_SPLIT):
        sl = slice(i * rows, (i + 1) * rows)
        mixed.append(_dot(a_ref[sl, :], wa_ref[...]) + _dot(r_ref[sl, :], wr_ref[...])
                     + _dot(gl_ref[sl, :], wg_ref[...]))
    for i in range(OUTPROJ_SPLIT):
        sl = slice(i * rows, (i + 1) * rows)
        y = DEEPNORM_ALPHA * x_ref[sl, :] + mod_ref[5:6, :] * mixed[i]
        o_ref[sl, :] = _layer_norm(y, g_ref[...], b_ref[...])


def _outproj(x, mod, who0, rows_per_mod, att, ret, gla, w, ln_g, ln_b, l):
    M, D = x.shape
    return pl.pallas_call(
        _outproj_kernel,
        grid=(M // ROW_TILE,),
        in_specs=[
            pl.BlockSpec((ROW_TILE, D), lambda m: (m, 0)),
            _mod_spec(l, who0, rows_per_mod, 1),
            pl.BlockSpec((ROW_TILE, ATT_WIDTH), lambda m: (m, 0)),
            pl.BlockSpec((ROW_TILE, RET_WIDTH), lambda m: (m, 0)),
            pl.BlockSpec((ROW_TILE, GLA_WIDTH), lambda m: (m, 0)),
            pl.BlockSpec((None, ATT_WIDTH, D), lambda m: (l, 0, 0)),
            pl.BlockSpec((None, RET_WIDTH, D), lambda m: (l, ATT_WIDTH // RET_WIDTH, 0)),
            pl.BlockSpec((None, GLA_WIDTH, D), lambda m: (l, (ATT_WIDTH + RET_WIDTH) // GLA_WIDTH, 0)),
            pl.BlockSpec((None, None, 1, D), lambda m: (l, 1, 0, 0)),
            pl.BlockSpec((None, None, 1, D), lambda m: (l, 1, 0, 0)),
        ],
        out_specs=pl.BlockSpec((ROW_TILE, D), lambda m: (m, 0)),
        out_shape=jax.ShapeDtypeStruct((M, D), F32),
        compiler_params=_params("parallel"),
        name="outproj",
    )(x, mod, att, ret, gla, w, w, w, ln_g, ln_b)


def _trunk_layer(x, B, S, l, who0, w, rope_tabs, ctx, attn_tq):
    is_context = ctx is None
    rows_per_mod = x.shape[0] if is_context else S
    mod = w["mod"]

    def ffn(x, half):
        nxt = (l, 1) if half == 0 else (l + 1, 0)
        convert = None
        if nxt[0] < DEPTH and nxt not in w["ffn_bf16"]:
            convert = (w["ffn_w_in"], w["ffn_w_out"], nxt[0], nxt[1])
        outs = _ffn(x, mod, who0, rows_per_mod, *w["ffn_bf16"][(l, half)], w["ln_g"], w["ln_b"], l, half, convert)
        if convert is not None:
            w["ffn_bf16"][nxt] = (outs[1], outs[2])
        return outs[0]

    x = ffn(x, 0)
    P = _inproj(x, mod, who0, rows_per_mod, w["mix_w_in"], l)
    cache = None if is_context else (ctx[0], ctx[1])
    s_ret0 = None if is_context else ctx[2]
    s_gla0 = None if is_context else ctx[3]
    att_out = _attention(P, B, S, w["q_norm"], w["k_norm"], l, rope_tabs, cache, is_context, attn_tq)
    ret_out = _retention(P, B, S, w["log_decay"], l, rope_tabs, s_ret0, is_context,
                         RET_HEADS if is_context else 1)
    gla_out = _gla(P, B, S, w["gla_wa"], w["gla_ba"], w["gla_norm"], l, s_gla0, is_context,
                   GLA_PAIRS if is_context else 1)
    x = _outproj(x, mod, who0, rows_per_mod, att_out[0], ret_out[0], gla_out[0],
                 w["mix_w_out"], w["ln_g"], w["ln_b"], l)
    x = ffn(x, 1)
    new_ctx = (att_out[1], att_out[2], ret_out[1], gla_out[1]) if is_context else None
    return x, new_ctx


def _rope_tables(rows):
    row = jnp.repeat(jnp.arange(rows, dtype=F32), GRID_W)
    col = jnp.tile(jnp.arange(GRID_W, dtype=F32), rows)
    n_freq = ATT_HEAD_DIM // 4
    inv = ROPE_THETA ** (-jnp.arange(n_freq, dtype=F32) / n_freq)
    ang = jnp.concatenate([row[:, None] * inv, col[:, None] * inv], axis=-1)
    cos, sin = jnp.cos(ang), jnp.sin(ang)
    cos_full = jnp.repeat(cos, 2, axis=-1)
    sin_signed = jnp.stack([-sin, sin], axis=-1).reshape(ang.shape[0], ATT_HEAD_DIM)
    return cos_full, sin_signed


def kernel(x_prompt, x_sample, c, cache_attn_k, cache_attn_v, state_ret, state_gla, c_ctx,
           w_mod, b_mod, ln_g, ln_b, ffn_w_in, ffn_w_out, mix_w_in, mix_w_out,
           att_q_norm, att_k_norm, ret_log_decay, gla_w_a2, gla_b_a, gla_norm):
    B_ctx, S_ctx, D = x_prompt.shape
    B_lat, S_lat, _ = x_sample.shape

    gla_wa = jnp.zeros((DEPTH, 2, LANES, GLA_HEADS * GLA_DK), BF16)
    for d in range(2):
        gla_wa = gla_wa.at[:, d, d * GLA_GATE_RANK:(d + 1) * GLA_GATE_RANK, :].set(gla_w_a2[:, d].astype(BF16))

    cvec = jnp.concatenate([c_ctx[None, :], c, jnp.zeros((MOD_ROWS - 1 - B_lat, D), F32)], axis=0)
    mod = _modulation(cvec, w_mod, b_mod[:, None, :]).reshape(DEPTH, MOD_ROWS, N_MOD, D)

    w = dict(
        mod=mod,
        ffn_w_in=ffn_w_in, ffn_w_out=ffn_w_out,
        ffn_bf16={(0, 0): (ffn_w_in[0, 0].astype(BF16), ffn_w_out[0, 0].astype(BF16))},
        mix_w_in=jnp.pad(mix_w_in.astype(BF16), ((0, 0), (0, 0), (0, PROJ_COLS_PADDED - PROJ_COLS))),
        mix_w_out=mix_w_out.astype(BF16),
        ln_g=ln_g[:, :, None, :], ln_b=ln_b[:, :, None, :],
        q_norm=att_q_norm[:, None, :], k_norm=att_k_norm[:, None, :],
        log_decay=ret_log_decay, gla_wa=gla_wa, gla_ba=gla_b_a[:, :, None, :],
        gla_norm=gla_norm[:, None, :])

    h = x_prompt.reshape(B_ctx * S_ctx, D)
    ks_l, vs_l, sr_l, sg_l = [], [], [], []
    for l in range(DEPTH):
        h, (k_l, v_l, s_r, s_g) = _trunk_layer(h, B_ctx, S_ctx, l, 0, w, None, None, S_ctx)
        ks_l.append(k_l.reshape(B_ctx, S_ctx, ATT_KV_HEADS, ATT_HEAD_DIM))
        vs_l.append(v_l.reshape(B_ctx, S_ctx, ATT_KV_HEADS, ATT_HEAD_DIM))
        sr_l.append(s_r)
        sg_l.append(s_g.reshape(B_ctx, 2, GLA_HEADS, GLA_DK, GLA_DV))
    y_prompt = h.reshape(B_ctx, S_ctx, D)

    rope_tabs = _rope_tables(S_lat // GRID_W)
    ctx = (cache_attn_k.transpose(0, 1, 3, 2, 4), cache_attn_v.transpose(0, 1, 3, 2, 4), state_ret,
           state_gla.reshape(B_lat, DEPTH, 2, GLA_PAIRS, LANES, GLA_DV))
    g = x_sample.reshape(B_lat * S_lat, D)
    for l in range(DEPTH):
        g, _ = _trunk_layer(g, B_lat, S_lat, l, 1, w, rope_tabs, ctx, 256)
    y_sample = g.reshape(B_lat, S_lat, D)

    return (y_prompt, y_sample, jnp.stack(ks_l, axis=1), jnp.stack(vs_l, axis=1),
            jnp.stack(sr_l, axis=1), jnp.stack(sg_l, axis=1))
```
